```python
import math
import jax, jax.numpy as jnp
from jax import lax
import numpy as np

D_MODEL = 1024
BATCH = 2
SEQ = 8192
DEPTH = 1

CHUNK = 64
BAND_PREV = 8
BAND = (BAND_PREV + 1) * CHUNK
REL_CLIP = 256
N_REL = REL_CLIP + CHUNK

W_A = D_MODEL // 2
HEAD_A = 64
H_A = W_A // HEAD_A
W_B = D_MODEL // 2
HEAD_B = 64
H_B = W_B // HEAD_B
DECAY_LORA = 64
A_LORA = 64
GATE_LORA = 128
D_FF = 4 * D_MODEL

ATT_COLS = 3 * W_A
RWKV_COLS = 3 * W_B + DECAY_LORA + A_LORA + GATE_LORA
GATE_COLS = 2 * D_MODEL
IN_COLS = ATT_COLS + RWKV_COLS + GATE_COLS

RMS_EPS = 1e-6
GN_EPS = HEAD_B * 1e-5
NEG_INF = -1e30

kernel_name = "chunked_attn_rwkv7_gated_hybrid"


def rms_norm(x, g):
    xf = x.astype(jnp.float32)
    y = xf * lax.rsqrt(jnp.mean(xf * xf, axis=-1, keepdims=True) + RMS_EPS)
    return (y * g.astype(jnp.float32)).astype(x.dtype)


def chunk_band_attention(q, k, v, rel_bias):
    b, s, h, d = q.shape
    nc = s // CHUNK
    qc = q.reshape(b, nc, CHUNK, h, d)
    pad = ((0, 0), (BAND_PREV * CHUNK, 0), (0, 0), (0, 0))
    kp = jnp.pad(k, pad).reshape(b, nc + BAND_PREV, CHUNK, h, d)
    vp = jnp.pad(v, pad).reshape(b, nc + BAND_PREV, CHUNK, h, d)
    kb = jnp.concatenate([kp[:, o:o + nc] for o in range(BAND_PREV + 1)], axis=2)
    vb = jnp.concatenate([vp[:, o:o + nc] for o in range(BAND_PREV + 1)], axis=2)
    scores = jnp.einsum('bnqhd,bnkhd->bnhqk', qc, kb).astype(jnp.float32) * (d ** -0.5)
    qi = jnp.arange(CHUNK)[:, None]
    kj = jnp.arange(BAND)[None, :]
    dist = qi + BAND_PREV * CHUNK - kj
    idx = jnp.clip(dist, -(CHUNK - 1), REL_CLIP) + (CHUNK - 1)
    bias = rel_bias.astype(jnp.float32)[:, idx]
    valid = (jnp.arange(nc)[:, None] - BAND_PREV + kj // CHUNK) >= 0
    scores = jnp.where(valid[None, :, None, None, :], scores + bias[None, None], NEG_INF)
    p = jax.nn.softmax(scores, axis=-1)
    out = jnp.einsum('bnhqk,bnkhd->bnqhd', p.astype(vb.dtype), vb)
    return out.reshape(b, s, h * d)


def rwkv7_time_mix(p, shift_mu, w0, w2, a0, a2, g2, k_k, k_a, r_k, ln_x_w, ln_x_b):
    b, s, _ = p.shape
    prev = jnp.pad(p, ((0, 0), (1, 0), (0, 0)))[:, :-1]
    p = p + (prev - p) * shift_mu
    splits = [W_B, 2 * W_B, 3 * W_B, 3 * W_B + DECAY_LORA, 3 * W_B + DECAY_LORA + A_LORA]
    r, k, v, cw, ca, cg = jnp.split(p, splits, axis=-1)
    w_log = -jax.nn.softplus(-(w0 + jnp.tanh(cw) @ w2)) - 0.5
    decay = jnp.exp(-jnp.exp(w_log.astype(jnp.float32)))
    a = jax.nn.sigmoid(a0 + ca @ a2)
    g = jax.nn.sigmoid(cg) @ g2

    def heads(t):
        return t.reshape(b, s, H_B, HEAD_B).astype(jnp.float32)

    kk = heads(k * k_k)
    kk = kk / jnp.maximum(jnp.sqrt(jnp.sum(kk * kk, axis=-1, keepdims=True)), 1e-12)
    k = k * (1.0 + (a - 1.0) * k_a)
    r_h, k_h, v_h, a_h, w_h = heads(r), heads(k), heads(v), heads(a), heads(decay)

    def step(state, inp):
        r_t, w_t, k_t, v_t, aa_t, bb_t = inp
        sa = jnp.einsum('bhij,bhj->bhi', state, aa_t)
        state = (state * w_t[:, :, None, :] + sa[..., None] * bb_t[:, :, None, :]
                 + v_t[..., None] * k_t[:, :, None, :])
        y = jnp.einsum('bhij,bhj->bhi', state, r_t)
        return state, y

    xs = tuple(jnp.moveaxis(t, 1, 0) for t in (r_h, w_h, k_h, v_h, -kk, kk * a_h))
    init = jnp.zeros((b, H_B, HEAD_B, HEAD_B), jnp.float32)
    _, ys = lax.scan(step, init, xs)
    y = jnp.moveaxis(ys, 0, 1)
    mu = jnp.mean(y, axis=-1, keepdims=True)
    var = jnp.mean(jnp.square(y - mu), axis=-1, keepdims=True)
    yn = (y - mu) * lax.rsqrt(var + GN_EPS)
    gn_w = ln_x_w.reshape(H_B, HEAD_B).astype(jnp.float32)
    gn_b = ln_x_b.reshape(H_B, HEAD_B).astype(jnp.float32)
    yn = yn * gn_w + gn_b
    bonus = jnp.sum(r_h * k_h * r_k.astype(jnp.float32), axis=-1, keepdims=True) * v_h
    out = (yn + bonus) * heads(g)
    return out.reshape(b, s, W_B).astype(p.dtype)


def setup_inputs(seed: int = 0) -> dict:
    key = jax.random.key(seed)
    ks = jax.random.split(key, 24)
    L = DEPTH

    def nrm(k, shape, scale):
        return jax.random.normal(k, shape, jnp.float32) * scale

    return {
        "x": nrm(ks[0], (BATCH, SEQ, D_MODEL), 1.0),
        "pre_mix_g": 1.0 + nrm(ks[1], (L, D_MODEL), 0.05),
        "w_in": nrm(ks[2], (L, D_MODEL, IN_COLS), D_MODEL ** -0.5),
        "gate_bias": nrm(ks[3], (L, GATE_COLS), 0.1),
        "rel_bias": nrm(ks[4], (L, H_A, N_REL), 0.5),
        "shift_mu": jax.random.uniform(ks[5], (L, RWKV_COLS), jnp.float32),
        "w0": jax.random.uniform(ks[6], (L, W_B), jnp.float32, -5.0, 0.5),
        "w2": nrm(ks[7], (L, DECAY_LORA, W_B), 0.1),
        "a0": nrm(ks[8], (L, W_B), 0.1),
        "a2": nrm(ks[9], (L, A_LORA, W_B), A_LORA ** -0.5),
        "g2": nrm(ks[10], (L, GATE_LORA, W_B), GATE_LORA ** -0.5),
        "k_k": 0.85 + nrm(ks[11], (L, W_B), 0.05),
        "k_a": 1.0 + nrm(ks[12], (L, W_B), 0.05),
        "r_k": nrm(ks[13], (L, H_B, HEAD_B), 0.1),
        "ln_x_w": 1.0 + nrm(ks[14], (L, W_B), 0.05),
        "ln_x_b": nrm(ks[15], (L, W_B), 0.05),
        "proj_a": nrm(ks[16], (L, W_A, D_MODEL), W_A ** -0.5),
        "proj_b": nrm(ks[17], (L, W_B, D_MODEL), W_B ** -0.5),
        "w_out": nrm(ks[18], (L, D_MODEL, D_MODEL), D_MODEL ** -0.5),
        "post_mix_g": 1.0 + nrm(ks[19], (L, D_MODEL), 0.05),
        "pre_ffn_g": 1.0 + nrm(ks[20], (L, D_MODEL), 0.05),
        "w_up": nrm(ks[21], (L, D_MODEL, D_FF), D_MODEL ** -0.5),
        "w_down": nrm(ks[22], (L, D_FF, D_MODEL), D_FF ** -0.5),
        "post_ffn_g": 1.0 + nrm(ks[23], (L, D_MODEL), 0.05),
    }


def reference(x, pre_mix_g, w_in, gate_bias, rel_bias, shift_mu, w0, w2, a0, a2, g2,
              k_k, k_a, r_k, ln_x_w, ln_x_b, proj_a, proj_b, w_out, post_mix_g,
              pre_ffn_g, w_up, w_down, post_ffn_g):
    b, s, _ = x.shape
    for l in range(DEPTH):
        h = rms_norm(x, pre_mix_g[l])
        proj = h @ w_in[l]
        p_att = proj[..., :ATT_COLS]
        p_rwkv = proj[..., ATT_COLS:ATT_COLS + RWKV_COLS]
        p_gate = proj[..., ATT_COLS + RWKV_COLS:] + gate_bias[l]

        q, k, v = jnp.split(p_att, 3, axis=-1)
        shp = (b, s, H_A, HEAD_A)
        y_a = chunk_band_attention(q.reshape(shp), k.reshape(shp), v.reshape(shp), rel_bias[l])
        y_b = rwkv7_time_mix(p_rwkv, shift_mu[l], w0[l], w2[l], a0[l], a2[l], g2[l],
                             k_k[l], k_a[l], r_k[l], ln_x_w[l], ln_x_b[l])

        gate_a = jax.nn.sigmoid(p_gate[..., :D_MODEL])
        gate_b = jax.nn.sigmoid(p_gate[..., D_MODEL:])
        merged = gate_a * (y_a @ proj_a[l]) + gate_b * (y_b @ proj_b[l])
        x = x + rms_norm(merged @ w_out[l], post_mix_g[l])

        hf = rms_norm(x, pre_ffn_g[l])
        u = jnp.square(jax.nn.relu(hf @ w_up[l])) @ w_down[l]
        x = x + rms_norm(u, post_ffn_g[l])
    return x
```

```python
import functools
import math

import jax
import jax.numpy as jnp
from jax import lax
from jax.experimental import pallas as pl
from jax.experimental.pallas import tpu as pltpu

BF = jnp.bfloat16
F32 = jnp.float32

D_MODEL = 1024
CHUNK = 64
BAND_PREV = 8
REL_CLIP = 256
W_A = D_MODEL // 2
HEAD = 64
N_HEAD = W_A // HEAD
W_B = D_MODEL // 2
DECAY_LORA = 64
A_LORA = 64
GATE_LORA = 128
D_FF = 4 * D_MODEL
ATT_COLS = 3 * W_A
RWKV_COLS = 3 * W_B + DECAY_LORA + A_LORA + GATE_LORA
GATE_COLS = 2 * D_MODEL
IN_COLS = ATT_COLS + RWKV_COLS + GATE_COLS
RMS_EPS = 1e-6
GN_EPS = HEAD * 1e-5
NEG_INF = -1e30

ROW_TILE = 512
ATT_Q_SUB = 128
ATT_K_WIN = ATT_Q_SUB + BAND_PREV * CHUNK
VMEM_LIMIT = 56 * 1024 * 1024


def _mm(a, b):
    return jnp.dot(a.astype(BF), b.astype(BF), preferred_element_type=F32)


def _mm_nt(a, b):
    return lax.dot_general(a.astype(BF), b.astype(BF), (((1,), (1,)), ((), ())),
                           preferred_element_type=F32)


def _mm_tn(a, b):
    return lax.dot_general(a.astype(BF), b.astype(BF), (((0,), (0,)), ((), ())),
                           preferred_element_type=F32)


def _rms(x, g):
    ms = jnp.mean(x * x, axis=-1, keepdims=True)
    return x * lax.rsqrt(ms + RMS_EPS) * g


def _inproj_kernel(x_ref, g_ref, w_ref, gb_ref, qkv_ref, pr_ref, gate_ref):
    h = _rms(x_ref[...], g_ref[...]).astype(BF)
    for c in range(0, ATT_COLS, 512):
        acc = jnp.dot(h, w_ref[:, c:c + 512], preferred_element_type=F32)
        if c < W_A:
            acc = acc * (HEAD ** -0.5)
        qkv_ref[:, c:c + 512] = acc.astype(BF)
    for c in range(0, RWKV_COLS, 256):
        o = ATT_COLS + c
        pr_ref[:, c:c + 256] = jnp.dot(h, w_ref[:, o:o + 256], preferred_element_type=F32)
    for c in range(0, GATE_COLS, 512):
        o = ATT_COLS + RWKV_COLS + c
        z = jnp.dot(h, w_ref[:, o:o + 512], preferred_element_type=F32) + gb_ref[:, c:c + 512]
        gate_ref[:, c:c + 512] = jax.nn.sigmoid(z).astype(BF)


def _inproj(x2, g, w_bf, gate_bias):
    n = x2.shape[0]
    const = lambda i: (0, 0)
    row = lambda i: (i, 0)
    return pl.pallas_call(
        _inproj_kernel,
        grid=(n // ROW_TILE,),
        in_specs=[
            pl.BlockSpec((ROW_TILE, D_MODEL), row),
            pl.BlockSpec((1, D_MODEL), const),
            pl.BlockSpec((D_MODEL, IN_COLS), const),
            pl.BlockSpec((1, GATE_COLS), const),
        ],
        out_specs=[
            pl.BlockSpec((ROW_TILE, ATT_COLS), row),
            pl.BlockSpec((ROW_TILE, RWKV_COLS), row),
            pl.BlockSpec((ROW_TILE, GATE_COLS), row),
        ],
        out_shape=[
            jax.ShapeDtypeStruct((n, ATT_COLS), BF),
            jax.ShapeDtypeStruct((n, RWKV_COLS), F32),
            jax.ShapeDtypeStruct((n, GATE_COLS), BF),
        ],
        compiler_params=pltpu.CompilerParams(
            dimension_semantics=("arbitrary",), vmem_limit_bytes=VMEM_LIMIT),
        name="inproj",
    )(x2, g, w_bf, gate_bias)


def _attn_kernel(q_ref, kp_ref, kc_ref, vp_ref, vc_ref, bias_ref, o_ref, kwin, vwin):
    i = pl.program_id(1)
    kwin[0:ROW_TILE, :] = kp_ref[...]
    kwin[ROW_TILE:2 * ROW_TILE, :] = kc_ref[...]
    vwin[0:ROW_TILE, :] = vp_ref[...]
    vwin[ROW_TILE:2 * ROW_TILE, :] = vc_ref[...]
    lane = lax.broadcasted_iota(jnp.int32, (ATT_Q_SUB, 2 * HEAD), 1)
    col = lax.broadcasted_iota(jnp.int32, (ATT_Q_SUB, ATT_K_WIN), 1)
    for j in range(ROW_TILE // ATT_Q_SUB):
        r0 = j * ATT_Q_SUB
        key_ok = jnp.logical_or(i > 0, col >= ROW_TILE - r0)
        for p in range(N_HEAD // 2):
            c0 = p * 2 * HEAD
            q2 = q_ref[r0:r0 + ATT_Q_SUB, c0:c0 + 2 * HEAD]
            k2 = kwin[r0:r0 + ATT_K_WIN, c0:c0 + 2 * HEAD]
            v2 = vwin[r0:r0 + ATT_K_WIN, c0:c0 + 2 * HEAD]
            outs = []
            for e in range(2):
                in_head = (lane >= HEAD) if e else (lane < HEAD)
                qm = jnp.where(in_head, q2, jnp.zeros_like(q2))
                s = lax.dot_general(qm, k2, (((1,), (1,)), ((), ())),
                                    preferred_element_type=F32)
                s = s + bias_ref[2 * p + e]
                s = jnp.where(key_ok, s, NEG_INF)
                m = jnp.max(s, axis=-1, keepdims=True)
                ex = jnp.exp(s - m)
                l = jnp.sum(ex, axis=-1, keepdims=True)
                o = jnp.dot(ex.astype(BF), v2, preferred_element_type=F32)
                outs.append(o / l)
            o = jnp.where(lane < HEAD, outs[0], outs[1])
            o_ref[r0:r0 + ATT_Q_SUB, c0:c0 + 2 * HEAD] = o.astype(BF)


def _attn_bias_tile(rel_bias):
    qi = jnp.arange(ATT_Q_SUB)[:, None]
    kj = jnp.arange(ATT_K_WIN)[None, :]
    dist = qi + BAND_PREV * CHUNK - kj
    idx = jnp.clip(dist, -(CHUNK - 1), REL_CLIP) + (CHUNK - 1)
    dchunk = (BAND_PREV + qi // CHUNK) - kj // CHUNK
    valid = jnp.logical_and(dchunk >= 0, dchunk <= BAND_PREV)
    bias = rel_bias.astype(F32)[:, idx]
    return jnp.where(valid[None], bias, NEG_INF)


def _attention(qkv, bias_tile, b, s):
    n = b * s
    nblk = s // ROW_TILE
    blk = (ROW_TILE, W_A)
    return pl.pallas_call(
        _attn_kernel,
        grid=(b, nblk),
        in_specs=[
            pl.BlockSpec(blk, lambda bi, i: (bi * nblk + i, 0)),
            pl.BlockSpec(blk, lambda bi, i: (bi * nblk + jnp.maximum(i - 1, 0), 1)),
            pl.BlockSpec(blk, lambda bi, i: (bi * nblk + i, 1)),
            pl.BlockSpec(blk, lambda bi, i: (bi * nblk + jnp.maximum(i - 1, 0), 2)),
            pl.BlockSpec(blk, lambda bi, i: (bi * nblk + i, 2)),
            pl.BlockSpec((N_HEAD, ATT_Q_SUB, ATT_K_WIN), lambda bi, i: (0, 0, 0)),
        ],
        out_specs=pl.BlockSpec(blk, lambda bi, i: (bi * nblk + i, 0)),
        out_shape=jax.ShapeDtypeStruct((n, W_A), BF),
        scratch_shapes=[pltpu.VMEM((2 * ROW_TILE, W_A), BF),
                        pltpu.VMEM((2 * ROW_TILE, W_A), BF)],
        compiler_params=pltpu.CompilerParams(
            dimension_semantics=("arbitrary", "arbitrary"), vmem_limit_bytes=VMEM_LIMIT),
        name="band_attn",
    )(qkv, qkv, qkv, qkv, qkv, bias_tile)


def _rwkv_kernel(p_ref, mu_ref, w0_ref, w2_ref, a0_ref, a2_ref, g2_ref, kk_ref, ka_ref,
                 rk_ref, lnw_ref, lnb_ref, bd_ref, tri_ref, o_ref,
                 carry_ref, state_ref, y_scr):
    t = pl.program_id(1)

    @pl.when(t == 0)
    def _():
        carry_ref[...] = jnp.zeros_like(carry_ref)
        state_ref[...] = jnp.zeros_like(state_ref)

    c = CHUNK
    p = p_ref[...]
    row = lax.broadcasted_iota(jnp.int32, p.shape, 0)
    prev = jnp.where(row == 0, jnp.broadcast_to(carry_ref[...], p.shape),
                     pltpu.roll(p, 1, 0))
    carry_ref[...] = p[c - 1:c, :]
    ps = p + (prev - p) * mu_ref[...]

    r = ps[:, 0:W_B]
    k = ps[:, W_B:2 * W_B]
    v = ps[:, 2 * W_B:3 * W_B]
    lora_in = ps[:, 3 * W_B:3 * W_B + DECAY_LORA + A_LORA]
    cg = ps[:, 3 * W_B + DECAY_LORA + A_LORA:]

    u = w0_ref[...] + _mm(jnp.tanh(lora_in), w2_ref[...])
    lw = -math.exp(-0.5) * jax.nn.sigmoid(u)
    a = jax.nn.sigmoid(a0_ref[...] + _mm(lora_in, a2_ref[...]))
    g = _mm(jax.nn.sigmoid(cg), g2_ref[...])

    kk = k * kk_ref[...]
    sq = kk * kk
    sq_hi = sq.astype(BF)
    sq_lo = (sq - sq_hi.astype(F32)).astype(BF)
    ss = (jnp.dot(sq_hi, bd_ref[...], preferred_element_type=F32)
          + jnp.dot(sq_lo, bd_ref[...], preferred_element_type=F32))
    kk = kk * lax.rsqrt(jnp.maximum(ss, 1e-24))
    k = k * (1.0 + (a - 1.0) * ka_ref[...])
    aa = -kk
    bb = kk * a

    lw_hi = lw.astype(BF)
    rem = lw - lw_hi.astype(F32)
    lw_mid = rem.astype(BF)
    lw_lo = (rem - lw_mid.astype(F32)).astype(BF)
    tri = tri_ref[...]
    cum = (jnp.dot(tri, lw_hi, preferred_element_type=F32)
           + jnp.dot(tri, lw_mid, preferred_element_type=F32)
           + jnp.dot(tri, lw_lo, preferred_element_type=F32))
    last = cum[c - 1:c, :]
    inv = jnp.exp(-cum)
    tail = jnp.exp(last - cum)
    at = aa * jnp.exp(cum - lw)
    bt = bb * inv
    kt = k * inv
    rt = r * jnp.exp(cum)
    bh = bb * tail
    kh = k * tail
    w_end = jnp.exp(last)
    bonus_rk = r * k * rk_ref[...]

    ri = lax.broadcasted_iota(jnp.int32, (c, c), 0)
    ci = lax.broadcasted_iota(jnp.int32, (c, c), 1)
    strict = ri > ci
    incl = ri >= ci
    eye = (ri == ci).astype(F32)

    for h in range(N_HEAD):
        sl = slice(h * HEAD, (h + 1) * HEAD)
        at_h, bt_h, kt_h, rt_h, v_h = at[:, sl], bt[:, sl], kt[:, sl], rt[:, sl], v[:, sl]
        a_ab = jnp.where(strict, _mm_nt(at_h, bt_h), 0.0)
        a_ak = jnp.where(strict, _mm_nt(at_h, kt_h), 0.0)
        a_rb = jnp.where(incl, _mm_nt(rt_h, bt_h), 0.0)
        a_rk = jnp.where(incl, _mm_nt(rt_h, kt_h), 0.0)
        tinv = eye + a_ab
        pw = a_ab
        for _ in range(int(math.log2(c)) - 1):
            pw = _mm(pw, pw)
            tinv = tinv + _mm(pw, tinv)
        x_h = _mm(a_ak, v_h)
        ah = _mm(tinv, at_h)
        uh = _mm(tinv, x_h)
        rh = rt_h + _mm(a_rb, ah)
        yh = _mm(a_rb, uh) + _mm(a_rk, v_h)
        s0 = state_ref[h]
        y = _mm(rh, s0) + yh
        u_full = _mm(ah, s0) + uh
        w_col = jnp.sum(eye * w_end[:, sl], axis=1, keepdims=True)
        state_ref[h] = w_col * s0 + _mm_tn(bh[:, sl], u_full) + _mm_tn(kh[:, sl], v_h)

        mean = jnp.mean(y, axis=-1, keepdims=True)
        yc = y - mean
        var = jnp.mean(yc * yc, axis=-1, keepdims=True)
        yn = yc * lax.rsqrt(var + GN_EPS) * lnw_ref[:, sl] + lnb_ref[:, sl]
        bonus = jnp.sum(bonus_rk[:, sl], axis=-1, keepdims=True) * v_h
        y_scr[:, sl] = (yn + bonus) * g[:, sl]
    o_ref[...] = y_scr[...].astype(BF)


def _rwkv(p_rwkv, prm, b, s):
    n = b * s
    nchunk = s // CHUNK
    const2 = lambda bi, t: (0, 0)
    vec = pl.BlockSpec((1, W_B), const2)
    lora = DECAY_LORA + A_LORA
    return pl.pallas_call(
        _rwkv_kernel,
        grid=(b, nchunk),
        in_specs=[
            pl.BlockSpec((CHUNK, RWKV_COLS), lambda bi, t: (bi * nchunk + t, 0)),
            pl.BlockSpec((1, RWKV_COLS), const2),
            vec,
            pl.BlockSpec((lora, W_B), const2),
            vec,
            pl.BlockSpec((lora, W_B), const2),
            pl.BlockSpec((GATE_LORA, W_B), const2),
            vec, vec, vec, vec, vec,
            pl.BlockSpec((W_B, W_B), const2),
            pl.BlockSpec((CHUNK, CHUNK), const2),
        ],
        out_specs=pl.BlockSpec((CHUNK, W_B), lambda bi, t: (bi * nchunk + t, 0)),
        out_shape=jax.ShapeDtypeStruct((n, W_B), BF),
        scratch_shapes=[pltpu.VMEM((1, RWKV_COLS), F32),
                        pltpu.VMEM((N_HEAD, HEAD, HEAD), F32),
                        pltpu.VMEM((CHUNK, W_B), F32)],
        compiler_params=pltpu.CompilerParams(
            dimension_semantics=("arbitrary", "arbitrary"), vmem_limit_bytes=VMEM_LIMIT),
        name="rwkv7",
    )(p_rwkv, prm["mu"], prm["w0"], prm["w2"], prm["a0"], prm["a2"], prm["g2"],
      prm["k_k"], prm["k_a"], prm["r_k"], prm["ln_w"], prm["ln_b"], prm["bd"], prm["tri"])


def _merge_kernel(ya_ref, yb_ref, gate_ref, x_ref, pa_ref, pb_ref, wo_ref, g_ref, o_ref):
    ma = jnp.dot(ya_ref[...], pa_ref[...], preferred_element_type=F32)
    mb = jnp.dot(yb_ref[...], pb_ref[...], preferred_element_type=F32)
    merged = (gate_ref[:, 0:D_MODEL].astype(F32) * ma
              + gate_ref[:, D_MODEL:2 * D_MODEL].astype(F32) * mb)
    z = jnp.dot(merged.astype(BF), wo_ref[...], preferred_element_type=F32)
    o_ref[...] = x_ref[...] + _rms(z, g_ref[...])


def _merge(ya, yb, gates, x2, pa, pb, wo, g):
    n = x2.shape[0]
    const = lambda i: (0, 0)
    row = lambda i: (i, 0)
    return pl.pallas_call(
        _merge_kernel,
        grid=(n // ROW_TILE,),
        in_specs=[
            pl.BlockSpec((ROW_TILE, W_A), row),
            pl.BlockSpec((ROW_TILE, W_B), row),
            pl.BlockSpec((ROW_TILE, GATE_COLS), row),
            pl.BlockSpec((ROW_TILE, D_MODEL), row),
            pl.BlockSpec((W_A, D_MODEL), const),
            pl.BlockSpec((W_B, D_MODEL), const),
            pl.BlockSpec((D_MODEL, D_MODEL), const),
            pl.BlockSpec((1, D_MODEL), const),
        ],
        out_specs=pl.BlockSpec((ROW_TILE, D_MODEL), row),
        out_shape=jax.ShapeDtypeStruct((n, D_MODEL), F32),
        compiler_params=pltpu.CompilerParams(
            dimension_semantics=("arbitrary",), vmem_limit_bytes=VMEM_LIMIT),
        name="merge_out",
    )(ya, yb, gates, x2, pa, pb, wo, g)


FF_TILE = 1024


def _ffn_kernel(x_ref, g1_ref, wu_ref, wd_ref, g2_ref, o_ref):
    x = x_ref[...]
    hf = _rms(x, g1_ref[...]).astype(BF)
    acc = jnp.zeros(x.shape, F32)
    for c in range(0, D_FF, FF_TILE):
        u = jnp.dot(hf, wu_ref[:, c:c + FF_TILE], preferred_element_type=F32)
        u = jnp.maximum(u, 0.0)
        u = (u * u).astype(BF)
        acc = acc + jnp.dot(u, wd_ref[c:c + FF_TILE, :], preferred_element_type=F32)
    o_ref[...] = x + _rms(acc, g2_ref[...])


def _ffn(x2, g1, wu, wd, g2):
    n = x2.shape[0]
    const = lambda i: (0, 0)
    row = lambda i: (i, 0)
    return pl.pallas_call(
        _ffn_kernel,
        grid=(n // ROW_TILE,),
        in_specs=[
            pl.BlockSpec((ROW_TILE, D_MODEL), row),
            pl.BlockSpec((1, D_MODEL), const),
            pl.BlockSpec((D_MODEL, D_FF), const),
            pl.BlockSpec((D_FF, D_MODEL), const),
            pl.BlockSpec((1, D_MODEL), const),
        ],
        out_specs=pl.BlockSpec((ROW_TILE, D_MODEL), row),
        out_shape=jax.ShapeDtypeStruct((n, D_MODEL), F32),
        compiler_params=pltpu.CompilerParams(
            dimension_semantics=("arbitrary",), vmem_limit_bytes=VMEM_LIMIT),
        name="ffn",
    )(x2, g1, wu, wd, g2)


def _layer(x2, b, s, lp):
    row = lambda a: a.reshape(1, -1).astype(F32)
    qkv, p_rwkv, gates = _inproj(x2, row(lp["pre_mix_g"]), lp["w_in"].astype(BF),
                                 row(lp["gate_bias"]))
    ya = _attention(qkv, _attn_bias_tile(lp["rel_bias"]), b, s)

    zeros_lora = jnp.zeros((DECAY_LORA, W_B), BF)
    blk = jnp.arange(W_B) // HEAD
    tri = jnp.arange(CHUNK)
    prm = {
        "mu": row(lp["shift_mu"]),
        "w0": row(lp["w0"]),
        "w2": jnp.concatenate([lp["w2"].astype(BF), zeros_lora], axis=0),
        "a0": row(lp["a0"]),
        "a2": jnp.concatenate([zeros_lora, lp["a2"].astype(BF)], axis=0),
        "g2": lp["g2"].astype(BF),
        "k_k": row(lp["k_k"]),
        "k_a": row(lp["k_a"]),
        "r_k": row(lp["r_k"]),
        "ln_w": row(lp["ln_x_w"]),
        "ln_b": row(lp["ln_x_b"]),
        "bd": (blk[:, None] == blk[None, :]).astype(BF),
        "tri": (tri[:, None] >= tri[None, :]).astype(BF),
    }
    yb = _rwkv(p_rwkv, prm, b, s)

    x2 = _merge(ya, yb, gates, x2, lp["proj_a"].astype(BF), lp["proj_b"].astype(BF),
                lp["w_out"].astype(BF), row(lp["post_mix_g"]))
    x2 = _ffn(x2, row(lp["pre_ffn_g"]), lp["w_up"].astype(BF), lp["w_down"].astype(BF),
              row(lp["post_ffn_g"]))
    return x2


@jax.jit
def _forward(x, params):
    b, s, d = x.shape
    assert d == D_MODEL and s % ROW_TILE == 0
    x2 = x.reshape(b * s, d)
    depth = params["w_in"].shape[0]
    for l in range(depth):
        lp = {name: val[l] for name, val in params.items()}
        x2 = _layer(x2, b, s, lp)
    return x2.reshape(b, s, d)


def kernel(x, pre_mix_g, w_in, gate_bias, rel_bias, shift_mu, w0, w2, a0, a2, g2, k_k, k_a, r_k, ln_x_w, ln_x_b, proj_a, proj_b, w_out, post_mix_g, pre_ffn_g, w_up, w_down, post_ffn_g):
    params = dict(pre_mix_g=pre_mix_g, w_in=w_in, gate_bias=gate_bias, rel_bias=rel_bias,
                  shift_mu=shift_mu, w0=w0, w2=w2, a0=a0, a2=a2, g2=g2, k_k=k_k, k_a=k_a,
                  r_k=r_k, ln_x_w=ln_x_w, ln_x_b=ln_x_b, proj_a=proj_a, proj_b=proj_b,
                  w_out=w_out, post_mix_g=post_mix_g, pre_ffn_g=pre_ffn_g, w_up=w_up,
                  w_down=w_down, post_ffn_g=post_ffn_g)
    return _forward(x, params)
```

```python
import functools
import math

import jax
import jax.numpy as jnp
from jax import lax
from jax.experimental import pallas as pl
from jax.experimental.pallas import tpu as pltpu

BF = jnp.bfloat16
F32 = jnp.float32

D_MODEL = 1024
CHUNK = 64
BAND_PREV = 8
REL_CLIP = 256
W_A = D_MODEL // 2
HEAD = 64
N_HEAD = W_A // HEAD
W_B = D_MODEL // 2
DECAY_LORA = 64
A_LORA = 64
GATE_LORA = 128
D_FF = 4 * D_MODEL
ATT_COLS = 3 * W_A
RWKV_COLS = 3 * W_B + DECAY_LORA + A_LORA + GATE_LORA
GATE_COLS = 2 * D_MODEL
IN_COLS = ATT_COLS + RWKV_COLS + GATE_COLS
RMS_EPS = 1e-6
GN_EPS = HEAD * 1e-5
NEG_INF = -1e30

ROW_TILE = 512
ATT_Q_SUB = 128
ATT_K_WIN = ATT_Q_SUB + BAND_PREV * CHUNK
ATT_BASE = ATT_K_WIN + ATT_Q_SUB
VMEM_LIMIT = 56 * 1024 * 1024


def _mm(a, b):
    return jnp.dot(a.astype(BF), b.astype(BF), preferred_element_type=F32)


def _mm_nt(a, b):
    return lax.dot_general(a.astype(BF), b.astype(BF), (((1,), (1,)), ((), ())),
                           preferred_element_type=F32)


def _mm_tn(a, b):
    return lax.dot_general(a.astype(BF), b.astype(BF), (((0,), (0,)), ((), ())),
                           preferred_element_type=F32)


def _rms(x, g):
    ms = jnp.mean(x * x, axis=-1, keepdims=True)
    return x * lax.rsqrt(ms + RMS_EPS) * g


def _inproj_kernel(x_ref, g_ref, w_ref, gb_ref, qkv_ref, pr_ref, gate_ref):
    h = _rms(x_ref[...], g_ref[...]).astype(BF)
    for c in range(0, ATT_COLS, 512):
        acc = jnp.dot(h, w_ref[:, c:c + 512], preferred_element_type=F32)
        if c < W_A:
            acc = acc * (HEAD ** -0.5)
        qkv_ref[:, c:c + 512] = acc.astype(BF)
    for c in range(0, RWKV_COLS, 256):
        o = ATT_COLS + c
        pr_ref[:, c:c + 256] = jnp.dot(h, w_ref[:, o:o + 256], preferred_element_type=F32)
    for c in range(0, GATE_COLS, 512):
        o = ATT_COLS + RWKV_COLS + c
        z = jnp.dot(h, w_ref[:, o:o + 512], preferred_element_type=F32) + gb_ref[:, c:c + 512]
        gate_ref[:, c:c + 512] = jax.nn.sigmoid(z).astype(BF)


def _inproj(x2, g, w_bf, gate_bias):
    n = x2.shape[0]
    const = lambda i: (0, 0)
    row = lambda i: (i, 0)
    return pl.pallas_call(
        _inproj_kernel,
        grid=(n // ROW_TILE,),
        in_specs=[
            pl.BlockSpec((ROW_TILE, D_MODEL), row),
            pl.BlockSpec((1, D_MODEL), const),
            pl.BlockSpec((D_MODEL, IN_COLS), const),
            pl.BlockSpec((1, GATE_COLS), const),
        ],
        out_specs=[
            pl.BlockSpec((ROW_TILE, ATT_COLS), row),
            pl.BlockSpec((ROW_TILE, RWKV_COLS), row),
            pl.BlockSpec((ROW_TILE, GATE_COLS), row),
        ],
        out_shape=[
            jax.ShapeDtypeStruct((n, ATT_COLS), BF),
            jax.ShapeDtypeStruct((n, RWKV_COLS), F32),
            jax.ShapeDtypeStruct((n, GATE_COLS), BF),
        ],
        compiler_params=pltpu.CompilerParams(
            dimension_semantics=("arbitrary",), vmem_limit_bytes=VMEM_LIMIT),
        name="inproj",
    )(x2, g, w_bf, gate_bias)


def _attn_kernel(q_ref, kp_ref, kc_ref, vp_ref, vc_ref, base_ref, o_ref, kwin, vwin, bias_scr):
    i = pl.program_id(1)

    @pl.when(jnp.logical_and(pl.program_id(0) == 0, i == 0))
    def _():
        qi = lax.broadcasted_iota(jnp.int32, (ATT_Q_SUB, ATT_K_WIN), 0)
        kj = lax.broadcasted_iota(jnp.int32, (ATT_Q_SUB, ATT_K_WIN), 1)
        dchunk = (BAND_PREV + qi // CHUNK) - kj // CHUNK
        band = jnp.logical_and(dchunk >= 0, dchunk <= BAND_PREV)
        for h in range(N_HEAD):
            rows = jnp.broadcast_to(base_ref[h:h + 1, :], (ATT_Q_SUB, ATT_BASE))
            toep = pltpu.roll(rows, 0, 1, stride=1, stride_axis=0)
            bias_scr[h] = jnp.where(band, toep[:, 0:ATT_K_WIN], NEG_INF)

    kwin[0:ROW_TILE, :] = kp_ref[...]
    kwin[ROW_TILE:2 * ROW_TILE, :] = kc_ref[...]
    vwin[0:ROW_TILE, :] = vp_ref[...]
    vwin[ROW_TILE:2 * ROW_TILE, :] = vc_ref[...]
    lane = lax.broadcasted_iota(jnp.int32, (ATT_Q_SUB, 2 * HEAD), 1)
    col = lax.broadcasted_iota(jnp.int32, (ATT_Q_SUB, ATT_K_WIN), 1)
    for j in range(ROW_TILE // ATT_Q_SUB):
        r0 = j * ATT_Q_SUB
        key_ok = jnp.logical_or(i > 0, col >= ROW_TILE - r0)
        for p in range(N_HEAD // 2):
            c0 = p * 2 * HEAD
            q2 = q_ref[r0:r0 + ATT_Q_SUB, c0:c0 + 2 * HEAD]
            k2 = kwin[r0:r0 + ATT_K_WIN, c0:c0 + 2 * HEAD]
            v2 = vwin[r0:r0 + ATT_K_WIN, c0:c0 + 2 * HEAD]
            outs = []
            for e in range(2):
                in_head = (lane >= HEAD) if e else (lane < HEAD)
                qm = jnp.where(in_head, q2, jnp.zeros_like(q2))
                s = lax.dot_general(qm, k2, (((1,), (1,)), ((), ())),
                                    preferred_element_type=F32)
                s = s + bias_scr[2 * p + e]
                s = jnp.where(key_ok, s, NEG_INF)
                m = jnp.max(s, axis=-1, keepdims=True)
                ex = jnp.exp(s - m)
                l = jnp.sum(ex, axis=-1, keepdims=True)
                o = jnp.dot(ex.astype(BF), v2, preferred_element_type=F32)
                outs.append(o / l)
            o = jnp.where(lane < HEAD, outs[0], outs[1])
            o_ref[r0:r0 + ATT_Q_SUB, c0:c0 + 2 * HEAD] = o.astype(BF)


def _attn_bias_base(rel_bias):
    pos = jnp.arange(ATT_BASE)
    d = jnp.where(pos < ATT_K_WIN, pos, pos - ATT_BASE)
    idx = jnp.clip(BAND_PREV * CHUNK - d, -(CHUNK - 1), REL_CLIP) + (CHUNK - 1)
    return rel_bias.astype(F32)[:, idx]


def _attention(qkv, bias_base, b, s):
    n = b * s
    nblk = s // ROW_TILE
    blk = (ROW_TILE, W_A)
    return pl.pallas_call(
        _attn_kernel,
        grid=(b, nblk),
        in_specs=[
            pl.BlockSpec(blk, lambda bi, i: (bi * nblk + i, 0)),
            pl.BlockSpec(blk, lambda bi, i: (bi * nblk + jnp.maximum(i - 1, 0), 1)),
            pl.BlockSpec(blk, lambda bi, i: (bi * nblk + i, 1)),
            pl.BlockSpec(blk, lambda bi, i: (bi * nblk + jnp.maximum(i - 1, 0), 2)),
            pl.BlockSpec(blk, lambda bi, i: (bi * nblk + i, 2)),
            pl.BlockSpec((N_HEAD, ATT_BASE), lambda bi, i: (0, 0)),
        ],
        out_specs=pl.BlockSpec(blk, lambda bi, i: (bi * nblk + i, 0)),
        out_shape=jax.ShapeDtypeStruct((n, W_A), BF),
        scratch_shapes=[pltpu.VMEM((2 * ROW_TILE, W_A), BF),
                        pltpu.VMEM((2 * ROW_TILE, W_A), BF),
                        pltpu.VMEM((N_HEAD, ATT_Q_SUB, ATT_K_WIN), F32)],
        compiler_params=pltpu.CompilerParams(
            dimension_semantics=("arbitrary", "arbitrary"), vmem_limit_bytes=VMEM_LIMIT),
        name="band_attn",
    )(qkv, qkv, qkv, qkv, qkv, bias_base)


def _rwkv_kernel(p_ref, mu_ref, w0_ref, w2_ref, a0_ref, a2_ref, g2_ref, kk_ref, ka_ref,
                 rk_ref, lnw_ref, lnb_ref, bd_ref, tri_ref, o_ref,
                 carry_ref, state_ref, y_scr):
    t = pl.program_id(1)

    @pl.when(t == 0)
    def _():
        carry_ref[...] = jnp.zeros_like(carry_ref)
        state_ref[...] = jnp.zeros_like(state_ref)

    c = CHUNK
    p = p_ref[...]
    row = lax.broadcasted_iota(jnp.int32, p.shape, 0)
    prev = jnp.where(row == 0, jnp.broadcast_to(carry_ref[...], p.shape),
                     pltpu.roll(p, 1, 0))
    carry_ref[...] = p[c - 1:c, :]
    ps = p + (prev - p) * mu_ref[...]

    r = ps[:, 0:W_B]
    k = ps[:, W_B:2 * W_B]
    v = ps[:, 2 * W_B:3 * W_B]
    lora_in = ps[:, 3 * W_B:3 * W_B + DECAY_LORA + A_LORA]
    cg = ps[:, 3 * W_B + DECAY_LORA + A_LORA:]

    u = w0_ref[...] + _mm(jnp.tanh(lora_in), w2_ref[...])
    lw = -math.exp(-0.5) * jax.nn.sigmoid(u)
    a = jax.nn.sigmoid(a0_ref[...] + _mm(lora_in, a2_ref[...]))
    g = _mm(jax.nn.sigmoid(cg), g2_ref[...])

    kk = k * kk_ref[...]
    sq = kk * kk
    sq_hi = sq.astype(BF)
    sq_lo = (sq - sq_hi.astype(F32)).astype(BF)
    ss = (jnp.dot(sq_hi, bd_ref[...], preferred_element_type=F32)
          + jnp.dot(sq_lo, bd_ref[...], preferred_element_type=F32))
    kk = kk * lax.rsqrt(jnp.maximum(ss, 1e-24))
    k = k * (1.0 + (a - 1.0) * ka_ref[...])
    aa = -kk
    bb = kk * a

    lw_hi = lw.astype(BF)
    rem = lw - lw_hi.astype(F32)
    lw_mid = rem.astype(BF)
    lw_lo = (rem - lw_mid.astype(F32)).astype(BF)
    tri = tri_ref[...]
    cum = (jnp.dot(tri, lw_hi, preferred_element_type=F32)
           + jnp.dot(tri, lw_mid, preferred_element_type=F32)
           + jnp.dot(tri, lw_lo, preferred_element_type=F32))
    last = cum[c - 1:c, :]
    inv = jnp.exp(-cum)
    tail = jnp.exp(last - cum)
    at = aa * jnp.exp(cum - lw)
    bt = bb * inv
    kt = k * inv
    rt = r * jnp.exp(cum)
    bh = bb * tail
    kh = k * tail
    w_end = jnp.exp(last)
    bonus_rk = r * k * rk_ref[...]

    ri = lax.broadcasted_iota(jnp.int32, (c, c), 0)
    ci = lax.broadcasted_iota(jnp.int32, (c, c), 1)
    strict = ri > ci
    incl = ri >= ci
    eye = (ri == ci).astype(F32)

    hs = range(N_HEAD)
    sls = [slice(h * HEAD, (h + 1) * HEAD) for h in hs]
    at_h = [at[:, sl] for sl in sls]
    bt_h = [bt[:, sl] for sl in sls]
    kt_h = [kt[:, sl] for sl in sls]
    rt_h = [rt[:, sl] for sl in sls]
    v_h = [v[:, sl] for sl in sls]
    a_ab = [jnp.where(strict, _mm_nt(at_h[h], bt_h[h]), 0.0) for h in hs]
    a_ak = [jnp.where(strict, _mm_nt(at_h[h], kt_h[h]), 0.0) for h in hs]
    a_rb = [jnp.where(incl, _mm_nt(rt_h[h], bt_h[h]), 0.0) for h in hs]
    a_rk = [jnp.where(incl, _mm_nt(rt_h[h], kt_h[h]), 0.0) for h in hs]
    tinv = [eye + a_ab[h] for h in hs]
    pw = a_ab
    x_h = [_mm(a_ak[h], v_h[h]) for h in hs]
    for _ in range(int(math.log2(c)) - 1):
        pw = [_mm(pw[h], pw[h]) for h in hs]
        tinv = [tinv[h] + _mm(pw[h], tinv[h]) for h in hs]
    ah = [_mm(tinv[h], at_h[h]) for h in hs]
    uh = [_mm(tinv[h], x_h[h]) for h in hs]
    rh = [rt_h[h] + _mm(a_rb[h], ah[h]) for h in hs]
    yh = [_mm(a_rb[h], uh[h]) + _mm(a_rk[h], v_h[h]) for h in hs]
    s0 = [state_ref[h] for h in hs]
    y = [_mm(rh[h], s0[h]) + yh[h] for h in hs]
    u_full = [_mm(ah[h], s0[h]) + uh[h] for h in hs]
    for h in hs:
        w_col = jnp.sum(eye * w_end[:, sls[h]], axis=1, keepdims=True)
        state_ref[h] = (w_col * s0[h] + _mm_tn(bh[:, sls[h]], u_full[h])
                        + _mm_tn(kh[:, sls[h]], v_h[h]))
    for h in hs:
        sl = sls[h]
        mean = jnp.mean(y[h], axis=-1, keepdims=True)
        yc = y[h] - mean
        var = jnp.mean(yc * yc, axis=-1, keepdims=True)
        yn = yc * lax.rsqrt(var + GN_EPS) * lnw_ref[:, sl] + lnb_ref[:, sl]
        bonus = jnp.sum(bonus_rk[:, sl], axis=-1, keepdims=True) * v_h[h]
        y_scr[:, sl] = (yn + bonus) * g[:, sl]
    o_ref[...] = y_scr[...].astype(BF)


def _rwkv(p_rwkv, prm, b, s):
    n = b * s
    nchunk = s // CHUNK
    const2 = lambda bi, t: (0, 0)
    vec = pl.BlockSpec((1, W_B), const2)
    lora = DECAY_LORA + A_LORA
    return pl.pallas_call(
        _rwkv_kernel,
        grid=(b, nchunk),
        in_specs=[
            pl.BlockSpec((CHUNK, RWKV_COLS), lambda bi, t: (bi * nchunk + t, 0)),
            pl.BlockSpec((1, RWKV_COLS), const2),
            vec,
            pl.BlockSpec((lora, W_B), const2),
            vec,
            pl.BlockSpec((lora, W_B), const2),
            pl.BlockSpec((GATE_LORA, W_B), const2),
            vec, vec, vec, vec, vec,
            pl.BlockSpec((W_B, W_B), const2),
            pl.BlockSpec((CHUNK, CHUNK), const2),
        ],
        out_specs=pl.BlockSpec((CHUNK, W_B), lambda bi, t: (bi * nchunk + t, 0)),
        out_shape=jax.ShapeDtypeStruct((n, W_B), BF),
        scratch_shapes=[pltpu.VMEM((1, RWKV_COLS), F32),
                        pltpu.VMEM((N_HEAD, HEAD, HEAD), F32),
                        pltpu.VMEM((CHUNK, W_B), F32)],
        compiler_params=pltpu.CompilerParams(
            dimension_semantics=("arbitrary", "arbitrary"), vmem_limit_bytes=VMEM_LIMIT),
        name="rwkv7",
    )(p_rwkv, prm["mu"], prm["w0"], prm["w2"], prm["a0"], prm["a2"], prm["g2"],
      prm["k_k"], prm["k_a"], prm["r_k"], prm["ln_w"], prm["ln_b"], prm["bd"], prm["tri"])


def _merge_kernel(ya_ref, yb_ref, gate_ref, x_ref, pa_ref, pb_ref, wo_ref, g_ref, o_ref):
    ma = jnp.dot(ya_ref[...], pa_ref[...], preferred_element_type=F32)
    mb = jnp.dot(yb_ref[...], pb_ref[...], preferred_element_type=F32)
    merged = (gate_ref[:, 0:D_MODEL].astype(F32) * ma
              + gate_ref[:, D_MODEL:2 * D_MODEL].astype(F32) * mb)
    z = jnp.dot(merged.astype(BF), wo_ref[...], preferred_element_type=F32)
    o_ref[...] = x_ref[...] + _rms(z, g_ref[...])


def _merge(ya, yb, gates, x2, pa, pb, wo, g):
    n = x2.shape[0]
    const = lambda i: (0, 0)
    row = lambda i: (i, 0)
    return pl.pallas_call(
        _merge_kernel,
        grid=(n // ROW_TILE,),
        in_specs=[
            pl.BlockSpec((ROW_TILE, W_A), row),
            pl.BlockSpec((ROW_TILE, W_B), row),
            pl.BlockSpec((ROW_TILE, GATE_COLS), row),
            pl.BlockSpec((ROW_TILE, D_MODEL), row),
            pl.BlockSpec((W_A, D_MODEL), const),
            pl.BlockSpec((W_B, D_MODEL), const),
            pl.BlockSpec((D_MODEL, D_MODEL), const),
            pl.BlockSpec((1, D_MODEL), const),
        ],
        out_specs=pl.BlockSpec((ROW_TILE, D_MODEL), row),
        out_shape=jax.ShapeDtypeStruct((n, D_MODEL), F32),
        compiler_params=pltpu.CompilerParams(
            dimension_semantics=("arbitrary",), vmem_limit_bytes=VMEM_LIMIT),
        name="merge_out",
    )(ya, yb, gates, x2, pa, pb, wo, g)


FF_TILE = 1024


def _ffn_kernel(x_ref, g1_ref, wu_ref, wd_ref, g2_ref, o_ref):
    x = x_ref[...]
    hf = _rms(x, g1_ref[...]).astype(BF)
    acc = jnp.zeros(x.shape, F32)
    for c in range(0, D_FF, FF_TILE):
        u = jnp.dot(hf, wu_ref[:, c:c + FF_TILE], preferred_element_type=F32)
        u = jnp.maximum(u, 0.0)
        u = (u * u).astype(BF)
        acc = acc + jnp.dot(u, wd_ref[c:c + FF_TILE, :], preferred_element_type=F32)
    o_ref[...] = x + _rms(acc, g2_ref[...])


def _ffn(x2, g1, wu, wd, g2):
    n = x2.shape[0]
    const = lambda i: (0, 0)
    row = lambda i: (i, 0)
    return pl.pallas_call(
        _ffn_kernel,
        grid=(n // ROW_TILE,),
        in_specs=[
            pl.BlockSpec((ROW_TILE, D_MODEL), row),
            pl.BlockSpec((1, D_MODEL), const),
            pl.BlockSpec((D_MODEL, D_FF), const),
            pl.BlockSpec((D_FF, D_MODEL), const),
            pl.BlockSpec((1, D_MODEL), const),
        ],
        out_specs=pl.BlockSpec((ROW_TILE, D_MODEL), row),
        out_shape=jax.ShapeDtypeStruct((n, D_MODEL), F32),
        compiler_params=pltpu.CompilerParams(
            dimension_semantics=("arbitrary",), vmem_limit_bytes=VMEM_LIMIT),
        name="ffn",
    )(x2, g1, wu, wd, g2)


def _layer(x2, b, s, lp):
    row = lambda a: a.reshape(1, -1).astype(F32)
    qkv, p_rwkv, gates = _inproj(x2, row(lp["pre_mix_g"]), lp["w_in"].astype(BF),
                                 row(lp["gate_bias"]))
    ya = _attention(qkv, _attn_bias_base(lp["rel_bias"]), b, s)

    zeros_lora = jnp.zeros((DECAY_LORA, W_B), BF)
    blk = jnp.arange(W_B) // HEAD
    tri = jnp.arange(CHUNK)
    prm = {
        "mu": row(lp["shift_mu"]),
        "w0": row(lp["w0"]),
        "w2": jnp.concatenate([lp["w2"].astype(BF), zeros_lora], axis=0),
        "a0": row(lp["a0"]),
        "a2": jnp.concatenate([zeros_lora, lp["a2"].astype(BF)], axis=0),
        "g2": lp["g2"].astype(BF),
        "k_k": row(lp["k_k"]),
        "k_a": row(lp["k_a"]),
        "r_k": row(lp["r_k"]),
        "ln_w": row(lp["ln_x_w"]),
        "ln_b": row(lp["ln_x_b"]),
        "bd": (blk[:, None] == blk[None, :]).astype(BF),
        "tri": (tri[:, None] >= tri[None, :]).astype(BF),
    }
    yb = _rwkv(p_rwkv, prm, b, s)

    x2 = _merge(ya, yb, gates, x2, lp["proj_a"].astype(BF), lp["proj_b"].astype(BF),
                lp["w_out"].astype(BF), row(lp["post_mix_g"]))
    x2 = _ffn(x2, row(lp["pre_ffn_g"]), lp["w_up"].astype(BF), lp["w_down"].astype(BF),
              row(lp["post_ffn_g"]))
    return x2


@jax.jit
def _forward(x, params):
    b, s, d = x.shape
    assert d == D_MODEL and s % ROW_TILE == 0
    x2 = x.reshape(b * s, d)
    depth = params["w_in"].shape[0]
    for l in range(depth):
        lp = {name: val[l] for name, val in params.items()}
        x2 = _layer(x2, b, s, lp)
    return x2.reshape(b, s, d)


def kernel(x, pre_mix_g, w_in, gate_bias, rel_bias, shift_mu, w0, w2, a0, a2, g2, k_k, k_a, r_k, ln_x_w, ln_x_b, proj_a, proj_b, w_out, post_mix_g, pre_ffn_g, w_up, w_down, post_ffn_g):
    params = dict(pre_mix_g=pre_mix_g, w_in=w_in, gate_bias=gate_bias, rel_bias=rel_bias,
                  shift_mu=shift_mu, w0=w0, w2=w2, a0=a0, a2=a2, g2=g2, k_k=k_k, k_a=k_a,
                  r_k=r_k, ln_x_w=ln_x_w, ln_x_b=ln_x_b, proj_a=proj_a, proj_b=proj_b,
                  w_out=w_out, post_mix_g=post_mix_g, pre_ffn_g=pre_ffn_g, w_up=w_up,
                  w_down=w_down, post_ffn_g=post_ffn_g)
    return _forward(x, params)
```

```python
import functools
import math

import jax
import jax.numpy as jnp
from jax import lax
from jax.experimental import pallas as pl
from jax.experimental.pallas import tpu as pltpu

BF = jnp.bfloat16
F32 = jnp.float32

D_MODEL = 1024
CHUNK = 64
BAND_PREV = 8
REL_CLIP = 256
W_A = D_MODEL // 2
HEAD = 64
N_HEAD = W_A // HEAD
W_B = D_MODEL // 2
DECAY_LORA = 64
A_LORA = 64
GATE_LORA = 128
D_FF = 4 * D_MODEL
ATT_COLS = 3 * W_A
RWKV_COLS = 3 * W_B + DECAY_LORA + A_LORA + GATE_LORA
GATE_COLS = 2 * D_MODEL
IN_COLS = ATT_COLS + RWKV_COLS + GATE_COLS
RMS_EPS = 1e-6
GN_EPS = HEAD * 1e-5
NEG_INF = -1e30

ROW_TILE = 512
ATT_Q_SUB = 128
ATT_K_WIN = ATT_Q_SUB + BAND_PREV * CHUNK
ATT_BASE = ATT_K_WIN + ATT_Q_SUB
SEG_W = 256
VMEM_LIMIT = 56 * 1024 * 1024


def _mm(a, b):
    return jnp.dot(a.astype(BF), b.astype(BF), preferred_element_type=F32)


def _mm_nt(a, b):
    return lax.dot_general(a.astype(BF), b.astype(BF), (((1,), (1,)), ((), ())),
                           preferred_element_type=F32)


def _mm_tn(a, b):
    return lax.dot_general(a.astype(BF), b.astype(BF), (((0,), (0,)), ((), ())),
                           preferred_element_type=F32)


def _rms(x, g):
    ms = jnp.mean(x * x, axis=-1, keepdims=True)
    return x * lax.rsqrt(ms + RMS_EPS) * g


def _inproj_kernel(x_ref, g_ref, w_ref, gb_ref, qkv_ref, pr_ref, gate_ref):
    h = _rms(x_ref[...], g_ref[...]).astype(BF)
    for c in range(0, ATT_COLS, 512):
        acc = jnp.dot(h, w_ref[:, c:c + 512], preferred_element_type=F32)
        if c < W_A:
            acc = acc * (HEAD ** -0.5)
        qkv_ref[:, c:c + 512] = acc.astype(BF)
    for c in range(0, RWKV_COLS, 256):
        o = ATT_COLS + c
        pr_ref[:, c:c + 256] = jnp.dot(h, w_ref[:, o:o + 256], preferred_element_type=F32)
    for c in range(0, GATE_COLS, 512):
        o = ATT_COLS + RWKV_COLS + c
        z = jnp.dot(h, w_ref[:, o:o + 512], preferred_element_type=F32) + gb_ref[:, c:c + 512]
        gate_ref[:, c:c + 512] = jax.nn.sigmoid(z).astype(BF)


def _inproj(x2, g, w_bf, gate_bias):
    n = x2.shape[0]
    const = lambda i: (0, 0)
    row = lambda i: (i, 0)
    return pl.pallas_call(
        _inproj_kernel,
        grid=(n // ROW_TILE,),
        in_specs=[
            pl.BlockSpec((ROW_TILE, D_MODEL), row),
            pl.BlockSpec((1, D_MODEL), const),
            pl.BlockSpec((D_MODEL, IN_COLS), const),
            pl.BlockSpec((1, GATE_COLS), const),
        ],
        out_specs=[
            pl.BlockSpec((ROW_TILE, ATT_COLS), row),
            pl.BlockSpec((ROW_TILE, RWKV_COLS), row),
            pl.BlockSpec((ROW_TILE, GATE_COLS), row),
        ],
        out_shape=[
            jax.ShapeDtypeStruct((n, ATT_COLS), BF),
            jax.ShapeDtypeStruct((n, RWKV_COLS), F32),
            jax.ShapeDtypeStruct((n, GATE_COLS), BF),
        ],
        compiler_params=pltpu.CompilerParams(
            dimension_semantics=("arbitrary",), vmem_limit_bytes=VMEM_LIMIT),
        name="inproj",
    )(x2, g, w_bf, gate_bias)


def _attn_kernel(q_ref, kp_ref, kc_ref, vp_ref, vc_ref, base_ref, o_ref, kwin, vwin, bias_scr):
    i = pl.program_id(1)

    @pl.when(jnp.logical_and(pl.program_id(0) == 0, i == 0))
    def _():
        qi = lax.broadcasted_iota(jnp.int32, (ATT_Q_SUB, ATT_K_WIN), 0)
        kj = lax.broadcasted_iota(jnp.int32, (ATT_Q_SUB, ATT_K_WIN), 1)
        dchunk = (BAND_PREV + qi // CHUNK) - kj // CHUNK
        band = jnp.logical_and(dchunk >= 0, dchunk <= BAND_PREV)
        for h in range(N_HEAD):
            rows = jnp.broadcast_to(base_ref[h:h + 1, :], (ATT_Q_SUB, ATT_BASE))
            toep = pltpu.roll(rows, 0, 1, stride=1, stride_axis=0)
            bias_scr[h] = jnp.where(band, toep[:, 0:ATT_K_WIN], NEG_INF)

    kwin[0:ROW_TILE, :] = kp_ref[...]
    kwin[ROW_TILE:2 * ROW_TILE, :] = kc_ref[...]
    vwin[0:ROW_TILE, :] = vp_ref[...]
    vwin[ROW_TILE:2 * ROW_TILE, :] = vc_ref[...]
    lane = lax.broadcasted_iota(jnp.int32, (ATT_Q_SUB, 2 * HEAD), 1)
    col = lax.broadcasted_iota(jnp.int32, (ATT_Q_SUB, ATT_K_WIN), 1)
    for j in range(ROW_TILE // ATT_Q_SUB):
        r0 = j * ATT_Q_SUB
        key_ok = jnp.logical_or(i > 0, col >= ROW_TILE - r0)
        for p in range(N_HEAD // 2):
            c0 = p * 2 * HEAD
            q2 = q_ref[r0:r0 + ATT_Q_SUB, c0:c0 + 2 * HEAD]
            k2 = kwin[r0:r0 + ATT_K_WIN, c0:c0 + 2 * HEAD]
            v2 = vwin[r0:r0 + ATT_K_WIN, c0:c0 + 2 * HEAD]
            outs = []
            for e in range(2):
                in_head = (lane >= HEAD) if e else (lane < HEAD)
                qm = jnp.where(in_head, q2, jnp.zeros_like(q2))
                s = lax.dot_general(qm, k2, (((1,), (1,)), ((), ())),
                                    preferred_element_type=F32)
                s = s + bias_scr[2 * p + e]
                s = jnp.where(key_ok, s, NEG_INF)
                m = jnp.max(s, axis=-1, keepdims=True)
                ex = jnp.exp(s - m)
                l = jnp.sum(ex, axis=-1, keepdims=True)
                o = jnp.dot(ex.astype(BF), v2, preferred_element_type=F32)
                outs.append(o / l)
            o = jnp.where(lane < HEAD, outs[0], outs[1])
            o_ref[r0:r0 + ATT_Q_SUB, c0:c0 + 2 * HEAD] = o.astype(BF)


def _attn_bias_base(rel_bias):
    pos = jnp.arange(ATT_BASE)
    d = jnp.where(pos < ATT_K_WIN, pos, pos - ATT_BASE)
    idx = jnp.clip(BAND_PREV * CHUNK - d, -(CHUNK - 1), REL_CLIP) + (CHUNK - 1)
    return rel_bias.astype(F32)[:, idx]


def _attention(qkv, bias_base, b, s):
    n = b * s
    nblk = s // ROW_TILE
    blk = (ROW_TILE, W_A)
    return pl.pallas_call(
        _attn_kernel,
        grid=(b, nblk),
        in_specs=[
            pl.BlockSpec(blk, lambda bi, i: (bi * nblk + i, 0)),
            pl.BlockSpec(blk, lambda bi, i: (bi * nblk + jnp.maximum(i - 1, 0), 1)),
            pl.BlockSpec(blk, lambda bi, i: (bi * nblk + i, 1)),
            pl.BlockSpec(blk, lambda bi, i: (bi * nblk + jnp.maximum(i - 1, 0), 2)),
            pl.BlockSpec(blk, lambda bi, i: (bi * nblk + i, 2)),
            pl.BlockSpec((N_HEAD, ATT_BASE), lambda bi, i: (0, 0)),
        ],
        out_specs=pl.BlockSpec(blk, lambda bi, i: (bi * nblk + i, 0)),
        out_shape=jax.ShapeDtypeStruct((n, W_A), BF),
        scratch_shapes=[pltpu.VMEM((2 * ROW_TILE, W_A), BF),
                        pltpu.VMEM((2 * ROW_TILE, W_A), BF),
                        pltpu.VMEM((N_HEAD, ATT_Q_SUB, ATT_K_WIN), F32)],
        compiler_params=pltpu.CompilerParams(
            dimension_semantics=("arbitrary", "arbitrary"), vmem_limit_bytes=VMEM_LIMIT),
        name="band_attn",
    )(qkv, qkv, qkv, qkv, qkv, bias_base)


def _split2(x):
    hi = x.astype(BF)
    lo = (x - hi.astype(F32)).astype(BF)
    return hi, lo


def _seg_sum(x, bd):
    w = bd.shape[0]
    xb = x.astype(BF)
    return jnp.concatenate(
        [jnp.dot(xb[:, j:j + w], bd, preferred_element_type=F32)
         for j in range(0, x.shape[1], w)], axis=1)


def _rwkv_kernel(p_ref, mu_ref, w0_ref, w2_ref, a0_ref, a2_ref, g2_ref, kk_ref, ka_ref,
                 rk_ref, lnw_ref, lnb_ref, bd_ref, tri_ref, o_ref,
                 carry_ref, state_ref, y_scr):
    t = pl.program_id(0)
    nb = p_ref.shape[0]
    c = CHUNK
    slab = 2 * HEAD

    @pl.when(t == 0)
    def _():
        carry_ref[...] = jnp.zeros_like(carry_ref)
        state_ref[...] = jnp.zeros_like(state_ref)

    p = jnp.concatenate([p_ref[bi] for bi in range(nb)], axis=0)
    row = lax.broadcasted_iota(jnp.int32, p.shape, 0)
    prev = pltpu.roll(p, 1, 0)
    for bi in range(nb):
        prev = jnp.where(row == bi * c, jnp.broadcast_to(carry_ref[bi], p.shape), prev)
        carry_ref[bi] = p[(bi + 1) * c - 1:(bi + 1) * c, :]
    ps = p + (prev - p) * mu_ref[...]

    r = ps[:, 0:W_B]
    k = ps[:, W_B:2 * W_B]
    v = ps[:, 2 * W_B:3 * W_B]
    lora_in = ps[:, 3 * W_B:3 * W_B + DECAY_LORA + A_LORA]
    cg = ps[:, 3 * W_B + DECAY_LORA + A_LORA:]

    u = w0_ref[...] + _mm(jnp.tanh(lora_in), w2_ref[...])
    lw = -math.exp(-0.5) * jax.nn.sigmoid(u)
    a = jax.nn.sigmoid(a0_ref[...] + _mm(lora_in, a2_ref[...]))
    g = _mm(jax.nn.sigmoid(cg), g2_ref[...])

    bd = bd_ref[...]
    kk = k * kk_ref[...]
    kk = kk * lax.rsqrt(jnp.maximum(_seg_sum(kk * kk, bd), 1e-24))
    k = k * (1.0 + (a - 1.0) * ka_ref[...])
    aa = -kk
    bb = kk * a

    tri = tri_ref[...]
    cum = sum(jnp.dot(tri, part, preferred_element_type=F32) for part in _split2(lw))
    last = jnp.concatenate(
        [jnp.broadcast_to(cum[(bi + 1) * c - 1:(bi + 1) * c, :], (c, W_B)) for bi in range(nb)],
        axis=0)
    inv = jnp.exp(-cum)
    tail = jnp.exp(last - cum)
    at = aa * jnp.exp(cum - lw)
    bt = bb * inv
    kt = k * inv
    rt = r * jnp.exp(cum)
    bh = bb * tail
    kh = k * tail
    w_end = jnp.exp(last)
    bonus = _seg_sum(r * k * rk_ref[...], bd) * v

    lane = lax.broadcasted_iota(jnp.int32, (c, slab), 1)
    lane2 = lax.broadcasted_iota(jnp.int32, (2 * c, slab), 1)
    row2 = lax.broadcasted_iota(jnp.int32, (2 * c, slab), 0)
    half = [lane < HEAD, lane >= HEAD]
    half2 = [lane2 < HEAD, lane2 >= HEAD]
    rl = row2 & (c - 1)
    cl = lane2 & (c - 1)
    mask4 = jnp.logical_or(rl > cl, jnp.logical_and(row2 >= c, rl == cl))
    ri = lax.broadcasted_iota(jnp.int32, (c, c), 0)
    ci = lax.broadcasted_iota(jnp.int32, (c, c), 1)
    eye = (ri == ci).astype(F32)
    zeros_top = jnp.zeros((c, slab), BF)

    chains = [(bi, pp, e) for bi in range(nb) for pp in range(N_HEAD // 2) for e in range(2)]
    n = range(len(chains))

    def blk(x, ch):
        bi, pp, _ = ch
        return x[bi * c:(bi + 1) * c, pp * slab:(pp + 1) * slab]

    ar = [jnp.concatenate([blk(at, ch), blk(rt, ch)], axis=0) for ch in chains]
    bk = [jnp.concatenate([blk(bt, ch), blk(kt, ch)], axis=0).astype(BF) for ch in chains]
    bkh = [jnp.concatenate([blk(bh, ch), blk(kh, ch)], axis=0).astype(BF) for ch in chains]
    vsw = [pltpu.roll(blk(v, ch), HEAD, 1) for ch in chains]
    vx = [jnp.where(half[1 - chains[i][2]], vsw[i], 0.0).astype(BF) for i in n]
    a4 = [jnp.where(mask4,
                    _mm_nt(jnp.where(half2[chains[i][2]], ar[i], 0.0), bk[i]), 0.0).astype(BF)
          for i in n]
    a_ab = [a4[i][0:c, 0:c].astype(F32) for i in n]
    x_o = [jnp.dot(a4[i][0:c, :], jnp.concatenate([zeros_top, vx[i]], axis=0),
                   preferred_element_type=F32) for i in n]
    tinv = [eye + a_ab[i] for i in n]
    pw = a_ab
    for _ in range(int(math.log2(c)) - 1):
        pw = [_mm(pw[i], pw[i]) for i in n]
        tinv = [tinv[i] + _mm(pw[i], tinv[i]) for i in n]
    z0 = [jnp.where(half[chains[i][2]], blk(at, chains[i]), x_o[i]) for i in n]
    ta = [_mm(tinv[i], z0[i]) for i in n]
    ry = [jnp.dot(a4[i][c:2 * c, :], jnp.concatenate([ta[i].astype(BF), vx[i]], axis=0),
                  preferred_element_type=F32) for i in n]
    gm = [jnp.concatenate(
        [ry[i] + jnp.where(half[chains[i][2]], blk(rt, chains[i]), 0.0), ta[i]], axis=0)
        for i in n]
    s0 = [state_ref[ch[0], 2 * ch[1] + ch[2]] for ch in chains]
    yu = [_mm_nt(gm[i], s0[i]) + gm[i] for i in n]
    for i in n:
        bi, pp, e = chains[i]
        uv = jnp.concatenate([yu[i][c:2 * c, :].astype(BF), vx[i]], axis=0)
        upd = lax.dot_general(uv, bkh[i], (((0,), (0,)), ((), ())), preferred_element_type=F32)
        keep = jnp.logical_and(half2[e], (row2 >= HEAD) if e == 0 else (row2 < HEAD))
        state_ref[bi, 2 * pp + e] = jnp.where(keep, s0[i] * blk(w_end, chains[i])[0:1, :] + upd, 0.0)
    for bi in range(nb):
        for pp in range(N_HEAD // 2):
            i0 = chains.index((bi, pp, 0))
            i1 = chains.index((bi, pp, 1))
            both = jnp.where(half[1], yu[i0][0:c, :], yu[i1][0:c, :])
            y_scr[bi * c:(bi + 1) * c, pp * slab:(pp + 1) * slab] = pltpu.roll(both, HEAD, 1)

    y = y_scr[...]
    yc = y - _seg_sum(y, bd) * (1.0 / HEAD)
    var = _seg_sum(yc * yc, bd) * (1.0 / HEAD)
    yn = yc * lax.rsqrt(var + GN_EPS) * lnw_ref[...] + lnb_ref[...]
    out = ((yn + bonus) * g).astype(BF)
    for bi in range(nb):
        o_ref[bi] = out[bi * c:(bi + 1) * c, :]


def _rwkv(p_rwkv, prm, b, s):
    nchunk = s // CHUNK
    const2 = lambda t: (0, 0)
    vec = pl.BlockSpec((1, W_B), const2)
    lora = DECAY_LORA + A_LORA
    rows = jnp.arange(b * CHUNK)
    tri = jnp.logical_and(rows[:, None] >= rows[None, :],
                          rows[:, None] // CHUNK == rows[None, :] // CHUNK).astype(BF)
    out = pl.pallas_call(
        _rwkv_kernel,
        grid=(nchunk,),
        in_specs=[
            pl.BlockSpec((b, CHUNK, RWKV_COLS), lambda t: (0, t, 0)),
            pl.BlockSpec((1, RWKV_COLS), const2),
            vec,
            pl.BlockSpec((lora, W_B), const2),
            vec,
            pl.BlockSpec((lora, W_B), const2),
            pl.BlockSpec((GATE_LORA, W_B), const2),
            vec, vec, vec, vec, vec,
            pl.BlockSpec((SEG_W, SEG_W), const2),
            pl.BlockSpec((b * CHUNK, b * CHUNK), const2),
        ],
        out_specs=pl.BlockSpec((b, CHUNK, W_B), lambda t: (0, t, 0)),
        out_shape=jax.ShapeDtypeStruct((b, s, W_B), BF),
        scratch_shapes=[pltpu.VMEM((b, 1, RWKV_COLS), F32),
                        pltpu.VMEM((b, N_HEAD, 2 * HEAD, 2 * HEAD), F32),
                        pltpu.VMEM((b * CHUNK, W_B), F32)],
        compiler_params=pltpu.CompilerParams(
            dimension_semantics=("arbitrary",), vmem_limit_bytes=VMEM_LIMIT),
        name="rwkv7",
    )(p_rwkv.reshape(b, s, RWKV_COLS), prm["mu"], prm["w0"], prm["w2"], prm["a0"], prm["a2"],
      prm["g2"], prm["k_k"], prm["k_a"], prm["r_k"], prm["ln_w"], prm["ln_b"], prm["bd"], tri)
    return out.reshape(b * s, W_B)


def _merge_kernel(ya_ref, yb_ref, gate_ref, x_ref, pa_ref, pb_ref, wo_ref, g_ref, o_ref):
    ma = jnp.dot(ya_ref[...], pa_ref[...], preferred_element_type=F32)
    mb = jnp.dot(yb_ref[...], pb_ref[...], preferred_element_type=F32)
    merged = (gate_ref[:, 0:D_MODEL].astype(F32) * ma
              + gate_ref[:, D_MODEL:2 * D_MODEL].astype(F32) * mb)
    z = jnp.dot(merged.astype(BF), wo_ref[...], preferred_element_type=F32)
    o_ref[...] = x_ref[...] + _rms(z, g_ref[...])


def _merge(ya, yb, gates, x2, pa, pb, wo, g):
    n = x2.shape[0]
    const = lambda i: (0, 0)
    row = lambda i: (i, 0)
    return pl.pallas_call(
        _merge_kernel,
        grid=(n // ROW_TILE,),
        in_specs=[
            pl.BlockSpec((ROW_TILE, W_A), row),
            pl.BlockSpec((ROW_TILE, W_B), row),
            pl.BlockSpec((ROW_TILE, GATE_COLS), row),
            pl.BlockSpec((ROW_TILE, D_MODEL), row),
            pl.BlockSpec((W_A, D_MODEL), const),
            pl.BlockSpec((W_B, D_MODEL), const),
            pl.BlockSpec((D_MODEL, D_MODEL), const),
            pl.BlockSpec((1, D_MODEL), const),
        ],
        out_specs=pl.BlockSpec((ROW_TILE, D_MODEL), row),
        out_shape=jax.ShapeDtypeStruct((n, D_MODEL), F32),
        compiler_params=pltpu.CompilerParams(
            dimension_semantics=("arbitrary",), vmem_limit_bytes=VMEM_LIMIT),
        name="merge_out",
    )(ya, yb, gates, x2, pa, pb, wo, g)


FF_TILE = 1024


def _ffn_kernel(x_ref, g1_ref, wu_ref, wd_ref, g2_ref, o_ref):
    x = x_ref[...]
    hf = _rms(x, g1_ref[...]).astype(BF)
    acc = jnp.zeros(x.shape, F32)
    for c in range(0, D_FF, FF_TILE):
        u = jnp.dot(hf, wu_ref[:, c:c + FF_TILE], preferred_element_type=F32)
        u = jnp.maximum(u, 0.0)
        u = (u * u).astype(BF)
        acc = acc + jnp.dot(u, wd_ref[c:c + FF_TILE, :], preferred_element_type=F32)
    o_ref[...] = x + _rms(acc, g2_ref[...])


def _ffn(x2, g1, wu, wd, g2):
    n = x2.shape[0]
    const = lambda i: (0, 0)
    row = lambda i: (i, 0)
    return pl.pallas_call(
        _ffn_kernel,
        grid=(n // ROW_TILE,),
        in_specs=[
            pl.BlockSpec((ROW_TILE, D_MODEL), row),
            pl.BlockSpec((1, D_MODEL), const),
            pl.BlockSpec((D_MODEL, D_FF), const),
            pl.BlockSpec((D_FF, D_MODEL), const),
            pl.BlockSpec((1, D_MODEL), const),
        ],
        out_specs=pl.BlockSpec((ROW_TILE, D_MODEL), row),
        out_shape=jax.ShapeDtypeStruct((n, D_MODEL), F32),
        compiler_params=pltpu.CompilerParams(
            dimension_semantics=("arbitrary",), vmem_limit_bytes=VMEM_LIMIT),
        name="ffn",
    )(x2, g1, wu, wd, g2)


def _layer(x2, b, s, lp):
    row = lambda a: a.reshape(1, -1).astype(F32)
    qkv, p_rwkv, gates = _inproj(x2, row(lp["pre_mix_g"]), lp["w_in"].astype(BF),
                                 row(lp["gate_bias"]))
    ya = _attention(qkv, _attn_bias_base(lp["rel_bias"]), b, s)

    zeros_lora = jnp.zeros((DECAY_LORA, W_B), BF)
    blk = jnp.arange(SEG_W) // HEAD
    prm = {
        "mu": row(lp["shift_mu"]),
        "w0": row(lp["w0"]),
        "w2": jnp.concatenate([lp["w2"].astype(BF), zeros_lora], axis=0),
        "a0": row(lp["a0"]),
        "a2": jnp.concatenate([zeros_lora, lp["a2"].astype(BF)], axis=0),
        "g2": lp["g2"].astype(BF),
        "k_k": row(lp["k_k"]),
        "k_a": row(lp["k_a"]),
        "r_k": row(lp["r_k"]),
        "ln_w": row(lp["ln_x_w"]),
        "ln_b": row(lp["ln_x_b"]),
        "bd": (blk[:, None] == blk[None, :]).astype(BF),
    }
    yb = _rwkv(p_rwkv, prm, b, s)

    x2 = _merge(ya, yb, gates, x2, lp["proj_a"].astype(BF), lp["proj_b"].astype(BF),
                lp["w_out"].astype(BF), row(lp["post_mix_g"]))
    x2 = _ffn(x2, row(lp["pre_ffn_g"]), lp["w_up"].astype(BF), lp["w_down"].astype(BF),
              row(lp["post_ffn_g"]))
    return x2


@jax.jit
def _forward(x, params):
    b, s, d = x.shape
    assert d == D_MODEL and s % ROW_TILE == 0
    x2 = x.reshape(b * s, d)
    depth = params["w_in"].shape[0]
    for l in range(depth):
        lp = {name: val[l] for name, val in params.items()}
        x2 = _layer(x2, b, s, lp)
    return x2.reshape(b, s, d)


def kernel(x, pre_mix_g, w_in, gate_bias, rel_bias, shift_mu, w0, w2, a0, a2, g2, k_k, k_a, r_k, ln_x_w, ln_x_b, proj_a, proj_b, w_out, post_mix_g, pre_ffn_g, w_up, w_down, post_ffn_g):
    params = dict(pre_mix_g=pre_mix_g, w_in=w_in, gate_bias=gate_bias, rel_bias=rel_bias,
                  shift_mu=shift_mu, w0=w0, w2=w2, a0=a0, a2=a2, g2=g2, k_k=k_k, k_a=k_a,
                  r_k=r_k, ln_x_w=ln_x_w, ln_x_b=ln_x_b, proj_a=proj_a, proj_b=proj_b,
                  w_out=w_out, post_mix_g=post_mix_g, pre_ffn_g=pre_ffn_g, w_up=w_up,
                  w_down=w_down, post_ffn_g=post_ffn_g)
    return _forward(x, params)
```

```python
import functools
import math

import jax
import jax.numpy as jnp
from jax import lax
from jax.experimental import pallas as pl
from jax.experimental.pallas import tpu as pltpu

BF = jnp.bfloat16
F32 = jnp.float32

D_MODEL = 1024
CHUNK = 64
BAND_PREV = 8
REL_CLIP = 256
W_A = D_MODEL // 2
HEAD = 64
N_HEAD = W_A // HEAD
W_B = D_MODEL // 2
DECAY_LORA = 64
A_LORA = 64
GATE_LORA = 128
D_FF = 4 * D_MODEL
ATT_COLS = 3 * W_A
RWKV_COLS = 3 * W_B + DECAY_LORA + A_LORA + GATE_LORA
GATE_COLS = 2 * D_MODEL
IN_COLS = ATT_COLS + RWKV_COLS + GATE_COLS
RMS_EPS = 1e-6
GN_EPS = HEAD * 1e-5
NEG_INF = -1e30

ROW_TILE = 512
ATT_Q_SUB = 128
ATT_K_WIN = ATT_Q_SUB + BAND_PREV * CHUNK
ATT_BASE = ATT_K_WIN + ATT_Q_SUB
SEG_W = 256
VMEM_LIMIT = 56 * 1024 * 1024


def _mm(a, b):
    return jnp.dot(a.astype(BF), b.astype(BF), preferred_element_type=F32)


def _mm_nt(a, b):
    return lax.dot_general(a.astype(BF), b.astype(BF), (((1,), (1,)), ((), ())),
                           preferred_element_type=F32)


def _mm_tn(a, b):
    return lax.dot_general(a.astype(BF), b.astype(BF), (((0,), (0,)), ((), ())),
                           preferred_element_type=F32)


def _rms(x, g):
    ms = jnp.mean(x * x, axis=-1, keepdims=True)
    return x * lax.rsqrt(ms + RMS_EPS) * g


def _inproj_kernel(x_ref, g_ref, w_ref, gb_ref, qkv_ref, pr_ref, gate_ref):
    h = _rms(x_ref[...], g_ref[...]).astype(BF)
    for c in range(0, ATT_COLS, 512):
        acc = jnp.dot(h, w_ref[:, c:c + 512], preferred_element_type=F32)
        if c < W_A:
            acc = acc * (HEAD ** -0.5)
        qkv_ref[:, c:c + 512] = acc.astype(BF)
    for c in range(0, RWKV_COLS, 256):
        o = ATT_COLS + c
        pr_ref[:, c:c + 256] = jnp.dot(h, w_ref[:, o:o + 256], preferred_element_type=F32)
    for c in range(0, GATE_COLS, 512):
        o = ATT_COLS + RWKV_COLS + c
        z = jnp.dot(h, w_ref[:, o:o + 512], preferred_element_type=F32) + gb_ref[:, c:c + 512]
        gate_ref[:, c:c + 512] = jax.nn.sigmoid(z).astype(BF)


def _inproj(x2, g, w_bf, gate_bias):
    n = x2.shape[0]
    const = lambda i: (0, 0)
    row = lambda i: (i, 0)
    return pl.pallas_call(
        _inproj_kernel,
        grid=(n // ROW_TILE,),
        in_specs=[
            pl.BlockSpec((ROW_TILE, D_MODEL), row),
            pl.BlockSpec((1, D_MODEL), const),
            pl.BlockSpec((D_MODEL, IN_COLS), const),
            pl.BlockSpec((1, GATE_COLS), const),
        ],
        out_specs=[
            pl.BlockSpec((ROW_TILE, ATT_COLS), row),
            pl.BlockSpec((ROW_TILE, RWKV_COLS), row),
            pl.BlockSpec((ROW_TILE, GATE_COLS), row),
        ],
        out_shape=[
            jax.ShapeDtypeStruct((n, ATT_COLS), BF),
            jax.ShapeDtypeStruct((n, RWKV_COLS), F32),
            jax.ShapeDtypeStruct((n, GATE_COLS), BF),
        ],
        compiler_params=pltpu.CompilerParams(
            dimension_semantics=("arbitrary",), vmem_limit_bytes=VMEM_LIMIT),
        name="inproj",
    )(x2, g, w_bf, gate_bias)


def _attn_kernel(q_ref, kp_ref, kc_ref, vp_ref, vc_ref, base_ref, o_ref, kwin, vwin, bias_scr):
    i = pl.program_id(1)

    @pl.when(jnp.logical_and(pl.program_id(0) == 0, i == 0))
    def _():
        qi = lax.broadcasted_iota(jnp.int32, (ATT_Q_SUB, ATT_K_WIN), 0)
        kj = lax.broadcasted_iota(jnp.int32, (ATT_Q_SUB, ATT_K_WIN), 1)
        dchunk = (BAND_PREV + qi // CHUNK) - kj // CHUNK
        band = jnp.logical_and(dchunk >= 0, dchunk <= BAND_PREV)
        for h in range(N_HEAD):
            rows = jnp.broadcast_to(base_ref[h:h + 1, :], (ATT_Q_SUB, ATT_BASE))
            toep = pltpu.roll(rows, 0, 1, stride=1, stride_axis=0)
            bias_scr[h] = jnp.where(band, toep[:, 0:ATT_K_WIN], NEG_INF)

    kwin[0:ROW_TILE, :] = kp_ref[...]
    kwin[ROW_TILE:2 * ROW_TILE, :] = kc_ref[...]
    vwin[0:ROW_TILE, :] = vp_ref[...]
    vwin[ROW_TILE:2 * ROW_TILE, :] = vc_ref[...]
    lane = lax.broadcasted_iota(jnp.int32, (ATT_Q_SUB, 2 * HEAD), 1)
    col = lax.broadcasted_iota(jnp.int32, (ATT_Q_SUB, ATT_K_WIN), 1)
    for j in range(ROW_TILE // ATT_Q_SUB):
        r0 = j * ATT_Q_SUB
        key_ok = jnp.logical_or(i > 0, col >= ROW_TILE - r0)
        hs = range(N_HEAD)
        cs = [(h // 2) * 2 * HEAD for h in hs]
        ss = []
        for h in hs:
            q2 = q_ref[r0:r0 + ATT_Q_SUB, cs[h]:cs[h] + 2 * HEAD]
            in_head = (lane >= HEAD) if h % 2 else (lane < HEAD)
            qm = jnp.where(in_head, q2, jnp.zeros_like(q2))
            ss.append(lax.dot_general(qm, kwin[r0:r0 + ATT_K_WIN, cs[h]:cs[h] + 2 * HEAD],
                                      (((1,), (1,)), ((), ())), preferred_element_type=F32))
        exs, ls = [], []
        for h in hs:
            s = jnp.where(key_ok, ss[h] + bias_scr[h], NEG_INF)
            m = jnp.max(s, axis=-1, keepdims=True)
            ex = jnp.exp(s - m)
            ls.append(jnp.sum(ex, axis=-1, keepdims=True))
            exs.append(ex.astype(BF))
        outs = [jnp.dot(exs[h], vwin[r0:r0 + ATT_K_WIN, cs[h]:cs[h] + 2 * HEAD],
                        preferred_element_type=F32) / ls[h] for h in hs]
        for p in range(N_HEAD // 2):
            o = jnp.where(lane < HEAD, outs[2 * p], outs[2 * p + 1])
            o_ref[r0:r0 + ATT_Q_SUB, cs[2 * p]:cs[2 * p] + 2 * HEAD] = o.astype(BF)


def _attn_bias_base(rel_bias):
    pos = jnp.arange(ATT_BASE)
    d = jnp.where(pos < ATT_K_WIN, pos, pos - ATT_BASE)
    idx = jnp.clip(BAND_PREV * CHUNK - d, -(CHUNK - 1), REL_CLIP) + (CHUNK - 1)
    return rel_bias.astype(F32)[:, idx]


def _attention(qkv, bias_base, b, s):
    n = b * s
    nblk = s // ROW_TILE
    blk = (ROW_TILE, W_A)
    return pl.pallas_call(
        _attn_kernel,
        grid=(b, nblk),
        in_specs=[
            pl.BlockSpec(blk, lambda bi, i: (bi * nblk + i, 0)),
            pl.BlockSpec(blk, lambda bi, i: (bi * nblk + jnp.maximum(i - 1, 0), 1)),
            pl.BlockSpec(blk, lambda bi, i: (bi * nblk + i, 1)),
            pl.BlockSpec(blk, lambda bi, i: (bi * nblk + jnp.maximum(i - 1, 0), 2)),
            pl.BlockSpec(blk, lambda bi, i: (bi * nblk + i, 2)),
            pl.BlockSpec((N_HEAD, ATT_BASE), lambda bi, i: (0, 0)),
        ],
        out_specs=pl.BlockSpec(blk, lambda bi, i: (bi * nblk + i, 0)),
        out_shape=jax.ShapeDtypeStruct((n, W_A), BF),
        scratch_shapes=[pltpu.VMEM((2 * ROW_TILE, W_A), BF),
                        pltpu.VMEM((2 * ROW_TILE, W_A), BF),
                        pltpu.VMEM((N_HEAD, ATT_Q_SUB, ATT_K_WIN), F32)],
        compiler_params=pltpu.CompilerParams(
            dimension_semantics=("arbitrary", "arbitrary"), vmem_limit_bytes=VMEM_LIMIT),
        name="band_attn",
    )(qkv, qkv, qkv, qkv, qkv, bias_base)


def _split2(x):
    hi = x.astype(BF)
    lo = (x - hi.astype(F32)).astype(BF)
    return hi, lo


def _seg_sum(x, bd):
    w = bd.shape[0]
    xb = x.astype(BF)
    return jnp.concatenate(
        [jnp.dot(xb[:, j:j + w], bd, preferred_element_type=F32)
         for j in range(0, x.shape[1], w)], axis=1)


def _rwkv_kernel(p_ref, mu_ref, w0_ref, w2_ref, a0_ref, a2_ref, g2_ref, kk_ref, ka_ref,
                 rk_ref, lnw_ref, lnb_ref, bd_ref, tri_ref, o_ref,
                 carry_ref, state_ref, y_scr):
    t = pl.program_id(0)
    nb = p_ref.shape[0]
    c = CHUNK
    slab = 2 * HEAD

    @pl.when(t == 0)
    def _():
        carry_ref[...] = jnp.zeros_like(carry_ref)
        state_ref[...] = jnp.zeros_like(state_ref)

    p = jnp.concatenate([p_ref[bi] for bi in range(nb)], axis=0)
    row = lax.broadcasted_iota(jnp.int32, p.shape, 0)
    prev = pltpu.roll(p, 1, 0)
    for bi in range(nb):
        prev = jnp.where(row == bi * c, jnp.broadcast_to(carry_ref[bi], p.shape), prev)
        carry_ref[bi] = p[(bi + 1) * c - 1:(bi + 1) * c, :]
    ps = p + (prev - p) * mu_ref[...]

    r = ps[:, 0:W_B]
    k = ps[:, W_B:2 * W_B]
    v = ps[:, 2 * W_B:3 * W_B]
    lora_in = ps[:, 3 * W_B:3 * W_B + DECAY_LORA + A_LORA]
    cg = ps[:, 3 * W_B + DECAY_LORA + A_LORA:]

    u = w0_ref[...] + _mm(jnp.tanh(lora_in), w2_ref[...])
    lw = -math.exp(-0.5) * jax.nn.sigmoid(u)
    a = jax.nn.sigmoid(a0_ref[...] + _mm(lora_in, a2_ref[...]))
    g = _mm(jax.nn.sigmoid(cg), g2_ref[...])

    bd = bd_ref[...]
    kk = k * kk_ref[...]
    kk = kk * lax.rsqrt(jnp.maximum(_seg_sum(kk * kk, bd), 1e-24))
    k = k * (1.0 + (a - 1.0) * ka_ref[...])
    aa = -kk
    bb = kk * a

    tri = tri_ref[...]
    cum = sum(jnp.dot(tri, part, preferred_element_type=F32) for part in _split2(lw))
    last = jnp.concatenate(
        [jnp.broadcast_to(cum[(bi + 1) * c - 1:(bi + 1) * c, :], (c, W_B)) for bi in range(nb)],
        axis=0)
    inv = jnp.exp(-cum)
    tail = jnp.exp(last - cum)
    at = aa * jnp.exp(cum - lw)
    bt = bb * inv
    kt = k * inv
    rt = r * jnp.exp(cum)
    bh = bb * tail
    kh = k * tail
    w_end = jnp.exp(last)
    bonus = _seg_sum(r * k * rk_ref[...], bd) * v

    lane = lax.broadcasted_iota(jnp.int32, (c, slab), 1)
    lane2 = lax.broadcasted_iota(jnp.int32, (2 * c, slab), 1)
    row2 = lax.broadcasted_iota(jnp.int32, (2 * c, slab), 0)
    half = [lane < HEAD, lane >= HEAD]
    half2 = [lane2 < HEAD, lane2 >= HEAD]
    rl = row2 & (c - 1)
    cl = lane2 & (c - 1)
    mask4 = jnp.logical_or(rl > cl, jnp.logical_and(row2 >= c, rl == cl))
    ri = lax.broadcasted_iota(jnp.int32, (c, c), 0)
    ci = lax.broadcasted_iota(jnp.int32, (c, c), 1)
    eye = (ri == ci).astype(F32)
    zeros_top = jnp.zeros((c, slab), BF)

    chains = [(bi, pp, e) for bi in range(nb) for pp in range(N_HEAD // 2) for e in range(2)]
    n = range(len(chains))

    def blk(x, ch):
        bi, pp, _ = ch
        return x[bi * c:(bi + 1) * c, pp * slab:(pp + 1) * slab]

    ar = [jnp.concatenate([blk(at, ch), blk(rt, ch)], axis=0) for ch in chains]
    bk = [jnp.concatenate([blk(bt, ch), blk(kt, ch)], axis=0).astype(BF) for ch in chains]
    bkh = [jnp.concatenate([blk(bh, ch), blk(kh, ch)], axis=0).astype(BF) for ch in chains]
    vsw = [pltpu.roll(blk(v, ch), HEAD, 1) for ch in chains]
    vx = [jnp.where(half[1 - chains[i][2]], vsw[i], 0.0).astype(BF) for i in n]
    a4 = [jnp.where(mask4,
                    _mm_nt(jnp.where(half2[chains[i][2]], ar[i], 0.0), bk[i]), 0.0).astype(BF)
          for i in n]
    a_ab = [a4[i][0:c, 0:c].astype(F32) for i in n]
    x_o = [jnp.dot(a4[i][0:c, :], jnp.concatenate([zeros_top, vx[i]], axis=0),
                   preferred_element_type=F32) for i in n]
    tinv = [eye + a_ab[i] for i in n]
    pw = a_ab
    for _ in range(int(math.log2(c)) - 1):
        pw = [_mm(pw[i], pw[i]) for i in n]
        tinv = [tinv[i] + _mm(pw[i], tinv[i]) for i in n]
    z0 = [jnp.where(half[chains[i][2]], blk(at, chains[i]), x_o[i]) for i in n]
    ta = [_mm(tinv[i], z0[i]) for i in n]
    ry = [jnp.dot(a4[i][c:2 * c, :], jnp.concatenate([ta[i].astype(BF), vx[i]], axis=0),
                  preferred_element_type=F32) for i in n]
    gm = [jnp.concatenate(
        [ry[i] + jnp.where(half[chains[i][2]], blk(rt, chains[i]), 0.0), ta[i]], axis=0)
        for i in n]
    s0 = [state_ref[ch[0], 2 * ch[1] + ch[2]] for ch in chains]
    yu = [_mm_nt(gm[i], s0[i]) + gm[i] for i in n]
    for i in n:
        bi, pp, e = chains[i]
        uv = jnp.concatenate([yu[i][c:2 * c, :].astype(BF), vx[i]], axis=0)
        upd = lax.dot_general(uv, bkh[i], (((0,), (0,)), ((), ())), preferred_element_type=F32)
        keep = jnp.logical_and(half2[e], (row2 >= HEAD) if e == 0 else (row2 < HEAD))
        state_ref[bi, 2 * pp + e] = jnp.where(keep, s0[i] * blk(w_end, chains[i])[0:1, :] + upd, 0.0)
    for bi in range(nb):
        for pp in range(N_HEAD // 2):
            i0 = chains.index((bi, pp, 0))
            i1 = chains.index((bi, pp, 1))
            both = jnp.where(half[1], yu[i0][0:c, :], yu[i1][0:c, :])
            y_scr[bi * c:(bi + 1) * c, pp * slab:(pp + 1) * slab] = pltpu.roll(both, HEAD, 1)

    y = y_scr[...]
    yc = y - _seg_sum(y, bd) * (1.0 / HEAD)
    var = _seg_sum(yc * yc, bd) * (1.0 / HEAD)
    yn = yc * lax.rsqrt(var + GN_EPS) * lnw_ref[...] + lnb_ref[...]
    out = ((yn + bonus) * g).astype(BF)
    for bi in range(nb):
        o_ref[bi] = out[bi * c:(bi + 1) * c, :]


def _rwkv(p_rwkv, prm, b, s):
    nchunk = s // CHUNK
    const2 = lambda t: (0, 0)
    vec = pl.BlockSpec((1, W_B), const2)
    lora = DECAY_LORA + A_LORA
    rows = jnp.arange(b * CHUNK)
    tri = jnp.logical_and(rows[:, None] >= rows[None, :],
                          rows[:, None] // CHUNK == rows[None, :] // CHUNK).astype(BF)
    out = pl.pallas_call(
        _rwkv_kernel,
        grid=(nchunk,),
        in_specs=[
            pl.BlockSpec((b, CHUNK, RWKV_COLS), lambda t: (0, t, 0)),
            pl.BlockSpec((1, RWKV_COLS), const2),
            vec,
            pl.BlockSpec((lora, W_B), const2),
            vec,
            pl.BlockSpec((lora, W_B), const2),
            pl.BlockSpec((GATE_LORA, W_B), const2),
            vec, vec, vec, vec, vec,
            pl.BlockSpec((SEG_W, SEG_W), const2),
            pl.BlockSpec((b * CHUNK, b * CHUNK), const2),
        ],
        out_specs=pl.BlockSpec((b, CHUNK, W_B), lambda t: (0, t, 0)),
        out_shape=jax.ShapeDtypeStruct((b, s, W_B), BF),
        scratch_shapes=[pltpu.VMEM((b, 1, RWKV_COLS), F32),
                        pltpu.VMEM((b, N_HEAD, 2 * HEAD, 2 * HEAD), F32),
                        pltpu.VMEM((b * CHUNK, W_B), F32)],
        compiler_params=pltpu.CompilerParams(
            dimension_semantics=("arbitrary",), vmem_limit_bytes=VMEM_LIMIT),
        name="rwkv7",
    )(p_rwkv.reshape(b, s, RWKV_COLS), prm["mu"], prm["w0"], prm["w2"], prm["a0"], prm["a2"],
      prm["g2"], prm["k_k"], prm["k_a"], prm["r_k"], prm["ln_w"], prm["ln_b"], prm["bd"], tri)
    return out.reshape(b * s, W_B)


def _merge_kernel(ya_ref, yb_ref, gate_ref, x_ref, pa_ref, pb_ref, wo_ref, g_ref, o_ref):
    ma = jnp.dot(ya_ref[...], pa_ref[...], preferred_element_type=F32)
    mb = jnp.dot(yb_ref[...], pb_ref[...], preferred_element_type=F32)
    merged = (gate_ref[:, 0:D_MODEL].astype(F32) * ma
              + gate_ref[:, D_MODEL:2 * D_MODEL].astype(F32) * mb)
    z = jnp.dot(merged.astype(BF), wo_ref[...], preferred_element_type=F32)
    o_ref[...] = x_ref[...] + _rms(z, g_ref[...])


def _merge(ya, yb, gates, x2, pa, pb, wo, g):
    n = x2.shape[0]
    const = lambda i: (0, 0)
    row = lambda i: (i, 0)
    return pl.pallas_call(
        _merge_kernel,
        grid=(n // ROW_TILE,),
        in_specs=[
            pl.BlockSpec((ROW_TILE, W_A), row),
            pl.BlockSpec((ROW_TILE, W_B), row),
            pl.BlockSpec((ROW_TILE, GATE_COLS), row),
            pl.BlockSpec((ROW_TILE, D_MODEL), row),
            pl.BlockSpec((W_A, D_MODEL), const),
            pl.BlockSpec((W_B, D_MODEL), const),
            pl.BlockSpec((D_MODEL, D_MODEL), const),
            pl.BlockSpec((1, D_MODEL), const),
        ],
        out_specs=pl.BlockSpec((ROW_TILE, D_MODEL), row),
        out_shape=jax.ShapeDtypeStruct((n, D_MODEL), F32),
        compiler_params=pltpu.CompilerParams(
            dimension_semantics=("arbitrary",), vmem_limit_bytes=VMEM_LIMIT),
        name="merge_out",
    )(ya, yb, gates, x2, pa, pb, wo, g)


FF_TILE = 1024


def _ffn_kernel(x_ref, g1_ref, wu_ref, wd_ref, g2_ref, o_ref):
    x = x_ref[...]
    hf = _rms(x, g1_ref[...]).astype(BF)
    acc = jnp.zeros(x.shape, F32)
    for c in range(0, D_FF, FF_TILE):
        u = jnp.dot(hf, wu_ref[:, c:c + FF_TILE], preferred_element_type=F32)
        u = jnp.maximum(u, 0.0)
        u = (u * u).astype(BF)
        acc = acc + jnp.dot(u, wd_ref[c:c + FF_TILE, :], preferred_element_type=F32)
    o_ref[...] = x + _rms(acc, g2_ref[...])


def _ffn(x2, g1, wu, wd, g2):
    n = x2.shape[0]
    const = lambda i: (0, 0)
    row = lambda i: (i, 0)
    return pl.pallas_call(
        _ffn_kernel,
        grid=(n // ROW_TILE,),
        in_specs=[
            pl.BlockSpec((ROW_TILE, D_MODEL), row),
            pl.BlockSpec((1, D_MODEL), const),
            pl.BlockSpec((D_MODEL, D_FF), const),
            pl.BlockSpec((D_FF, D_MODEL), const),
            pl.BlockSpec((1, D_MODEL), const),
        ],
        out_specs=pl.BlockSpec((ROW_TILE, D_MODEL), row),
        out_shape=jax.ShapeDtypeStruct((n, D_MODEL), F32),
        compiler_params=pltpu.CompilerParams(
            dimension_semantics=("arbitrary",), vmem_limit_bytes=VMEM_LIMIT),
        name="ffn",
    )(x2, g1, wu, wd, g2)


def _layer(x2, b, s, lp):
    row = lambda a: a.reshape(1, -1).astype(F32)
    qkv, p_rwkv, gates = _inproj(x2, row(lp["pre_mix_g"]), lp["w_in"].astype(BF),
                                 row(lp["gate_bias"]))
    ya = _attention(qkv, _attn_bias_base(lp["rel_bias"]), b, s)

    zeros_lora = jnp.zeros((DECAY_LORA, W_B), BF)
    blk = jnp.arange(SEG_W) // HEAD
    prm = {
        "mu": row(lp["shift_mu"]),
        "w0": row(lp["w0"]),
        "w2": jnp.concatenate([lp["w2"].astype(BF), zeros_lora], axis=0),
        "a0": row(lp["a0"]),
        "a2": jnp.concatenate([zeros_lora, lp["a2"].astype(BF)], axis=0),
        "g2": lp["g2"].astype(BF),
        "k_k": row(lp["k_k"]),
        "k_a": row(lp["k_a"]),
        "r_k": row(lp["r_k"]),
        "ln_w": row(lp["ln_x_w"]),
        "ln_b": row(lp["ln_x_b"]),
        "bd": (blk[:, None] == blk[None, :]).astype(BF),
    }
    yb = _rwkv(p_rwkv, prm, b, s)

    x2 = _merge(ya, yb, gates, x2, lp["proj_a"].astype(BF), lp["proj_b"].astype(BF),
                lp["w_out"].astype(BF), row(lp["post_mix_g"]))
    x2 = _ffn(x2, row(lp["pre_ffn_g"]), lp["w_up"].astype(BF), lp["w_down"].astype(BF),
              row(lp["post_ffn_g"]))
    return x2


@jax.jit
def _forward(x, params):
    b, s, d = x.shape
    assert d == D_MODEL and s % ROW_TILE == 0
    x2 = x.reshape(b * s, d)
    depth = params["w_in"].shape[0]
    for l in range(depth):
        lp = {name: val[l] for name, val in params.items()}
        x2 = _layer(x2, b, s, lp)
    return x2.reshape(b, s, d)


def kernel(x, pre_mix_g, w_in, gate_bias, rel_bias, shift_mu, w0, w2, a0, a2, g2, k_k, k_a, r_k, ln_x_w, ln_x_b, proj_a, proj_b, w_out, post_mix_g, pre_ffn_g, w_up, w_down, post_ffn_g):
    params = dict(pre_mix_g=pre_mix_g, w_in=w_in, gate_bias=gate_bias, rel_bias=rel_bias,
                  shift_mu=shift_mu, w0=w0, w2=w2, a0=a0, a2=a2, g2=g2, k_k=k_k, k_a=k_a,
                  r_k=r_k, ln_x_w=ln_x_w, ln_x_b=ln_x_b, proj_a=proj_a, proj_b=proj_b,
                  w_out=w_out, post_mix_g=post_mix_g, pre_ffn_g=pre_ffn_g, w_up=w_up,
                  w_down=w_down, post_ffn_g=post_ffn_g)
    return _forward(x, params)
```

```python
import functools
import math

import jax
import jax.numpy as jnp
from jax import lax
from jax.experimental import pallas as pl
from jax.experimental.pallas import tpu as pltpu

BF = jnp.bfloat16
F32 = jnp.float32

D_MODEL = 1024
CHUNK = 64
BAND_PREV = 8
REL_CLIP = 256
W_A = D_MODEL // 2
HEAD = 64
N_HEAD = W_A // HEAD
W_B = D_MODEL // 2
DECAY_LORA = 64
A_LORA = 64
GATE_LORA = 128
D_FF = 4 * D_MODEL
ATT_COLS = 3 * W_A
RWKV_COLS = 3 * W_B + DECAY_LORA + A_LORA + GATE_LORA
GATE_COLS = 2 * D_MODEL
IN_COLS = ATT_COLS + RWKV_COLS + GATE_COLS
RMS_EPS = 1e-6
GN_EPS = HEAD * 1e-5
NEG_INF = -1e30

ROW_TILE = 512
ATT_Q_SUB = 128
ATT_K_WIN = ATT_Q_SUB + BAND_PREV * CHUNK
ATT_BASE = ATT_K_WIN + ATT_Q_SUB
SEG_W = 256
VMEM_LIMIT = 56 * 1024 * 1024


def _mm(a, b):
    return jnp.dot(a.astype(BF), b.astype(BF), preferred_element_type=F32)


def _mm_nt(a, b):
    return lax.dot_general(a.astype(BF), b.astype(BF), (((1,), (1,)), ((), ())),
                           preferred_element_type=F32)


def _mm_tn(a, b):
    return lax.dot_general(a.astype(BF), b.astype(BF), (((0,), (0,)), ((), ())),
                           preferred_element_type=F32)


def _rms(x, g):
    ms = jnp.mean(x * x, axis=-1, keepdims=True)
    return x * lax.rsqrt(ms + RMS_EPS) * g


def _inproj_kernel(x_ref, g_ref, w_ref, gb_ref, qkv_ref, pr_ref, gate_ref):
    h = _rms(x_ref[...], g_ref[...]).astype(BF)
    for c in range(0, ATT_COLS, 512):
        acc = jnp.dot(h, w_ref[:, c:c + 512], preferred_element_type=F32)
        if c < W_A:
            acc = acc * (HEAD ** -0.5)
        qkv_ref[:, c:c + 512] = acc.astype(BF)
    for c in range(0, RWKV_COLS, 256):
        o = ATT_COLS + c
        pr_ref[:, c:c + 256] = jnp.dot(h, w_ref[:, o:o + 256], preferred_element_type=F32)
    for c in range(0, GATE_COLS, 512):
        o = ATT_COLS + RWKV_COLS + c
        z = jnp.dot(h, w_ref[:, o:o + 512], preferred_element_type=F32) + gb_ref[:, c:c + 512]
        gate_ref[:, c:c + 512] = jax.nn.sigmoid(z).astype(BF)


def _inproj(x2, g, w_bf, gate_bias):
    n = x2.shape[0]
    const = lambda i: (0, 0)
    row = lambda i: (i, 0)
    return pl.pallas_call(
        _inproj_kernel,
        grid=(n // ROW_TILE,),
        in_specs=[
            pl.BlockSpec((ROW_TILE, D_MODEL), row),
            pl.BlockSpec((1, D_MODEL), const),
            pl.BlockSpec((D_MODEL, IN_COLS), const),
            pl.BlockSpec((1, GATE_COLS), const),
        ],
        out_specs=[
            pl.BlockSpec((ROW_TILE, ATT_COLS), row),
            pl.BlockSpec((ROW_TILE, RWKV_COLS), row),
            pl.BlockSpec((ROW_TILE, GATE_COLS), row),
        ],
        out_shape=[
            jax.ShapeDtypeStruct((n, ATT_COLS), BF),
            jax.ShapeDtypeStruct((n, RWKV_COLS), F32),
            jax.ShapeDtypeStruct((n, GATE_COLS), BF),
        ],
        compiler_params=pltpu.CompilerParams(
            dimension_semantics=("arbitrary",), vmem_limit_bytes=VMEM_LIMIT),
        name="inproj",
    )(x2, g, w_bf, gate_bias)


def _attn_kernel(q_ref, kp_ref, kc_ref, vp_ref, vc_ref, base_ref, o_ref, kwin, vwin, bias_scr):
    i = pl.program_id(1)

    @pl.when(jnp.logical_and(pl.program_id(0) == 0, i == 0))
    def _():
        qi = lax.broadcasted_iota(jnp.int32, (ATT_Q_SUB, ATT_K_WIN), 0)
        kj = lax.broadcasted_iota(jnp.int32, (ATT_Q_SUB, ATT_K_WIN), 1)
        dchunk = (BAND_PREV + qi // CHUNK) - kj // CHUNK
        band = jnp.logical_and(dchunk >= 0, dchunk <= BAND_PREV)
        for h in range(N_HEAD):
            rows = jnp.broadcast_to(base_ref[h:h + 1, :], (ATT_Q_SUB, ATT_BASE))
            toep = pltpu.roll(rows, 0, 1, stride=1, stride_axis=0)
            bias_scr[h] = jnp.where(band, toep[:, 0:ATT_K_WIN], NEG_INF)

    kwin[0:ROW_TILE, :] = kp_ref[...]
    kwin[ROW_TILE:2 * ROW_TILE, :] = kc_ref[...]
    vwin[0:ROW_TILE, :] = vp_ref[...]
    vwin[ROW_TILE:2 * ROW_TILE, :] = vc_ref[...]
    lane = lax.broadcasted_iota(jnp.int32, (ATT_Q_SUB, 2 * HEAD), 1)
    col = lax.broadcasted_iota(jnp.int32, (ATT_Q_SUB, ATT_K_WIN), 1)
    for j in range(ROW_TILE // ATT_Q_SUB):
        r0 = j * ATT_Q_SUB
        key_ok = jnp.logical_or(i > 0, col >= ROW_TILE - r0)
        hs = range(N_HEAD)
        cs = [(h // 2) * 2 * HEAD for h in hs]
        ss = []
        for h in hs:
            q2 = q_ref[r0:r0 + ATT_Q_SUB, cs[h]:cs[h] + 2 * HEAD]
            in_head = (lane >= HEAD) if h % 2 else (lane < HEAD)
            qm = jnp.where(in_head, q2, jnp.zeros_like(q2))
            ss.append(lax.dot_general(qm, kwin[r0:r0 + ATT_K_WIN, cs[h]:cs[h] + 2 * HEAD],
                                      (((1,), (1,)), ((), ())), preferred_element_type=F32))
        exs, ls = [], []
        for h in hs:
            s = jnp.where(key_ok, ss[h] + bias_scr[h], NEG_INF)
            m = jnp.max(s, axis=-1, keepdims=True)
            ex = jnp.exp(s - m)
            ls.append(jnp.sum(ex, axis=-1, keepdims=True))
            exs.append(ex.astype(BF))
        outs = [jnp.dot(exs[h], vwin[r0:r0 + ATT_K_WIN, cs[h]:cs[h] + 2 * HEAD],
                        preferred_element_type=F32) / ls[h] for h in hs]
        for p in range(N_HEAD // 2):
            o = jnp.where(lane < HEAD, outs[2 * p], outs[2 * p + 1])
            o_ref[r0:r0 + ATT_Q_SUB, cs[2 * p]:cs[2 * p] + 2 * HEAD] = o.astype(BF)


def _attn_bias_base(rel_bias):
    pos = jnp.arange(ATT_BASE)
    d = jnp.where(pos < ATT_K_WIN, pos, pos - ATT_BASE)
    idx = jnp.clip(BAND_PREV * CHUNK - d, -(CHUNK - 1), REL_CLIP) + (CHUNK - 1)
    return rel_bias.astype(F32)[:, idx]


def _attention(qkv, bias_base, b, s):
    n = b * s
    nblk = s // ROW_TILE
    blk = (ROW_TILE, W_A)
    return pl.pallas_call(
        _attn_kernel,
        grid=(b, nblk),
        in_specs=[
            pl.BlockSpec(blk, lambda bi, i: (bi * nblk + i, 0)),
            pl.BlockSpec(blk, lambda bi, i: (bi * nblk + jnp.maximum(i - 1, 0), 1)),
            pl.BlockSpec(blk, lambda bi, i: (bi * nblk + i, 1)),
            pl.BlockSpec(blk, lambda bi, i: (bi * nblk + jnp.maximum(i - 1, 0), 2)),
            pl.BlockSpec(blk, lambda bi, i: (bi * nblk + i, 2)),
            pl.BlockSpec((N_HEAD, ATT_BASE), lambda bi, i: (0, 0)),
        ],
        out_specs=pl.BlockSpec(blk, lambda bi, i: (bi * nblk + i, 0)),
        out_shape=jax.ShapeDtypeStruct((n, W_A), BF),
        scratch_shapes=[pltpu.VMEM((2 * ROW_TILE, W_A), BF),
                        pltpu.VMEM((2 * ROW_TILE, W_A), BF),
                        pltpu.VMEM((N_HEAD, ATT_Q_SUB, ATT_K_WIN), F32)],
        compiler_params=pltpu.CompilerParams(
            dimension_semantics=("arbitrary", "arbitrary"), vmem_limit_bytes=VMEM_LIMIT),
        name="band_attn",
    )(qkv, qkv, qkv, qkv, qkv, bias_base)


def _split2(x):
    hi = x.astype(BF)
    lo = (x - hi.astype(F32)).astype(BF)
    return hi, lo


def _seg_sum(x, bd):
    w = bd.shape[0]
    xb = x.astype(BF)
    return jnp.concatenate(
        [jnp.dot(xb[:, j:j + w], bd, preferred_element_type=F32)
         for j in range(0, x.shape[1], w)], axis=1)


def _rwkv_kernel(p_ref, mu_ref, w0_ref, w2_ref, a0_ref, a2_ref, g2_ref, kk_ref, ka_ref,
                 rk_ref, lnw_ref, lnb_ref, bd_ref, tri_ref, o_ref,
                 carry_ref, state_ref, y_scr):
    t = pl.program_id(0)
    nb = p_ref.shape[0]
    c = CHUNK
    slab = 2 * HEAD

    @pl.when(t == 0)
    def _():
        carry_ref[...] = jnp.zeros_like(carry_ref)
        state_ref[...] = jnp.zeros_like(state_ref)

    p = jnp.concatenate([p_ref[bi] for bi in range(nb)], axis=0)
    row = lax.broadcasted_iota(jnp.int32, p.shape, 0)
    prev = pltpu.roll(p, 1, 0)
    for bi in range(nb):
        prev = jnp.where(row == bi * c, jnp.broadcast_to(carry_ref[bi], p.shape), prev)
        carry_ref[bi] = p[(bi + 1) * c - 1:(bi + 1) * c, :]
    ps = p + (prev - p) * mu_ref[...]

    r = ps[:, 0:W_B]
    k = ps[:, W_B:2 * W_B]
    v = ps[:, 2 * W_B:3 * W_B]
    lora_in = ps[:, 3 * W_B:3 * W_B + DECAY_LORA + A_LORA]
    cg = ps[:, 3 * W_B + DECAY_LORA + A_LORA:]

    u = w0_ref[...] + _mm(jnp.tanh(lora_in), w2_ref[...])
    lw = -math.exp(-0.5) * jax.nn.sigmoid(u)
    a = jax.nn.sigmoid(a0_ref[...] + _mm(lora_in, a2_ref[...]))
    g = _mm(jax.nn.sigmoid(cg), g2_ref[...])

    bd = bd_ref[...]
    kk = k * kk_ref[...]
    kk = kk * lax.rsqrt(jnp.maximum(_seg_sum(kk * kk, bd), 1e-24))
    k = k * (1.0 + (a - 1.0) * ka_ref[...])
    aa = -kk
    bb = kk * a

    tri = tri_ref[...]
    cum = sum(jnp.dot(tri, part, preferred_element_type=F32) for part in _split2(lw))
    last = jnp.concatenate(
        [jnp.broadcast_to(cum[(bi + 1) * c - 1:(bi + 1) * c, :], (c, W_B)) for bi in range(nb)],
        axis=0)
    inv = jnp.exp(-cum)
    tail = jnp.exp(last - cum)
    at = aa * jnp.exp(cum - lw)
    bt = bb * inv
    kt = k * inv
    rt = r * jnp.exp(cum)
    bh = bb * tail
    kh = k * tail
    w_end = jnp.exp(last)
    bonus = _seg_sum(r * k * rk_ref[...], bd) * v

    lane = lax.broadcasted_iota(jnp.int32, (c, slab), 1)
    lane2 = lax.broadcasted_iota(jnp.int32, (2 * c, slab), 1)
    row2 = lax.broadcasted_iota(jnp.int32, (2 * c, slab), 0)
    half = [lane < HEAD, lane >= HEAD]
    half2 = [lane2 < HEAD, lane2 >= HEAD]
    rl = row2 & (c - 1)
    cl = lane2 & (c - 1)
    mask4 = jnp.logical_or(rl > cl, jnp.logical_and(row2 >= c, rl == cl))
    zeros_top = jnp.zeros((c, slab), BF)

    chains = [(bi, pp, e) for bi in range(nb) for pp in range(N_HEAD // 2) for e in range(2)]
    n = range(len(chains))

    def blk(x, ch):
        bi, pp, _ = ch
        return x[bi * c:(bi + 1) * c, pp * slab:(pp + 1) * slab]

    ar = [jnp.concatenate([blk(at, ch), blk(rt, ch)], axis=0) for ch in chains]
    bk = [jnp.concatenate([blk(bt, ch), blk(kt, ch)], axis=0).astype(BF) for ch in chains]
    bkh = [jnp.concatenate([blk(bh, ch), blk(kh, ch)], axis=0).astype(BF) for ch in chains]
    vsw = [pltpu.roll(blk(v, ch), HEAD, 1) for ch in chains]
    vx = [jnp.where(half[1 - chains[i][2]], vsw[i], 0.0).astype(BF) for i in n]
    a4 = [jnp.where(mask4,
                    _mm_nt(jnp.where(half2[chains[i][2]], ar[i], 0.0), bk[i]), 0.0).astype(BF)
          for i in n]
    a_ab = [a4[i][0:c, 0:c].astype(F32) for i in n]
    x_o = [jnp.dot(a4[i][0:c, :], jnp.concatenate([zeros_top, vx[i]], axis=0),
                   preferred_element_type=F32) for i in n]
    ta = [jnp.where(half[chains[i][2]], blk(at, chains[i]), x_o[i]) for i in n]
    pw = a_ab
    for _ in range(int(math.log2(c)) - 1):
        both = [_mm(pw[i], jnp.concatenate([ta[i], pw[i]], axis=1)) for i in n]
        ta = [ta[i] + both[i][:, 0:slab] for i in n]
        pw = [both[i][:, slab:slab + c] for i in n]
    ta = [ta[i] + _mm(pw[i], ta[i]) for i in n]
    ry = [jnp.dot(a4[i][c:2 * c, :], jnp.concatenate([ta[i].astype(BF), vx[i]], axis=0),
                  preferred_element_type=F32) for i in n]
    gm = [jnp.concatenate(
        [ry[i] + jnp.where(half[chains[i][2]], blk(rt, chains[i]), 0.0), ta[i]], axis=0)
        for i in n]
    s0 = [state_ref[ch[0], 2 * ch[1] + ch[2]] for ch in chains]
    yu = [_mm_nt(gm[i], s0[i]) + gm[i] for i in n]
    for i in n:
        bi, pp, e = chains[i]
        uv = jnp.concatenate([yu[i][c:2 * c, :].astype(BF), vx[i]], axis=0)
        upd = lax.dot_general(uv, bkh[i], (((0,), (0,)), ((), ())), preferred_element_type=F32)
        keep = jnp.logical_and(half2[e], (row2 >= HEAD) if e == 0 else (row2 < HEAD))
        state_ref[bi, 2 * pp + e] = jnp.where(keep, s0[i] * blk(w_end, chains[i])[0:1, :] + upd, 0.0)
    for bi in range(nb):
        for pp in range(N_HEAD // 2):
            i0 = chains.index((bi, pp, 0))
            i1 = chains.index((bi, pp, 1))
            both = jnp.where(half[1], yu[i0][0:c, :], yu[i1][0:c, :])
            y_scr[bi * c:(bi + 1) * c, pp * slab:(pp + 1) * slab] = pltpu.roll(both, HEAD, 1)

    y = y_scr[...]
    yc = y - _seg_sum(y, bd) * (1.0 / HEAD)
    var = _seg_sum(yc * yc, bd) * (1.0 / HEAD)
    yn = yc * lax.rsqrt(var + GN_EPS) * lnw_ref[...] + lnb_ref[...]
    out = ((yn + bonus) * g).astype(BF)
    for bi in range(nb):
        o_ref[bi] = out[bi * c:(bi + 1) * c, :]


def _rwkv(p_rwkv, prm, b, s):
    nchunk = s // CHUNK
    const2 = lambda t: (0, 0)
    vec = pl.BlockSpec((1, W_B), const2)
    lora = DECAY_LORA + A_LORA
    rows = jnp.arange(b * CHUNK)
    tri = jnp.logical_and(rows[:, None] >= rows[None, :],
                          rows[:, None] // CHUNK == rows[None, :] // CHUNK).astype(BF)
    out = pl.pallas_call(
        _rwkv_kernel,
        grid=(nchunk,),
        in_specs=[
            pl.BlockSpec((b, CHUNK, RWKV_COLS), lambda t: (0, t, 0)),
            pl.BlockSpec((1, RWKV_COLS), const2),
            vec,
            pl.BlockSpec((lora, W_B), const2),
            vec,
            pl.BlockSpec((lora, W_B), const2),
            pl.BlockSpec((GATE_LORA, W_B), const2),
            vec, vec, vec, vec, vec,
            pl.BlockSpec((SEG_W, SEG_W), const2),
            pl.BlockSpec((b * CHUNK, b * CHUNK), const2),
        ],
        out_specs=pl.BlockSpec((b, CHUNK, W_B), lambda t: (0, t, 0)),
        out_shape=jax.ShapeDtypeStruct((b, s, W_B), BF),
        scratch_shapes=[pltpu.VMEM((b, 1, RWKV_COLS), F32),
                        pltpu.VMEM((b, N_HEAD, 2 * HEAD, 2 * HEAD), F32),
                        pltpu.VMEM((b * CHUNK, W_B), F32)],
        compiler_params=pltpu.CompilerParams(
            dimension_semantics=("arbitrary",), vmem_limit_bytes=VMEM_LIMIT),
        name="rwkv7",
    )(p_rwkv.reshape(b, s, RWKV_COLS), prm["mu"], prm["w0"], prm["w2"], prm["a0"], prm["a2"],
      prm["g2"], prm["k_k"], prm["k_a"], prm["r_k"], prm["ln_w"], prm["ln_b"], prm["bd"], tri)
    return out.reshape(b * s, W_B)


def _merge_kernel(ya_ref, yb_ref, gate_ref, x_ref, pa_ref, pb_ref, wo_ref, g_ref, o_ref):
    ma = jnp.dot(ya_ref[...], pa_ref[...], preferred_element_type=F32)
    mb = jnp.dot(yb_ref[...], pb_ref[...], preferred_element_type=F32)
    merged = (gate_ref[:, 0:D_MODEL].astype(F32) * ma
              + gate_ref[:, D_MODEL:2 * D_MODEL].astype(F32) * mb)
    z = jnp.dot(merged.astype(BF), wo_ref[...], preferred_element_type=F32)
    o_ref[...] = x_ref[...] + _rms(z, g_ref[...])


def _merge(ya, yb, gates, x2, pa, pb, wo, g):
    n = x2.shape[0]
    const = lambda i: (0, 0)
    row = lambda i: (i, 0)
    return pl.pallas_call(
        _merge_kernel,
        grid=(n // ROW_TILE,),
        in_specs=[
            pl.BlockSpec((ROW_TILE, W_A), row),
            pl.BlockSpec((ROW_TILE, W_B), row),
            pl.BlockSpec((ROW_TILE, GATE_COLS), row),
            pl.BlockSpec((ROW_TILE, D_MODEL), row),
            pl.BlockSpec((W_A, D_MODEL), const),
            pl.BlockSpec((W_B, D_MODEL), const),
            pl.BlockSpec((D_MODEL, D_MODEL), const),
            pl.BlockSpec((1, D_MODEL), const),
        ],
        out_specs=pl.BlockSpec((ROW_TILE, D_MODEL), row),
        out_shape=jax.ShapeDtypeStruct((n, D_MODEL), F32),
        compiler_params=pltpu.CompilerParams(
            dimension_semantics=("arbitrary",), vmem_limit_bytes=VMEM_LIMIT),
        name="merge_out",
    )(ya, yb, gates, x2, pa, pb, wo, g)


FF_TILE = 1024


def _ffn_kernel(x_ref, g1_ref, wu_ref, wd_ref, g2_ref, o_ref):
    x = x_ref[...]
    hf = _rms(x, g1_ref[...]).astype(BF)
    acc = jnp.zeros(x.shape, F32)
    for c in range(0, D_FF, FF_TILE):
        u = jnp.dot(hf, wu_ref[:, c:c + FF_TILE], preferred_element_type=F32)
        u = jnp.maximum(u, 0.0)
        u = (u * u).astype(BF)
        acc = acc + jnp.dot(u, wd_ref[c:c + FF_TILE, :], preferred_element_type=F32)
    o_ref[...] = x + _rms(acc, g2_ref[...])


def _ffn(x2, g1, wu, wd, g2):
    n = x2.shape[0]
    const = lambda i: (0, 0)
    row = lambda i: (i, 0)
    return pl.pallas_call(
        _ffn_kernel,
        grid=(n // ROW_TILE,),
        in_specs=[
            pl.BlockSpec((ROW_TILE, D_MODEL), row),
            pl.BlockSpec((1, D_MODEL), const),
            pl.BlockSpec((D_MODEL, D_FF), const),
            pl.BlockSpec((D_FF, D_MODEL), const),
            pl.BlockSpec((1, D_MODEL), const),
        ],
        out_specs=pl.BlockSpec((ROW_TILE, D_MODEL), row),
        out_shape=jax.ShapeDtypeStruct((n, D_MODEL), F32),
        compiler_params=pltpu.CompilerParams(
            dimension_semantics=("arbitrary",), vmem_limit_bytes=VMEM_LIMIT),
        name="ffn",
    )(x2, g1, wu, wd, g2)


def _layer(x2, b, s, lp):
    row = lambda a: a.reshape(1, -1).astype(F32)
    qkv, p_rwkv, gates = _inproj(x2, row(lp["pre_mix_g"]), lp["w_in"].astype(BF),
                                 row(lp["gate_bias"]))
    ya = _attention(qkv, _attn_bias_base(lp["rel_bias"]), b, s)

    zeros_lora = jnp.zeros((DECAY_LORA, W_B), BF)
    blk = jnp.arange(SEG_W) // HEAD
    prm = {
        "mu": row(lp["shift_mu"]),
        "w0": row(lp["w0"]),
        "w2": jnp.concatenate([lp["w2"].astype(BF), zeros_lora], axis=0),
        "a0": row(lp["a0"]),
        "a2": jnp.concatenate([zeros_lora, lp["a2"].astype(BF)], axis=0),
        "g2": lp["g2"].astype(BF),
        "k_k": row(lp["k_k"]),
        "k_a": row(lp["k_a"]),
        "r_k": row(lp["r_k"]),
        "ln_w": row(lp["ln_x_w"]),
        "ln_b": row(lp["ln_x_b"]),
        "bd": (blk[:, None] == blk[None, :]).astype(BF),
    }
    yb = _rwkv(p_rwkv, prm, b, s)

    x2 = _merge(ya, yb, gates, x2, lp["proj_a"].astype(BF), lp["proj_b"].astype(BF),
                lp["w_out"].astype(BF), row(lp["post_mix_g"]))
    x2 = _ffn(x2, row(lp["pre_ffn_g"]), lp["w_up"].astype(BF), lp["w_down"].astype(BF),
              row(lp["post_ffn_g"]))
    return x2


@jax.jit
def _forward(x, params):
    b, s, d = x.shape
    assert d == D_MODEL and s % ROW_TILE == 0
    x2 = x.reshape(b * s, d)
    depth = params["w_in"].shape[0]
    for l in range(depth):
        lp = {name: val[l] for name, val in params.items()}
        x2 = _layer(x2, b, s, lp)
    return x2.reshape(b, s, d)


def kernel(x, pre_mix_g, w_in, gate_bias, rel_bias, shift_mu, w0, w2, a0, a2, g2, k_k, k_a, r_k, ln_x_w, ln_x_b, proj_a, proj_b, w_out, post_mix_g, pre_ffn_g, w_up, w_down, post_ffn_g):
    params = dict(pre_mix_g=pre_mix_g, w_in=w_in, gate_bias=gate_bias, rel_bias=rel_bias,
                  shift_mu=shift_mu, w0=w0, w2=w2, a0=a0, a2=a2, g2=g2, k_k=k_k, k_a=k_a,
                  r_k=r_k, ln_x_w=ln_x_w, ln_x_b=ln_x_b, proj_a=proj_a, proj_b=proj_b,
                  w_out=w_out, post_mix_g=post_mix_g, pre_ffn_g=pre_ffn_g, w_up=w_up,
                  w_down=w_down, post_ffn_g=post_ffn_g)
    return _forward(x, params)
```

```python
import functools
import math

import jax
import jax.numpy as jnp
from jax import lax
from jax.experimental import pallas as pl
from jax.experimental.pallas import tpu as pltpu

BF = jnp.bfloat16
F32 = jnp.float32

D_MODEL = 1024
CHUNK = 64
BAND_PREV = 8
REL_CLIP = 256
W_A = D_MODEL // 2
HEAD = 64
N_HEAD = W_A // HEAD
W_B = D_MODEL // 2
DECAY_LORA = 64
A_LORA = 64
GATE_LORA = 128
D_FF = 4 * D_MODEL
ATT_COLS = 3 * W_A
RWKV_COLS = 3 * W_B + DECAY_LORA + A_LORA + GATE_LORA
GATE_COLS = 2 * D_MODEL
IN_COLS = ATT_COLS + RWKV_COLS + GATE_COLS
RMS_EPS = 1e-6
GN_EPS = HEAD * 1e-5
NEG_INF = -1e30

ROW_TILE = 512
ATT_Q_SUB = 128
ATT_K_WIN = ATT_Q_SUB + BAND_PREV * CHUNK
ATT_BASE = ATT_K_WIN + ATT_Q_SUB
SEG_W = 256
RWKV_TILE = 128
VMEM_LIMIT = 56 * 1024 * 1024


def _mm(a, b):
    return jnp.dot(a.astype(BF), b.astype(BF), preferred_element_type=F32)


def _mm_nt(a, b):
    return lax.dot_general(a.astype(BF), b.astype(BF), (((1,), (1,)), ((), ())),
                           preferred_element_type=F32)


def _mm_tn(a, b):
    return lax.dot_general(a.astype(BF), b.astype(BF), (((0,), (0,)), ((), ())),
                           preferred_element_type=F32)


def _rms(x, g):
    ms = jnp.mean(x * x, axis=-1, keepdims=True)
    return x * lax.rsqrt(ms + RMS_EPS) * g


def _inproj_kernel(x_ref, g_ref, w_ref, gb_ref, qkv_ref, pr_ref, gate_ref):
    h = _rms(x_ref[...], g_ref[...]).astype(BF)
    for c in range(0, ATT_COLS, 512):
        acc = jnp.dot(h, w_ref[:, c:c + 512], preferred_element_type=F32)
        if c < W_A:
            acc = acc * (HEAD ** -0.5)
        qkv_ref[:, c:c + 512] = acc.astype(BF)
    for c in range(0, RWKV_COLS, 256):
        o = ATT_COLS + c
        pr_ref[:, c:c + 256] = jnp.dot(h, w_ref[:, o:o + 256], preferred_element_type=F32)
    for c in range(0, GATE_COLS, 512):
        o = ATT_COLS + RWKV_COLS + c
        z = jnp.dot(h, w_ref[:, o:o + 512], preferred_element_type=F32) + gb_ref[:, c:c + 512]
        gate_ref[:, c:c + 512] = jax.nn.sigmoid(z).astype(BF)


def _inproj(x2, g, w_bf, gate_bias):
    n = x2.shape[0]
    const = lambda i: (0, 0)
    row = lambda i: (i, 0)
    return pl.pallas_call(
        _inproj_kernel,
        grid=(n // ROW_TILE,),
        in_specs=[
            pl.BlockSpec((ROW_TILE, D_MODEL), row),
            pl.BlockSpec((1, D_MODEL), const),
            pl.BlockSpec((D_MODEL, IN_COLS), const),
            pl.BlockSpec((1, GATE_COLS), const),
        ],
        out_specs=[
            pl.BlockSpec((ROW_TILE, ATT_COLS), row),
            pl.BlockSpec((ROW_TILE, RWKV_COLS), row),
            pl.BlockSpec((ROW_TILE, GATE_COLS), row),
        ],
        out_shape=[
            jax.ShapeDtypeStruct((n, ATT_COLS), BF),
            jax.ShapeDtypeStruct((n, RWKV_COLS), F32),
            jax.ShapeDtypeStruct((n, GATE_COLS), BF),
        ],
        compiler_params=pltpu.CompilerParams(
            dimension_semantics=("arbitrary",), vmem_limit_bytes=VMEM_LIMIT),
        name="inproj",
    )(x2, g, w_bf, gate_bias)


def _attn_kernel(q_ref, kp_ref, kc_ref, vp_ref, vc_ref, base_ref, o_ref, kwin, vwin, bias_scr):
    i = pl.program_id(1)

    @pl.when(jnp.logical_and(pl.program_id(0) == 0, i == 0))
    def _():
        qi = lax.broadcasted_iota(jnp.int32, (ATT_Q_SUB, ATT_K_WIN), 0)
        kj = lax.broadcasted_iota(jnp.int32, (ATT_Q_SUB, ATT_K_WIN), 1)
        dchunk = (BAND_PREV + qi // CHUNK) - kj // CHUNK
        band = jnp.logical_and(dchunk >= 0, dchunk <= BAND_PREV)
        for h in range(N_HEAD):
            rows = jnp.broadcast_to(base_ref[h:h + 1, :], (ATT_Q_SUB, ATT_BASE))
            toep = pltpu.roll(rows, 0, 1, stride=1, stride_axis=0)
            bias_scr[h] = jnp.where(band, toep[:, 0:ATT_K_WIN], NEG_INF)

    kwin[0:ROW_TILE, :] = kp_ref[...]
    kwin[ROW_TILE:2 * ROW_TILE, :] = kc_ref[...]
    vwin[0:ROW_TILE, :] = vp_ref[...]
    vwin[ROW_TILE:2 * ROW_TILE, :] = vc_ref[...]
    lane = lax.broadcasted_iota(jnp.int32, (ATT_Q_SUB, 2 * HEAD), 1)
    col = lax.broadcasted_iota(jnp.int32, (ATT_Q_SUB, ATT_K_WIN), 1)
    for j in range(ROW_TILE // ATT_Q_SUB):
        r0 = j * ATT_Q_SUB
        key_ok = jnp.logical_or(i > 0, col >= ROW_TILE - r0)
        hs = range(N_HEAD)
        cs = [(h // 2) * 2 * HEAD for h in hs]
        ss = []
        for h in hs:
            q2 = q_ref[r0:r0 + ATT_Q_SUB, cs[h]:cs[h] + 2 * HEAD]
            in_head = (lane >= HEAD) if h % 2 else (lane < HEAD)
            qm = jnp.where(in_head, q2, jnp.zeros_like(q2))
            ss.append(lax.dot_general(qm, kwin[r0:r0 + ATT_K_WIN, cs[h]:cs[h] + 2 * HEAD],
                                      (((1,), (1,)), ((), ())), preferred_element_type=F32))
        exs, ls = [], []
        for h in hs:
            s = jnp.where(key_ok, ss[h] + bias_scr[h], NEG_INF)
            m = jnp.max(s, axis=-1, keepdims=True)
            ex = jnp.exp(s - m)
            ls.append(jnp.sum(ex, axis=-1, keepdims=True))
            exs.append(ex.astype(BF))
        outs = [jnp.dot(exs[h], vwin[r0:r0 + ATT_K_WIN, cs[h]:cs[h] + 2 * HEAD],
                        preferred_element_type=F32) / ls[h] for h in hs]
        for p in range(N_HEAD // 2):
            o = jnp.where(lane < HEAD, outs[2 * p], outs[2 * p + 1])
            o_ref[r0:r0 + ATT_Q_SUB, cs[2 * p]:cs[2 * p] + 2 * HEAD] = o.astype(BF)


def _attn_bias_base(rel_bias):
    pos = jnp.arange(ATT_BASE)
    d = jnp.where(pos < ATT_K_WIN, pos, pos - ATT_BASE)
    idx = jnp.clip(BAND_PREV * CHUNK - d, -(CHUNK - 1), REL_CLIP) + (CHUNK - 1)
    return rel_bias.astype(F32)[:, idx]


def _attention(qkv, bias_base, b, s):
    n = b * s
    nblk = s // ROW_TILE
    blk = (ROW_TILE, W_A)
    return pl.pallas_call(
        _attn_kernel,
        grid=(b, nblk),
        in_specs=[
            pl.BlockSpec(blk, lambda bi, i: (bi * nblk + i, 0)),
            pl.BlockSpec(blk, lambda bi, i: (bi * nblk + jnp.maximum(i - 1, 0), 1)),
            pl.BlockSpec(blk, lambda bi, i: (bi * nblk + i, 1)),
            pl.BlockSpec(blk, lambda bi, i: (bi * nblk + jnp.maximum(i - 1, 0), 2)),
            pl.BlockSpec(blk, lambda bi, i: (bi * nblk + i, 2)),
            pl.BlockSpec((N_HEAD, ATT_BASE), lambda bi, i: (0, 0)),
        ],
        out_specs=pl.BlockSpec(blk, lambda bi, i: (bi * nblk + i, 0)),
        out_shape=jax.ShapeDtypeStruct((n, W_A), BF),
        scratch_shapes=[pltpu.VMEM((2 * ROW_TILE, W_A), BF),
                        pltpu.VMEM((2 * ROW_TILE, W_A), BF),
                        pltpu.VMEM((N_HEAD, ATT_Q_SUB, ATT_K_WIN), F32)],
        compiler_params=pltpu.CompilerParams(
            dimension_semantics=("arbitrary", "arbitrary"), vmem_limit_bytes=VMEM_LIMIT),
        name="band_attn",
    )(qkv, qkv, qkv, qkv, qkv, bias_base)


def _split2(x):
    hi = x.astype(BF)
    lo = (x - hi.astype(F32)).astype(BF)
    return hi, lo


def _seg_sum(x, bd):
    w = bd.shape[0]
    xb = x.astype(BF)
    return jnp.concatenate(
        [jnp.dot(xb[:, j:j + w], bd, preferred_element_type=F32)
         for j in range(0, x.shape[1], w)], axis=1)


def _rwkv_kernel(p0_ref, p_ref, mu_ref, w0_ref, w2_ref, a0_ref, a2_ref, g2_ref, kk_ref, ka_ref,
                 rk_ref, lnw_ref, lnb_ref, bd_ref, tri_ref, o_ref,
                 carry_ref, state_ref, y_scr, ar_scr, we_scr, arm_scr, vx_scr, bk_scr, bkh_scr,
                 gb_scr):
    t = pl.program_id(0)
    nb, tt = p_ref.shape[0], p_ref.shape[1]
    c = CHUNK
    nsub = tt // c
    nblk = nb * nsub
    slab = 2 * HEAD
    bd = bd_ref[...]

    lane = lax.broadcasted_iota(jnp.int32, (c, slab), 1)
    lane2 = lax.broadcasted_iota(jnp.int32, (2 * c, slab), 1)
    row2 = lax.broadcasted_iota(jnp.int32, (2 * c, slab), 0)
    half = [lane < HEAD, lane >= HEAD]
    half2 = [lane2 < HEAD, lane2 >= HEAD]
    pairs = [(g, pp) for g in range(nblk) for pp in range(N_HEAD // 2)]
    chains = [(g, pp, e) for (g, pp) in pairs for e in range(2)]
    n = range(len(chains))

    def prep(src_ref, slot):
        st = {}

        def shift():
            p = jnp.concatenate([src_ref[bi] for bi in range(nb)], axis=0)
            row = lax.broadcasted_iota(jnp.int32, p.shape, 0)
            prev = pltpu.roll(p, 1, 0)
            for bi in range(nb):
                prev = jnp.where(row == bi * tt, jnp.broadcast_to(carry_ref[bi], p.shape), prev)
                carry_ref[bi] = p[(bi + 1) * tt - 1:(bi + 1) * tt, :]
            ps = p + (prev - p) * mu_ref[...]
            st["r"] = ps[:, 0:W_B]
            st["k"] = ps[:, W_B:2 * W_B]
            st["v"] = ps[:, 2 * W_B:3 * W_B]
            lora_in = ps[:, 3 * W_B:3 * W_B + DECAY_LORA + A_LORA]
            cg = ps[:, 3 * W_B + DECAY_LORA + A_LORA:]
            st["tanh"] = jnp.tanh(lora_in).astype(BF)
            st["lora"] = lora_in.astype(BF)
            st["sig_cg"] = jax.nn.sigmoid(cg).astype(BF)
            st["kk"] = st["k"] * kk_ref[...]

        def lora():
            u = w0_ref[...] + jnp.dot(st["tanh"], w2_ref[...], preferred_element_type=F32)
            st["lw"] = -math.exp(-0.5) * jax.nn.sigmoid(u)
            st["a"] = jax.nn.sigmoid(
                a0_ref[...] + jnp.dot(st["lora"], a2_ref[...], preferred_element_type=F32))
            gb_scr[slot, 0] = jnp.dot(st["sig_cg"], g2_ref[...], preferred_element_type=F32)

        def knorm():
            kk = st["kk"]
            kk = kk * lax.rsqrt(jnp.maximum(_seg_sum(kk * kk, bd), 1e-24))
            a = st["a"]
            st["k2"] = st["k"] * (1.0 + (a - 1.0) * ka_ref[...])
            st["aa"] = -kk
            st["bb"] = kk * a

        def cumsum():
            tri = tri_ref[...]
            st["cum"] = sum(jnp.dot(tri, part, preferred_element_type=F32)
                            for part in _split2(st["lw"]))

        def bonus():
            gb_scr[slot, 1] = _seg_sum(st["r"] * st["k2"] * rk_ref[...], bd) * st["v"]

        def stage():
            cum, lw, v = st["cum"], st["lw"], st["v"]
            last = jnp.concatenate(
                [jnp.broadcast_to(cum[(g + 1) * c - 1:(g + 1) * c, :], (c, W_B))
                 for g in range(nblk)], axis=0)
            inv = jnp.exp(-cum)
            tail = jnp.exp(last - cum)
            at = st["aa"] * jnp.exp(cum - lw)
            rt = st["r"] * jnp.exp(cum)
            bt = st["bb"] * inv
            kt = st["k2"] * inv
            bh = st["bb"] * tail
            kh = st["k2"] * tail
            ar_scr[slot, 0] = at
            ar_scr[slot, 1] = rt
            for g in range(nblk):
                we_scr[slot, g] = jnp.exp(cum[(g + 1) * c - 1:(g + 1) * c, :])
            for j, (g, pp) in enumerate(pairs):
                rs = slice(g * c, (g + 1) * c)
                ls = slice(pp * slab, (pp + 1) * slab)
                bk_scr[slot, j] = jnp.concatenate([bt[rs, ls], kt[rs, ls]], axis=0).astype(BF)
                bkh_scr[slot, j] = jnp.concatenate([bh[rs, ls], kh[rs, ls]], axis=0).astype(BF)
                ar = jnp.concatenate([at[rs, ls], rt[rs, ls]], axis=0)
                vsw = pltpu.roll(v[rs, ls], HEAD, 1)
                for e in range(2):
                    arm_scr[slot, 2 * j + e] = jnp.where(half2[e], ar, 0.0).astype(BF)
                    vx_scr[slot, 2 * j + e] = jnp.where(half[1 - e], vsw, 0.0).astype(BF)

        return shift, lora, knorm, cumsum, bonus, stage

    @pl.when(t == 0)
    def _():
        carry_ref[...] = jnp.zeros_like(carry_ref)
        state_ref[...] = jnp.zeros_like(state_ref)
        y_scr[...] = jnp.zeros_like(y_scr)
        gb_scr[1] = jnp.zeros(gb_scr.shape[1:], F32)
        for phase in prep(p0_ref, 0):
            phase()

    cur = t % 2
    nxt = (t + 1) % 2
    rl = row2 & (c - 1)
    cl = lane2 & (c - 1)
    mask4 = jnp.logical_or(rl > cl, jnp.logical_and(row2 >= c, rl == cl))
    zeros_top = jnp.zeros((c, slab), BF)
    pr_of = [i // 2 for i in n]
    e_of = [ch[2] for ch in chains]

    def f32_slab(field, i):
        g, pp, _ = chains[i]
        return ar_scr[cur, field, g * c:(g + 1) * c, pp * slab:(pp + 1) * slab]

    cs = {}

    def st_a4():
        cs["a4"] = [jnp.where(mask4,
                              lax.dot_general(arm_scr[cur, i], bk_scr[cur, pr_of[i]],
                                              (((1,), (1,)), ((), ())),
                                              preferred_element_type=F32), 0.0).astype(BF)
                    for i in n]

    def st_x():
        a4 = cs["a4"]
        x_o = [jnp.dot(a4[i][0:c, :], jnp.concatenate([zeros_top, vx_scr[cur, i]], axis=0),
                       preferred_element_type=F32) for i in n]
        cs["ta"] = [jnp.where(half[e_of[i]], f32_slab(0, i), x_o[i]) for i in n]
        cs["pw"] = [a4[i][0:c, 0:c] for i in n]

    def st_double():
        ta, pw = cs["ta"], cs["pw"]
        both = [_mm(pw[i], jnp.concatenate([ta[i], pw[i].astype(F32)], axis=1)) for i in n]
        cs["ta"] = [ta[i] + both[i][:, 0:slab] for i in n]
        cs["pw"] = [both[i][:, slab:slab + c] for i in n]

    def st_last_factor():
        ta, pw = cs["ta"], cs["pw"]
        cs["ta"] = [ta[i] + _mm(pw[i], ta[i]) for i in n]

    def st_ry():
        a4, ta = cs["a4"], cs["ta"]
        ry = [jnp.dot(a4[i][c:2 * c, :],
                      jnp.concatenate([ta[i].astype(BF), vx_scr[cur, i]], axis=0),
                      preferred_element_type=F32) for i in n]
        cs["gm"] = [jnp.concatenate(
            [ry[i] + jnp.where(half[e_of[i]], f32_slab(1, i), 0.0), ta[i]], axis=0) for i in n]

    def st_state():
        gm = cs["gm"]
        heads = [(bi, pp, e) for bi in range(nb) for pp in range(N_HEAD // 2) for e in range(2)]
        state = {hd: state_ref[hd[0], 2 * hd[1] + hd[2]] for hd in heads}
        yu = {}
        for ci in range(nsub):
            sel = [(chains.index((bi * nsub + ci, pp, e)), (bi, pp, e)) for (bi, pp, e) in heads]
            for i, hd in sel:
                yu[i] = _mm_nt(gm[i], state[hd]) + gm[i]
            for i, hd in sel:
                g, pp, e = chains[i]
                uv = jnp.concatenate([yu[i][c:2 * c, :].astype(BF), vx_scr[cur, i]], axis=0)
                upd = lax.dot_general(uv, bkh_scr[cur, pr_of[i]], (((0,), (0,)), ((), ())),
                                      preferred_element_type=F32)
                keep = jnp.logical_and(half2[e], (row2 >= HEAD) if e == 0 else (row2 < HEAD))
                w_end = we_scr[cur, g, :, pp * slab:(pp + 1) * slab]
                state[hd] = jnp.where(keep, state[hd] * w_end + upd, 0.0)
        for hd in heads:
            state_ref[hd[0], 2 * hd[1] + hd[2]] = state[hd]
        for j, (g, pp) in enumerate(pairs):
            both = jnp.where(half[1], yu[2 * j][0:c, :], yu[2 * j + 1][0:c, :])
            y_scr[g * c:(g + 1) * c, pp * slab:(pp + 1) * slab] = pltpu.roll(both, HEAD, 1)

    post = {}

    def gn_center():
        y_prev = y_scr[...]
        post["yc"] = y_prev - _seg_sum(y_prev, bd) * (1.0 / HEAD)

    def gn_scale_store():
        yc = post["yc"]
        var = _seg_sum(yc * yc, bd) * (1.0 / HEAD)
        yn = yc * lax.rsqrt(var + GN_EPS) * lnw_ref[...] + lnb_ref[...]
        out = ((yn + gb_scr[nxt, 1]) * gb_scr[nxt, 0]).astype(BF)
        for bi in range(nb):
            o_ref[bi] = out[bi * tt:(bi + 1) * tt, :]

    p_shift, p_lora, p_knorm, p_cumsum, p_bonus, p_stage = prep(p_ref, nxt)
    gn_center()
    st_a4()
    gn_scale_store()
    p_shift()
    st_x()
    p_lora()
    st_double()
    p_knorm()
    st_double()
    p_cumsum()
    st_double()
    p_bonus()
    st_double()
    st_double()
    st_last_factor()
    st_ry()
    st_state()
    p_stage()


def _rwkv(p_rwkv, prm, b, s):
    tt = RWKV_TILE
    nstep = s // tt
    const2 = lambda t: (0, 0)
    vec = pl.BlockSpec((1, W_B), const2)
    lora = DECAY_LORA + A_LORA
    rows = jnp.arange(b * tt)
    tri = jnp.logical_and(rows[:, None] >= rows[None, :],
                          rows[:, None] // CHUNK == rows[None, :] // CHUNK).astype(BF)
    p3 = p_rwkv.reshape(b, s, RWKV_COLS)
    n_blk = b * tt // CHUNK
    n_pair = n_blk * N_HEAD // 2
    slab = 2 * HEAD
    out = pl.pallas_call(
        _rwkv_kernel,
        grid=(nstep + 1,),
        in_specs=[
            pl.BlockSpec((b, tt, RWKV_COLS), lambda t: (0, 0, 0)),
            pl.BlockSpec((b, tt, RWKV_COLS), lambda t: (0, jnp.minimum(t + 1, nstep - 1), 0)),
            pl.BlockSpec((1, RWKV_COLS), const2),
            vec,
            pl.BlockSpec((lora, W_B), const2),
            vec,
            pl.BlockSpec((lora, W_B), const2),
            pl.BlockSpec((GATE_LORA, W_B), const2),
            vec, vec, vec, vec, vec,
            pl.BlockSpec((SEG_W, SEG_W), const2),
            pl.BlockSpec((b * tt, b * tt), const2),
        ],
        out_specs=pl.BlockSpec((b, tt, W_B), lambda t: (0, jnp.maximum(t - 1, 0), 0)),
        out_shape=jax.ShapeDtypeStruct((b, s, W_B), BF),
        scratch_shapes=[pltpu.VMEM((b, 1, RWKV_COLS), F32),
                        pltpu.VMEM((b, N_HEAD, slab, slab), F32),
                        pltpu.VMEM((b * tt, W_B), F32),
                        pltpu.VMEM((2, 2, b * tt, W_B), F32),
                        pltpu.VMEM((2, n_blk, 1, W_B), F32),
                        pltpu.VMEM((2, 2 * n_pair, 2 * CHUNK, slab), BF),
                        pltpu.VMEM((2, 2 * n_pair, CHUNK, slab), BF),
                        pltpu.VMEM((2, n_pair, 2 * CHUNK, slab), BF),
                        pltpu.VMEM((2, n_pair, 2 * CHUNK, slab), BF),
                        pltpu.VMEM((2, 2, b * tt, W_B), F32)],
        compiler_params=pltpu.CompilerParams(
            dimension_semantics=("arbitrary",), vmem_limit_bytes=VMEM_LIMIT),
        name="rwkv7",
    )(p3, p3, prm["mu"], prm["w0"], prm["w2"], prm["a0"], prm["a2"],
      prm["g2"], prm["k_k"], prm["k_a"], prm["r_k"], prm["ln_w"], prm["ln_b"], prm["bd"], tri)
    return out.reshape(b * s, W_B)


def _merge_kernel(ya_ref, yb_ref, gate_ref, x_ref, pa_ref, pb_ref, wo_ref, g_ref, o_ref):
    ma = jnp.dot(ya_ref[...], pa_ref[...], preferred_element_type=F32)
    mb = jnp.dot(yb_ref[...], pb_ref[...], preferred_element_type=F32)
    merged = (gate_ref[:, 0:D_MODEL].astype(F32) * ma
              + gate_ref[:, D_MODEL:2 * D_MODEL].astype(F32) * mb)
    z = jnp.dot(merged.astype(BF), wo_ref[...], preferred_element_type=F32)
    o_ref[...] = x_ref[...] + _rms(z, g_ref[...])


def _merge(ya, yb, gates, x2, pa, pb, wo, g):
    n = x2.shape[0]
    const = lambda i: (0, 0)
    row = lambda i: (i, 0)
    return pl.pallas_call(
        _merge_kernel,
        grid=(n // ROW_TILE,),
        in_specs=[
            pl.BlockSpec((ROW_TILE, W_A), row),
            pl.BlockSpec((ROW_TILE, W_B), row),
            pl.BlockSpec((ROW_TILE, GATE_COLS), row),
            pl.BlockSpec((ROW_TILE, D_MODEL), row),
            pl.BlockSpec((W_A, D_MODEL), const),
            pl.BlockSpec((W_B, D_MODEL), const),
            pl.BlockSpec((D_MODEL, D_MODEL), const),
            pl.BlockSpec((1, D_MODEL), const),
        ],
        out_specs=pl.BlockSpec((ROW_TILE, D_MODEL), row),
        out_shape=jax.ShapeDtypeStruct((n, D_MODEL), F32),
        compiler_params=pltpu.CompilerParams(
            dimension_semantics=("arbitrary",), vmem_limit_bytes=VMEM_LIMIT),
        name="merge_out",
    )(ya, yb, gates, x2, pa, pb, wo, g)


FF_TILE = 1024


def _ffn_kernel(x_ref, g1_ref, wu_ref, wd_ref, g2_ref, o_ref):
    x = x_ref[...]
    hf = _rms(x, g1_ref[...]).astype(BF)
    acc = jnp.zeros(x.shape, F32)
    for c in range(0, D_FF, FF_TILE):
        u = jnp.dot(hf, wu_ref[:, c:c + FF_TILE], preferred_element_type=F32)
        u = jnp.maximum(u, 0.0)
        u = (u * u).astype(BF)
        acc = acc + jnp.dot(u, wd_ref[c:c + FF_TILE, :], preferred_element_type=F32)
    o_ref[...] = x + _rms(acc, g2_ref[...])


def _ffn(x2, g1, wu, wd, g2):
    n = x2.shape[0]
    const = lambda i: (0, 0)
    row = lambda i: (i, 0)
    return pl.pallas_call(
        _ffn_kernel,
        grid=(n // ROW_TILE,),
        in_specs=[
            pl.BlockSpec((ROW_TILE, D_MODEL), row),
            pl.BlockSpec((1, D_MODEL), const),
            pl.BlockSpec((D_MODEL, D_FF), const),
            pl.BlockSpec((D_FF, D_MODEL), const),
            pl.BlockSpec((1, D_MODEL), const),
        ],
        out_specs=pl.BlockSpec((ROW_TILE, D_MODEL), row),
        out_shape=jax.ShapeDtypeStruct((n, D_MODEL), F32),
        compiler_params=pltpu.CompilerParams(
            dimension_semantics=("arbitrary",), vmem_limit_bytes=VMEM_LIMIT),
        name="ffn",
    )(x2, g1, wu, wd, g2)


def _layer(x2, b, s, lp):
    row = lambda a: a.reshape(1, -1).astype(F32)
    qkv, p_rwkv, gates = _inproj(x2, row(lp["pre_mix_g"]), lp["w_in"].astype(BF),
                                 row(lp["gate_bias"]))
    ya = _attention(qkv, _attn_bias_base(lp["rel_bias"]), b, s)

    zeros_lora = jnp.zeros((DECAY_LORA, W_B), BF)
    blk = jnp.arange(SEG_W) // HEAD
    prm = {
        "mu": row(lp["shift_mu"]),
        "w0": row(lp["w0"]),
        "w2": jnp.concatenate([lp["w2"].astype(BF), zeros_lora], axis=0),
        "a0": row(lp["a0"]),
        "a2": jnp.concatenate([zeros_lora, lp["a2"].astype(BF)], axis=0),
        "g2": lp["g2"].astype(BF),
        "k_k": row(lp["k_k"]),
        "k_a": row(lp["k_a"]),
        "r_k": row(lp["r_k"]),
        "ln_w": row(lp["ln_x_w"]),
        "ln_b": row(lp["ln_x_b"]),
        "bd": (blk[:, None] == blk[None, :]).astype(BF),
    }
    yb = _rwkv(p_rwkv, prm, b, s)

    x2 = _merge(ya, yb, gates, x2, lp["proj_a"].astype(BF), lp["proj_b"].astype(BF),
                lp["w_out"].astype(BF), row(lp["post_mix_g"]))
    x2 = _ffn(x2, row(lp["pre_ffn_g"]), lp["w_up"].astype(BF), lp["w_down"].astype(BF),
              row(lp["post_ffn_g"]))
    return x2


@jax.jit
def _forward(x, params):
    b, s, d = x.shape
    assert d == D_MODEL and s % ROW_TILE == 0 and s % RWKV_TILE == 0
    x2 = x.reshape(b * s, d)
    depth = params["w_in"].shape[0]
    for l in range(depth):
        lp = {name: val[l] for name, val in params.items()}
        x2 = _layer(x2, b, s, lp)
    return x2.reshape(b, s, d)


def kernel(x, pre_mix_g, w_in, gate_bias, rel_bias, shift_mu, w0, w2, a0, a2, g2, k_k, k_a, r_k, ln_x_w, ln_x_b, proj_a, proj_b, w_out, post_mix_g, pre_ffn_g, w_up, w_down, post_ffn_g):
    params = dict(pre_mix_g=pre_mix_g, w_in=w_in, gate_bias=gate_bias, rel_bias=rel_bias,
                  shift_mu=shift_mu, w0=w0, w2=w2, a0=a0, a2=a2, g2=g2, k_k=k_k, k_a=k_a,
                  r_k=r_k, ln_x_w=ln_x_w, ln_x_b=ln_x_b, proj_a=proj_a, proj_b=proj_b,
                  w_out=w_out, post_mix_g=post_mix_g, pre_ffn_g=pre_ffn_g, w_up=w_up,
                  w_down=w_down, post_ffn_g=post_ffn_g)
    return _forward(x, params)
```

```python
import math

import jax
import jax.numpy as jnp
from jax import lax
from jax.experimental import pallas as pl
from jax.experimental.pallas import tpu as pltpu

BF = jnp.bfloat16
F32 = jnp.float32

D_MODEL = 1024
CHUNK = 64
BAND_PREV = 8
REL_CLIP = 256
W_A = D_MODEL // 2
HEAD = 64
N_HEAD = W_A // HEAD
W_B = D_MODEL // 2
DECAY_LORA = 64
A_LORA = 64
GATE_LORA = 128
D_FF = 4 * D_MODEL
ATT_COLS = 3 * W_A
RWKV_COLS = 3 * W_B + DECAY_LORA + A_LORA + GATE_LORA
GATE_COLS = 2 * D_MODEL
IN_COLS = ATT_COLS + RWKV_COLS + GATE_COLS
RMS_EPS = 1e-6
GN_EPS = HEAD * 1e-5
NEG_INF = -1e30

ROW_TILE = 512
ATT_Q_SUB = 128
ATT_K_WIN = ATT_Q_SUB + BAND_PREV * CHUNK
ATT_BASE = ATT_K_WIN + ATT_Q_SUB
SEG_W = 256
PROJ_PIECE = 256
VMEM_LIMIT = 56 * 1024 * 1024


def _mm(a, b):
    return jnp.dot(a.astype(BF), b.astype(BF), preferred_element_type=F32)


def _mm_nt(a, b):
    return lax.dot_general(a.astype(BF), b.astype(BF), (((1,), (1,)), ((), ())),
                           preferred_element_type=F32)


def _rms(x, g):
    ms = jnp.mean(x * x, axis=-1, keepdims=True)
    return x * lax.rsqrt(ms + RMS_EPS) * g


def _split2(x):
    hi = x.astype(BF)
    lo = (x - hi.astype(F32)).astype(BF)
    return hi, lo


def _seg_sum(x, bd):
    w = bd.shape[0]
    xb = x.astype(BF)
    return jnp.concatenate(
        [jnp.dot(xb[:, j:j + w], bd, preferred_element_type=F32)
         for j in range(0, x.shape[1], w)], axis=1)


def _mix_kernel(x_ref, gpre_ref, w_ref, gbias_ref, mu_ref, w0_ref, w2_ref, a0_ref, a2_ref,
                g2_ref, kk_ref, ka_ref, rk_ref, lnw_ref, lnb_ref, bd_ref, tri_ref,
                qkv_ref, gate_ref, o_ref, p_scr, carry_ref, state_ref, y_scr):
    t = pl.program_id(0)
    nb = x_ref.shape[0]
    c = CHUNK
    slab = 2 * HEAD
    cur = (t + 1) % 2
    nxt = t % 2

    @pl.when(t == 0)
    def _():
        carry_ref[...] = jnp.zeros_like(carry_ref)
        state_ref[...] = jnp.zeros_like(state_ref)
        p_scr[1] = jnp.zeros(p_scr.shape[1:], F32)

    xs = jnp.concatenate([x_ref[bi] for bi in range(nb)], axis=0)
    h = _rms(xs, gpre_ref[...]).astype(BF)

    def proj_piece(c0):
        c1 = min(c0 + PROJ_PIECE, ATT_COLS if c0 < ATT_COLS else
                 ATT_COLS + RWKV_COLS if c0 < ATT_COLS + RWKV_COLS else IN_COLS)

        def run():
            acc = jnp.dot(h, w_ref[:, c0:c1], preferred_element_type=F32)
            if c0 < ATT_COLS:
                if c0 < W_A:
                    acc = acc * (HEAD ** -0.5)
                out = acc.astype(BF)
                for bi in range(nb):
                    qkv_ref[bi, :, c0:c1] = out[bi * c:(bi + 1) * c, :]
            elif c0 < ATT_COLS + RWKV_COLS:
                p_scr[nxt, :, c0 - ATT_COLS:c1 - ATT_COLS] = acc
            else:
                g0 = c0 - ATT_COLS - RWKV_COLS
                out = jax.nn.sigmoid(acc + gbias_ref[:, g0:g0 + c1 - c0]).astype(BF)
                for bi in range(nb):
                    gate_ref[bi, :, g0:g0 + c1 - c0] = out[bi * c:(bi + 1) * c, :]
        return run, c1

    pieces = []
    col = 0
    while col < IN_COLS:
        run, col = proj_piece(col)
        pieces.append(run)
    pieces = iter(pieces)

    def fill(count=1):
        for _ in range(count):
            run = next(pieces, None)
            if run is not None:
                run()

    fill()

    p = p_scr[cur]
    row = lax.broadcasted_iota(jnp.int32, p.shape, 0)
    prev = pltpu.roll(p, 1, 0)
    for bi in range(nb):
        prev = jnp.where(row == bi * c, jnp.broadcast_to(carry_ref[bi], p.shape), prev)
        carry_ref[bi] = p[(bi + 1) * c - 1:(bi + 1) * c, :]
    ps = p + (prev - p) * mu_ref[...]

    r = ps[:, 0:W_B]
    k = ps[:, W_B:2 * W_B]
    v = ps[:, 2 * W_B:3 * W_B]
    lora_in = ps[:, 3 * W_B:3 * W_B + DECAY_LORA + A_LORA]
    cg = ps[:, 3 * W_B + DECAY_LORA + A_LORA:]

    u = w0_ref[...] + _mm(jnp.tanh(lora_in), w2_ref[...])
    lw = -math.exp(-0.5) * jax.nn.sigmoid(u)
    a = jax.nn.sigmoid(a0_ref[...] + _mm(lora_in, a2_ref[...]))
    g = _mm(jax.nn.sigmoid(cg), g2_ref[...])
    fill()

    bd = bd_ref[...]
    kk = k * kk_ref[...]
    kk = kk * lax.rsqrt(jnp.maximum(_seg_sum(kk * kk, bd), 1e-24))
    fill()
    k = k * (1.0 + (a - 1.0) * ka_ref[...])
    aa = -kk
    bb = kk * a

    tri = tri_ref[...]
    cum = sum(jnp.dot(tri, part, preferred_element_type=F32) for part in _split2(lw))
    fill()
    last = jnp.concatenate(
        [jnp.broadcast_to(cum[(bi + 1) * c - 1:(bi + 1) * c, :], (c, W_B)) for bi in range(nb)],
        axis=0)
    inv = jnp.exp(-cum)
    tail = jnp.exp(last - cum)
    at = aa * jnp.exp(cum - lw)
    bt = bb * inv
    kt = k * inv
    rt = r * jnp.exp(cum)
    bh = bb * tail
    kh = k * tail
    w_end = jnp.exp(last)
    bonus = _seg_sum(r * k * rk_ref[...], bd) * v

    lane = lax.broadcasted_iota(jnp.int32, (c, slab), 1)
    lane2 = lax.broadcasted_iota(jnp.int32, (2 * c, slab), 1)
    row2 = lax.broadcasted_iota(jnp.int32, (2 * c, slab), 0)
    half = [lane < HEAD, lane >= HEAD]
    half2 = [lane2 < HEAD, lane2 >= HEAD]
    rl = row2 & (c - 1)
    cl = lane2 & (c - 1)
    mask4 = jnp.logical_or(rl > cl, jnp.logical_and(row2 >= c, rl == cl))
    zeros_top = jnp.zeros((c, slab), BF)

    chains = [(bi, pp, e) for bi in range(nb) for pp in range(N_HEAD // 2) for e in range(2)]
    n = range(len(chains))

    def blk(x, ch):
        bi, pp, _ = ch
        return x[bi * c:(bi + 1) * c, pp * slab:(pp + 1) * slab]

    ar = [jnp.concatenate([blk(at, ch), blk(rt, ch)], axis=0) for ch in chains]
    bk = [jnp.concatenate([blk(bt, ch), blk(kt, ch)], axis=0).astype(BF) for ch in chains]
    bkh = [jnp.concatenate([blk(bh, ch), blk(kh, ch)], axis=0).astype(BF) for ch in chains]
    vsw = [pltpu.roll(blk(v, ch), HEAD, 1) for ch in chains]
    vx = [jnp.where(half[1 - chains[i][2]], vsw[i], 0.0).astype(BF) for i in n]
    a4 = [jnp.where(mask4,
                    _mm_nt(jnp.where(half2[chains[i][2]], ar[i], 0.0), bk[i]), 0.0).astype(BF)
          for i in n]
    fill()
    x_o = [jnp.dot(a4[i][0:c, :], jnp.concatenate([zeros_top, vx[i]], axis=0),
                   preferred_element_type=F32) for i in n]
    ta = [jnp.where(half[chains[i][2]], blk(at, chains[i]), x_o[i]) for i in n]
    pw = [a4[i][0:c, 0:c] for i in n]
    for _ in range(int(math.log2(c)) - 1):
        fill()
        both = [jnp.dot(pw[i].astype(BF),
                        jnp.concatenate([ta[i].astype(BF), pw[i].astype(BF)], axis=1),
                        preferred_element_type=F32) for i in n]
        ta = [ta[i] + both[i][:, 0:slab] for i in n]
        pw = [both[i][:, slab:slab + c] for i in n]
    fill()
    ta = [ta[i] + _mm(pw[i], ta[i]) for i in n]
    fill()
    ry = [jnp.dot(a4[i][c:2 * c, :], jnp.concatenate([ta[i].astype(BF), vx[i]], axis=0),
                  preferred_element_type=F32) for i in n]
    gm = [jnp.concatenate(
        [ry[i] + jnp.where(half[chains[i][2]], blk(rt, chains[i]), 0.0), ta[i]], axis=0)
        for i in n]
    fill(2)
    s0 = [state_ref[ch[0], 2 * ch[1] + ch[2]] for ch in chains]
    yu = [_mm_nt(gm[i], s0[i]) + gm[i] for i in n]
    fill(2)
    for i in n:
        bi, pp, e = chains[i]
        uv = jnp.concatenate([yu[i][c:2 * c, :].astype(BF), vx[i]], axis=0)
        upd = lax.dot_general(uv, bkh[i], (((0,), (0,)), ((), ())), preferred_element_type=F32)
        keep = jnp.logical_and(half2[e], (row2 >= HEAD) if e == 0 else (row2 < HEAD))
        state_ref[bi, 2 * pp + e] = jnp.where(keep, s0[i] * blk(w_end, chains[i])[0:1, :] + upd, 0.0)
    for bi in range(nb):
        for pp in range(N_HEAD // 2):
            i0 = chains.index((bi, pp, 0))
            both = jnp.where(half[1], yu[i0][0:c, :], yu[i0 + 1][0:c, :])
            y_scr[bi * c:(bi + 1) * c, pp * slab:(pp + 1) * slab] = pltpu.roll(both, HEAD, 1)
    fill(2)

    y = y_scr[...]
    yc = y - _seg_sum(y, bd) * (1.0 / HEAD)
    fill()
    var = _seg_sum(yc * yc, bd) * (1.0 / HEAD)
    fill(IN_COLS // PROJ_PIECE)
    yn = yc * lax.rsqrt(var + GN_EPS) * lnw_ref[...] + lnb_ref[...]
    out = ((yn + bonus) * g).astype(BF)
    for bi in range(nb):
        o_ref[bi] = out[bi * c:(bi + 1) * c, :]


def _mix(x3, g_pre, w_bf, gate_bias, prm):
    b, s, _ = x3.shape
    nchunk = s // CHUNK
    const2 = lambda t: (0, 0)
    vec = pl.BlockSpec((1, W_B), const2)
    lora = DECAY_LORA + A_LORA
    rows = jnp.arange(b * CHUNK)
    tri = jnp.logical_and(rows[:, None] >= rows[None, :],
                          rows[:, None] // CHUNK == rows[None, :] // CHUNK).astype(BF)
    this = lambda t: (0, jnp.minimum(t, nchunk - 1), 0)
    qkv, gates, yb = pl.pallas_call(
        _mix_kernel,
        grid=(nchunk + 1,),
        in_specs=[
            pl.BlockSpec((b, CHUNK, D_MODEL), this),
            pl.BlockSpec((1, D_MODEL), const2),
            pl.BlockSpec((D_MODEL, IN_COLS), const2),
            pl.BlockSpec((1, GATE_COLS), const2),
            pl.BlockSpec((1, RWKV_COLS), const2),
            vec,
            pl.BlockSpec((lora, W_B), const2),
            vec,
            pl.BlockSpec((lora, W_B), const2),
            pl.BlockSpec((GATE_LORA, W_B), const2),
            vec, vec, vec, vec, vec,
            pl.BlockSpec((SEG_W, SEG_W), const2),
            pl.BlockSpec((b * CHUNK, b * CHUNK), const2),
        ],
        out_specs=[
            pl.BlockSpec((b, CHUNK, ATT_COLS), this),
            pl.BlockSpec((b, CHUNK, GATE_COLS), this),
            pl.BlockSpec((b, CHUNK, W_B), lambda t: (0, jnp.maximum(t - 1, 0), 0)),
        ],
        out_shape=[
            jax.ShapeDtypeStruct((b, s, ATT_COLS), BF),
            jax.ShapeDtypeStruct((b, s, GATE_COLS), BF),
            jax.ShapeDtypeStruct((b, s, W_B), BF),
        ],
        scratch_shapes=[pltpu.VMEM((2, b * CHUNK, RWKV_COLS), F32),
                        pltpu.VMEM((b, 1, RWKV_COLS), F32),
                        pltpu.VMEM((b, N_HEAD, 2 * HEAD, 2 * HEAD), F32),
                        pltpu.VMEM((b * CHUNK, W_B), F32)],
        compiler_params=pltpu.CompilerParams(
            dimension_semantics=("arbitrary",), vmem_limit_bytes=VMEM_LIMIT),
        name="inproj_rwkv7",
    )(x3, g_pre, w_bf, gate_bias, prm["mu"], prm["w0"], prm["w2"], prm["a0"], prm["a2"],
      prm["g2"], prm["k_k"], prm["k_a"], prm["r_k"], prm["ln_w"], prm["ln_b"], prm["bd"], tri)
    n = b * s
    return qkv.reshape(n, ATT_COLS), gates.reshape(n, GATE_COLS), yb.reshape(n, W_B)


def _attn_kernel(q_ref, kp_ref, kc_ref, vp_ref, vc_ref, base_ref, o_ref, kwin, vwin, bias_scr):
    i = pl.program_id(1)

    @pl.when(jnp.logical_and(pl.program_id(0) == 0, i == 0))
    def _():
        qi = lax.broadcasted_iota(jnp.int32, (ATT_Q_SUB, ATT_K_WIN), 0)
        kj = lax.broadcasted_iota(jnp.int32, (ATT_Q_SUB, ATT_K_WIN), 1)
        dchunk = (BAND_PREV + qi // CHUNK) - kj // CHUNK
        band = jnp.logical_and(dchunk >= 0, dchunk <= BAND_PREV)
        for h in range(N_HEAD):
            rows = jnp.broadcast_to(base_ref[h:h + 1, :], (ATT_Q_SUB, ATT_BASE))
            toep = pltpu.roll(rows, 0, 1, stride=1, stride_axis=0)
            bias_scr[h] = jnp.where(band, toep[:, 0:ATT_K_WIN], NEG_INF)

    kwin[0:ROW_TILE, :] = kp_ref[...]
    kwin[ROW_TILE:2 * ROW_TILE, :] = kc_ref[...]
    vwin[0:ROW_TILE, :] = vp_ref[...]
    vwin[ROW_TILE:2 * ROW_TILE, :] = vc_ref[...]
    lane = lax.broadcasted_iota(jnp.int32, (ATT_Q_SUB, 2 * HEAD), 1)
    col = lax.broadcasted_iota(jnp.int32, (ATT_Q_SUB, ATT_K_WIN), 1)
    for j in range(ROW_TILE // ATT_Q_SUB):
        r0 = j * ATT_Q_SUB
        key_ok = jnp.logical_or(i > 0, col >= ROW_TILE - r0)
        hs = range(N_HEAD)
        cs = [(h // 2) * 2 * HEAD for h in hs]
        ss = []
        for h in hs:
            q2 = q_ref[r0:r0 + ATT_Q_SUB, cs[h]:cs[h] + 2 * HEAD]
            in_head = (lane >= HEAD) if h % 2 else (lane < HEAD)
            qm = jnp.where(in_head, q2, jnp.zeros_like(q2))
            ss.append(lax.dot_general(qm, kwin[r0:r0 + ATT_K_WIN, cs[h]:cs[h] + 2 * HEAD],
                                      (((1,), (1,)), ((), ())), preferred_element_type=F32))
        exs, ls = [], []
        for h in hs:
            s = jnp.where(key_ok, ss[h] + bias_scr[h], NEG_INF)
            m = jnp.max(s, axis=-1, keepdims=True)
            ex = jnp.exp(s - m)
            ls.append(jnp.sum(ex, axis=-1, keepdims=True))
            exs.append(ex.astype(BF))
        outs = [jnp.dot(exs[h], vwin[r0:r0 + ATT_K_WIN, cs[h]:cs[h] + 2 * HEAD],
                        preferred_element_type=F32) / ls[h] for h in hs]
        for p in range(N_HEAD // 2):
            o = jnp.where(lane < HEAD, outs[2 * p], outs[2 * p + 1])
            o_ref[r0:r0 + ATT_Q_SUB, cs[2 * p]:cs[2 * p] + 2 * HEAD] = o.astype(BF)


def _attn_bias_base(rel_bias):
    pos = jnp.arange(ATT_BASE)
    d = jnp.where(pos < ATT_K_WIN, pos, pos - ATT_BASE)
    idx = jnp.clip(BAND_PREV * CHUNK - d, -(CHUNK - 1), REL_CLIP) + (CHUNK - 1)
    return rel_bias.astype(F32)[:, idx]


def _attention(qkv, bias_base, b, s):
    n = b * s
    nblk = s // ROW_TILE
    blk = (ROW_TILE, W_A)
    return pl.pallas_call(
        _attn_kernel,
        grid=(b, nblk),
        in_specs=[
            pl.BlockSpec(blk, lambda bi, i: (bi * nblk + i, 0)),
            pl.BlockSpec(blk, lambda bi, i: (bi * nblk + jnp.maximum(i - 1, 0), 1)),
            pl.BlockSpec(blk, lambda bi, i: (bi * nblk + i, 1)),
            pl.BlockSpec(blk, lambda bi, i: (bi * nblk + jnp.maximum(i - 1, 0), 2)),
            pl.BlockSpec(blk, lambda bi, i: (bi * nblk + i, 2)),
            pl.BlockSpec((N_HEAD, ATT_BASE), lambda bi, i: (0, 0)),
        ],
        out_specs=pl.BlockSpec(blk, lambda bi, i: (bi * nblk + i, 0)),
        out_shape=jax.ShapeDtypeStruct((n, W_A), BF),
        scratch_shapes=[pltpu.VMEM((2 * ROW_TILE, W_A), BF),
                        pltpu.VMEM((2 * ROW_TILE, W_A), BF),
                        pltpu.VMEM((N_HEAD, ATT_Q_SUB, ATT_K_WIN), F32)],
        compiler_params=pltpu.CompilerParams(
            dimension_semantics=("arbitrary", "arbitrary"), vmem_limit_bytes=VMEM_LIMIT),
        name="band_attn",
    )(qkv, qkv, qkv, qkv, qkv, bias_base)


def _merge_kernel(ya_ref, yb_ref, gate_ref, x_ref, pa_ref, pb_ref, wo_ref, g_ref, o_ref):
    ma = jnp.dot(ya_ref[...], pa_ref[...], preferred_element_type=F32)
    mb = jnp.dot(yb_ref[...], pb_ref[...], preferred_element_type=F32)
    merged = (gate_ref[:, 0:D_MODEL].astype(F32) * ma
              + gate_ref[:, D_MODEL:2 * D_MODEL].astype(F32) * mb)
    z = jnp.dot(merged.astype(BF), wo_ref[...], preferred_element_type=F32)
    o_ref[...] = x_ref[...] + _rms(z, g_ref[...])


def _merge(ya, yb, gates, x2, pa, pb, wo, g):
    n = x2.shape[0]
    const = lambda i: (0, 0)
    row = lambda i: (i, 0)
    return pl.pallas_call(
        _merge_kernel,
        grid=(n // ROW_TILE,),
        in_specs=[
            pl.BlockSpec((ROW_TILE, W_A), row),
            pl.BlockSpec((ROW_TILE, W_B), row),
            pl.BlockSpec((ROW_TILE, GATE_COLS), row),
            pl.BlockSpec((ROW_TILE, D_MODEL), row),
            pl.BlockSpec((W_A, D_MODEL), const),
            pl.BlockSpec((W_B, D_MODEL), const),
            pl.BlockSpec((D_MODEL, D_MODEL), const),
            pl.BlockSpec((1, D_MODEL), const),
        ],
        out_specs=pl.BlockSpec((ROW_TILE, D_MODEL), row),
        out_shape=jax.ShapeDtypeStruct((n, D_MODEL), F32),
        compiler_params=pltpu.CompilerParams(
            dimension_semantics=("arbitrary",), vmem_limit_bytes=VMEM_LIMIT),
        name="merge_out",
    )(ya, yb, gates, x2, pa, pb, wo, g)


FF_TILE = 1024


def _ffn_kernel(x_ref, g1_ref, wu_ref, wd_ref, g2_ref, o_ref):
    x = x_ref[...]
    hf = _rms(x, g1_ref[...]).astype(BF)
    acc = jnp.zeros(x.shape, F32)
    for c in range(0, D_FF, FF_TILE):
        u = jnp.dot(hf, wu_ref[:, c:c + FF_TILE], preferred_element_type=F32)
        u = jnp.maximum(u, 0.0)
        u = (u * u).astype(BF)
        acc = acc + jnp.dot(u, wd_ref[c:c + FF_TILE, :], preferred_element_type=F32)
    o_ref[...] = x + _rms(acc, g2_ref[...])


def _ffn(x2, g1, wu, wd, g2):
    n = x2.shape[0]
    const = lambda i: (0, 0)
    row = lambda i: (i, 0)
    return pl.pallas_call(
        _ffn_kernel,
        grid=(n // ROW_TILE,),
        in_specs=[
            pl.BlockSpec((ROW_TILE, D_MODEL), row),
            pl.BlockSpec((1, D_MODEL), const),
            pl.BlockSpec((D_MODEL, D_FF), const),
            pl.BlockSpec((D_FF, D_MODEL), const),
            pl.BlockSpec((1, D_MODEL), const),
        ],
        out_specs=pl.BlockSpec((ROW_TILE, D_MODEL), row),
        out_shape=jax.ShapeDtypeStruct((n, D_MODEL), F32),
        compiler_params=pltpu.CompilerParams(
            dimension_semantics=("arbitrary",), vmem_limit_bytes=VMEM_LIMIT),
        name="ffn",
    )(x2, g1, wu, wd, g2)


def _layer(x2, b, s, lp):
    row = lambda a: a.reshape(1, -1).astype(F32)
    zeros_lora = jnp.zeros((DECAY_LORA, W_B), BF)
    blk = jnp.arange(SEG_W) // HEAD
    prm = {
        "mu": row(lp["shift_mu"]),
        "w0": row(lp["w0"]),
        "w2": jnp.concatenate([lp["w2"].astype(BF), zeros_lora], axis=0),
        "a0": row(lp["a0"]),
        "a2": jnp.concatenate([zeros_lora, lp["a2"].astype(BF)], axis=0),
        "g2": lp["g2"].astype(BF),
        "k_k": row(lp["k_k"]),
        "k_a": row(lp["k_a"]),
        "r_k": row(lp["r_k"]),
        "ln_w": row(lp["ln_x_w"]),
        "ln_b": row(lp["ln_x_b"]),
        "bd": (blk[:, None] == blk[None, :]).astype(BF),
    }
    qkv, gates, yb = _mix(x2.reshape(b, s, D_MODEL), row(lp["pre_mix_g"]),
                          lp["w_in"].astype(BF), row(lp["gate_bias"]), prm)
    ya = _attention(qkv, _attn_bias_base(lp["rel_bias"]), b, s)

    x2 = _merge(ya, yb, gates, x2, lp["proj_a"].astype(BF), lp["proj_b"].astype(BF),
                lp["w_out"].astype(BF), row(lp["post_mix_g"]))
    x2 = _ffn(x2, row(lp["pre_ffn_g"]), lp["w_up"].astype(BF), lp["w_down"].astype(BF),
              row(lp["post_ffn_g"]))
    return x2


@jax.jit
def _forward(x, params):
    b, s, d = x.shape
    assert d == D_MODEL and s % ROW_TILE == 0
    x2 = x.reshape(b * s, d)
    depth = params["w_in"].shape[0]
    for l in range(depth):
        lp = {name: val[l] for name, val in params.items()}
        x2 = _layer(x2, b, s, lp)
    return x2.reshape(b, s, d)


def kernel(x, pre_mix_g, w_in, gate_bias, rel_bias, shift_mu, w0, w2, a0, a2, g2, k_k, k_a, r_k, ln_x_w, ln_x_b, proj_a, proj_b, w_out, post_mix_g, pre_ffn_g, w_up, w_down, post_ffn_g):
    params = dict(pre_mix_g=pre_mix_g, w_in=w_in, gate_bias=gate_bias, rel_bias=rel_bias,
                  shift_mu=shift_mu, w0=w0, w2=w2, a0=a0, a2=a2, g2=g2, k_k=k_k, k_a=k_a,
                  r_k=r_k, ln_x_w=ln_x_w, ln_x_b=ln_x_b, proj_a=proj_a, proj_b=proj_b,
                  w_out=w_out, post_mix_g=post_mix_g, pre_ffn_g=pre_ffn_g, w_up=w_up,
                  w_down=w_down, post_ffn_g=post_ffn_g)
    return _forward(x, params)
```

```python
import math

import jax
import jax.numpy as jnp
from jax import lax
from jax.experimental import pallas as pl
from jax.experimental.pallas import tpu as pltpu

BF = jnp.bfloat16
F32 = jnp.float32

D_MODEL = 1024
CHUNK = 64
BAND_PREV = 8
REL_CLIP = 256
W_A = D_MODEL // 2
HEAD = 64
N_HEAD = W_A // HEAD
W_B = D_MODEL // 2
DECAY_LORA = 64
A_LORA = 64
GATE_LORA = 128
D_FF = 4 * D_MODEL
ATT_COLS = 3 * W_A
RWKV_COLS = 3 * W_B + DECAY_LORA + A_LORA + GATE_LORA
GATE_COLS = 2 * D_MODEL
IN_COLS = ATT_COLS + RWKV_COLS + GATE_COLS
RMS_EPS = 1e-6
GN_EPS = HEAD * 1e-5
NEG_INF = -1e30
LOG2E = math.log2(math.e)

ROW_TILE = 512
ATT_Q_SUB = 128
ATT_K_WIN = ATT_Q_SUB + BAND_PREV * CHUNK
ATT_BASE = ATT_K_WIN + ATT_Q_SUB
SEG_W = 256
PROJ_PIECE = 256
VMEM_LIMIT = 56 * 1024 * 1024


def _mm(a, b):
    return jnp.dot(a.astype(BF), b.astype(BF), preferred_element_type=F32)


def _mm_nt(a, b):
    return lax.dot_general(a.astype(BF), b.astype(BF), (((1,), (1,)), ((), ())),
                           preferred_element_type=F32)


def _rms(x, g):
    ms = jnp.mean(x * x, axis=-1, keepdims=True)
    return x * lax.rsqrt(ms + RMS_EPS) * g


def _split2(x):
    hi = x.astype(BF)
    lo = (x - hi.astype(F32)).astype(BF)
    return hi, lo


def _seg_sum(x, bd):
    w = bd.shape[0]
    xb = x.astype(BF)
    return jnp.concatenate(
        [jnp.dot(xb[:, j:j + w], bd, preferred_element_type=F32)
         for j in range(0, x.shape[1], w)], axis=1)


def _mix_kernel(x0_ref, x_ref, gpre_ref, w_ref, gbias_ref, mu_ref, w0_ref, w2_ref, a0_ref, a2_ref,
                g2_ref, kk_ref, ka_ref, rk_ref, lnw_ref, lnb_ref, bd_ref, tri_ref,
                qkv_ref, gate_ref, o_ref, h_scr, p_scr, carry_ref, state_ref, y_scr):
    t = pl.program_id(0)
    nb = x_ref.shape[0]
    c = CHUNK
    slab = 2 * HEAD
    cur = (t + 1) % 2
    nxt = t % 2

    @pl.when(t == 0)
    def _():
        carry_ref[...] = jnp.zeros_like(carry_ref)
        state_ref[...] = jnp.zeros_like(state_ref)
        p_scr[1] = jnp.zeros(p_scr.shape[1:], F32)
        x0 = jnp.concatenate([x0_ref[bi] for bi in range(nb)], axis=0)
        h_scr[...] = _rms(x0, gpre_ref[...]).astype(BF)

    h = h_scr[...]
    row_w = lax.broadcasted_iota(jnp.int32, (nb * c, PROJ_PIECE), 0)

    def proj_piece(c0):
        c1 = min(c0 + PROJ_PIECE, ATT_COLS if c0 < ATT_COLS else
                 ATT_COLS + RWKV_COLS if c0 < ATT_COLS + RWKV_COLS else IN_COLS)

        def run():
            acc = jnp.dot(h, w_ref[:, c0:c1], preferred_element_type=F32)
            if c0 < ATT_COLS:
                if c0 < W_A:
                    acc = acc * (HEAD ** -0.5 * LOG2E)
                out = acc.astype(BF)
                for bi in range(nb):
                    qkv_ref[bi, :, c0:c1] = out[bi * c:(bi + 1) * c, :]
            elif c0 < ATT_COLS + RWKV_COLS:
                cols = slice(c0 - ATT_COLS, c1 - ATT_COLS)
                prev = pltpu.roll(acc, 1, 0)
                for bi in range(nb):
                    prev = jnp.where(row_w[:, 0:c1 - c0] == bi * c,
                                     jnp.broadcast_to(carry_ref[bi, :, cols], acc.shape), prev)
                    carry_ref[bi, :, cols] = acc[(bi + 1) * c - 1:(bi + 1) * c, :]
                p_scr[nxt, :, cols] = acc + (prev - acc) * mu_ref[:, cols]
            else:
                g0 = c0 - ATT_COLS - RWKV_COLS
                out = jax.nn.sigmoid(acc + gbias_ref[:, g0:g0 + c1 - c0]).astype(BF)
                for bi in range(nb):
                    gate_ref[bi, :, g0:g0 + c1 - c0] = out[bi * c:(bi + 1) * c, :]
        return run, c1

    pieces = []
    col = 0
    while col < IN_COLS:
        run, col = proj_piece(col)
        pieces.append(run)
    pieces = iter(pieces)

    def fill(count=1):
        for _ in range(count):
            run = next(pieces, None)
            if run is not None:
                run()

    fill()

    ps = p_scr[cur]
    r = ps[:, 0:W_B]
    k = ps[:, W_B:2 * W_B]
    v = ps[:, 2 * W_B:3 * W_B]
    lora_in = ps[:, 3 * W_B:3 * W_B + DECAY_LORA + A_LORA]
    cg = ps[:, 3 * W_B + DECAY_LORA + A_LORA:]

    u = w0_ref[...] + _mm(jnp.tanh(lora_in), w2_ref[...])
    lw = -math.exp(-0.5) * jax.nn.sigmoid(u)
    a = jax.nn.sigmoid(a0_ref[...] + _mm(lora_in, a2_ref[...]))
    g = _mm(jax.nn.sigmoid(cg), g2_ref[...])
    fill()

    bd = bd_ref[...]
    kk = k * kk_ref[...]
    kk = kk * lax.rsqrt(jnp.maximum(_seg_sum(kk * kk, bd), 1e-24))
    fill()
    k = k * (1.0 + (a - 1.0) * ka_ref[...])
    aa = -kk
    bb = kk * a

    tri = tri_ref[...]
    cum = sum(jnp.dot(tri, part, preferred_element_type=F32) for part in _split2(lw))
    fill()
    last = jnp.concatenate(
        [jnp.broadcast_to(cum[(bi + 1) * c - 1:(bi + 1) * c, :], (c, W_B)) for bi in range(nb)],
        axis=0)
    inv = jnp.exp(-cum)
    tail = jnp.exp(last - cum)
    at = aa * jnp.exp(cum - lw)
    bt = bb * inv
    kt = k * inv
    rt = r * jnp.exp(cum)
    bh = bb * tail
    kh = k * tail
    w_end = jnp.exp(last)
    bonus = _seg_sum(r * k * rk_ref[...], bd) * v

    lane = lax.broadcasted_iota(jnp.int32, (c, slab), 1)
    lane2 = lax.broadcasted_iota(jnp.int32, (2 * c, slab), 1)
    row2 = lax.broadcasted_iota(jnp.int32, (2 * c, slab), 0)
    half = [lane < HEAD, lane >= HEAD]
    half2 = [lane2 < HEAD, lane2 >= HEAD]
    rl = row2 & (c - 1)
    cl = lane2 & (c - 1)
    mask4 = jnp.logical_or(rl > cl, jnp.logical_and(row2 >= c, rl == cl))
    zeros_top = jnp.zeros((c, slab), BF)

    chains = [(bi, pp, e) for bi in range(nb) for pp in range(N_HEAD // 2) for e in range(2)]
    n = range(len(chains))

    def blk(x, ch):
        bi, pp, _ = ch
        return x[bi * c:(bi + 1) * c, pp * slab:(pp + 1) * slab]

    ar = [jnp.concatenate([blk(at, ch), blk(rt, ch)], axis=0) for ch in chains]
    bk = [jnp.concatenate([blk(bt, ch), blk(kt, ch)], axis=0).astype(BF) for ch in chains]
    bkh = [jnp.concatenate([blk(bh, ch), blk(kh, ch)], axis=0).astype(BF) for ch in chains]
    vsw = [pltpu.roll(blk(v, ch), HEAD, 1) for ch in chains]
    vx = [jnp.where(half[1 - chains[i][2]], vsw[i], 0.0).astype(BF) for i in n]
    a4 = [jnp.where(mask4,
                    _mm_nt(jnp.where(half2[chains[i][2]], ar[i], 0.0), bk[i]), 0.0).astype(BF)
          for i in n]
    fill()
    x_o = [jnp.dot(a4[i][0:c, :], jnp.concatenate([zeros_top, vx[i]], axis=0),
                   preferred_element_type=F32) for i in n]
    ta = [jnp.where(half[chains[i][2]], blk(at, chains[i]), x_o[i]) for i in n]
    pw = [a4[i][0:c, 0:c] for i in n]
    for _ in range(int(math.log2(c)) - 1):
        fill()
        both = [jnp.dot(pw[i].astype(BF),
                        jnp.concatenate([ta[i].astype(BF), pw[i].astype(BF)], axis=1),
                        preferred_element_type=F32) for i in n]
        ta = [ta[i] + both[i][:, 0:slab] for i in n]
        pw = [both[i][:, slab:slab + c] for i in n]
    fill()
    ta = [ta[i] + _mm(pw[i], ta[i]) for i in n]
    fill()
    ry = [jnp.dot(a4[i][c:2 * c, :], jnp.concatenate([ta[i].astype(BF), vx[i]], axis=0),
                  preferred_element_type=F32) for i in n]
    gm = [jnp.concatenate(
        [ry[i] + jnp.where(half[chains[i][2]], blk(rt, chains[i]), 0.0), ta[i]], axis=0)
        for i in n]
    fill(2)
    s0 = [state_ref[ch[0], 2 * ch[1] + ch[2]] for ch in chains]
    yu = [_mm_nt(gm[i], s0[i]) + gm[i] for i in n]
    fill(2)
    for i in n:
        bi, pp, e = chains[i]
        uv = jnp.concatenate([yu[i][c:2 * c, :].astype(BF), vx[i]], axis=0)
        upd = lax.dot_general(uv, bkh[i], (((0,), (0,)), ((), ())), preferred_element_type=F32)
        keep = jnp.logical_and(half2[e], (row2 >= HEAD) if e == 0 else (row2 < HEAD))
        state_ref[bi, 2 * pp + e] = jnp.where(keep, s0[i] * blk(w_end, chains[i])[0:1, :] + upd, 0.0)
    for bi in range(nb):
        for pp in range(N_HEAD // 2):
            i0 = chains.index((bi, pp, 0))
            both = jnp.where(half[1], yu[i0][0:c, :], yu[i0 + 1][0:c, :])
            y_scr[bi * c:(bi + 1) * c, pp * slab:(pp + 1) * slab] = pltpu.roll(both, HEAD, 1)
    fill(2)

    y = y_scr[...]
    yc = y - _seg_sum(y, bd) * (1.0 / HEAD)
    fill()
    var = _seg_sum(yc * yc, bd) * (1.0 / HEAD)
    fill(IN_COLS // PROJ_PIECE)
    yn = yc * lax.rsqrt(var + GN_EPS) * lnw_ref[...] + lnb_ref[...]
    out = ((yn + bonus) * g).astype(BF)
    for bi in range(nb):
        o_ref[bi] = out[bi * c:(bi + 1) * c, :]

    xs = jnp.concatenate([x_ref[bi] for bi in range(nb)], axis=0)
    h_scr[...] = _rms(xs, gpre_ref[...]).astype(BF)


def _mix(x3, g_pre, w_bf, gate_bias, prm):
    b, s, _ = x3.shape
    nchunk = s // CHUNK
    const2 = lambda t: (0, 0)
    vec = pl.BlockSpec((1, W_B), const2)
    lora = DECAY_LORA + A_LORA
    rows = jnp.arange(b * CHUNK)
    tri = jnp.logical_and(rows[:, None] >= rows[None, :],
                          rows[:, None] // CHUNK == rows[None, :] // CHUNK).astype(BF)
    this = lambda t: (0, jnp.minimum(t, nchunk - 1), 0)
    ahead = lambda t: (0, jnp.minimum(t + 1, nchunk - 1), 0)
    qkv, gates, yb = pl.pallas_call(
        _mix_kernel,
        grid=(nchunk + 1,),
        in_specs=[
            pl.BlockSpec((b, CHUNK, D_MODEL), lambda t: (0, 0, 0)),
            pl.BlockSpec((b, CHUNK, D_MODEL), ahead),
            pl.BlockSpec((1, D_MODEL), const2),
            pl.BlockSpec((D_MODEL, IN_COLS), const2),
            pl.BlockSpec((1, GATE_COLS), const2),
            pl.BlockSpec((1, RWKV_COLS), const2),
            vec,
            pl.BlockSpec((lora, W_B), const2),
            vec,
            pl.BlockSpec((lora, W_B), const2),
            pl.BlockSpec((GATE_LORA, W_B), const2),
            vec, vec, vec, vec, vec,
            pl.BlockSpec((SEG_W, SEG_W), const2),
            pl.BlockSpec((b * CHUNK, b * CHUNK), const2),
        ],
        out_specs=[
            pl.BlockSpec((b, CHUNK, ATT_COLS), this),
            pl.BlockSpec((b, CHUNK, GATE_COLS), this),
            pl.BlockSpec((b, CHUNK, W_B), lambda t: (0, jnp.maximum(t - 1, 0), 0)),
        ],
        out_shape=[
            jax.ShapeDtypeStruct((b, s, ATT_COLS), BF),
            jax.ShapeDtypeStruct((b, s, GATE_COLS), BF),
            jax.ShapeDtypeStruct((b, s, W_B), BF),
        ],
        scratch_shapes=[pltpu.VMEM((b * CHUNK, D_MODEL), BF),
                        pltpu.VMEM((2, b * CHUNK, RWKV_COLS), F32),
                        pltpu.VMEM((b, 1, RWKV_COLS), F32),
                        pltpu.VMEM((b, N_HEAD, 2 * HEAD, 2 * HEAD), F32),
                        pltpu.VMEM((b * CHUNK, W_B), F32)],
        compiler_params=pltpu.CompilerParams(
            dimension_semantics=("arbitrary",), vmem_limit_bytes=VMEM_LIMIT),
        name="inproj_rwkv7",
    )(x3, x3, g_pre, w_bf, gate_bias, prm["mu"], prm["w0"], prm["w2"], prm["a0"], prm["a2"],
      prm["g2"], prm["k_k"], prm["k_a"], prm["r_k"], prm["ln_w"], prm["ln_b"], prm["bd"], tri)
    n = b * s
    return qkv.reshape(n, ATT_COLS), gates.reshape(n, GATE_COLS), yb.reshape(n, W_B)


def _attn_kernel(q_ref, kp_ref, kc_ref, vp_ref, vc_ref, base_ref, o_ref, kwin, vwin, bias_scr):
    i = pl.program_id(1)

    @pl.when(jnp.logical_and(pl.program_id(0) == 0, i == 0))
    def _():
        qi = lax.broadcasted_iota(jnp.int32, (ATT_Q_SUB, ATT_K_WIN), 0)
        kj = lax.broadcasted_iota(jnp.int32, (ATT_Q_SUB, ATT_K_WIN), 1)
        dchunk = (BAND_PREV + qi // CHUNK) - kj // CHUNK
        band = jnp.logical_and(dchunk >= 0, dchunk <= BAND_PREV)
        for h in range(N_HEAD):
            rows = jnp.broadcast_to(base_ref[h:h + 1, :], (ATT_Q_SUB, ATT_BASE))
            toep = pltpu.roll(rows, 0, 1, stride=1, stride_axis=0)
            bias_scr[h] = jnp.where(band, toep[:, 0:ATT_K_WIN] * LOG2E, NEG_INF)

    kwin[0:ROW_TILE, :] = kp_ref[...]
    kwin[ROW_TILE:2 * ROW_TILE, :] = kc_ref[...]
    vwin[0:ROW_TILE, :] = vp_ref[...]
    vwin[ROW_TILE:2 * ROW_TILE, :] = vc_ref[...]
    lane = lax.broadcasted_iota(jnp.int32, (ATT_Q_SUB, 2 * HEAD), 1)
    col = lax.broadcasted_iota(jnp.int32, (ATT_Q_SUB, ATT_K_WIN), 1)

    def block(first):
        for j in range(ROW_TILE // ATT_Q_SUB):
            r0 = j * ATT_Q_SUB
            hs = range(N_HEAD)
            cs = [(h // 2) * 2 * HEAD for h in hs]
            ss = []
            for h in hs:
                q2 = q_ref[r0:r0 + ATT_Q_SUB, cs[h]:cs[h] + 2 * HEAD]
                in_head = (lane >= HEAD) if h % 2 else (lane < HEAD)
                qm = jnp.where(in_head, q2, jnp.zeros_like(q2))
                ss.append(lax.dot_general(qm, kwin[r0:r0 + ATT_K_WIN, cs[h]:cs[h] + 2 * HEAD],
                                          (((1,), (1,)), ((), ())), preferred_element_type=F32))
            exs, ls = [], []
            for h in hs:
                s = ss[h] + bias_scr[h]
                if first:
                    s = jnp.where(col >= ROW_TILE - r0, s, NEG_INF)
                m = jnp.max(s, axis=-1, keepdims=True)
                ex = jnp.exp2(s - m)
                ls.append(jnp.sum(ex, axis=-1, keepdims=True))
                exs.append(ex.astype(BF))
            outs = [jnp.dot(exs[h], vwin[r0:r0 + ATT_K_WIN, cs[h]:cs[h] + 2 * HEAD],
                            preferred_element_type=F32) / ls[h] for h in hs]
            for p in range(N_HEAD // 2):
                o = jnp.where(lane < HEAD, outs[2 * p], outs[2 * p + 1])
                o_ref[r0:r0 + ATT_Q_SUB, cs[2 * p]:cs[2 * p] + 2 * HEAD] = o.astype(BF)

    pl.when(i == 0)(lambda: block(True))
    pl.when(i > 0)(lambda: block(False))


def _attn_bias_base(rel_bias):
    pos = jnp.arange(ATT_BASE)
    d = jnp.where(pos < ATT_K_WIN, pos, pos - ATT_BASE)
    idx = jnp.clip(BAND_PREV * CHUNK - d, -(CHUNK - 1), REL_CLIP) + (CHUNK - 1)
    return rel_bias.astype(F32)[:, idx]


def _attention(qkv, bias_base, b, s):
    n = b * s
    nblk = s // ROW_TILE
    blk = (ROW_TILE, W_A)
    return pl.pallas_call(
        _attn_kernel,
        grid=(b, nblk),
        in_specs=[
            pl.BlockSpec(blk, lambda bi, i: (bi * nblk + i, 0)),
            pl.BlockSpec(blk, lambda bi, i: (bi * nblk + jnp.maximum(i - 1, 0), 1)),
            pl.BlockSpec(blk, lambda bi, i: (bi * nblk + i, 1)),
            pl.BlockSpec(blk, lambda bi, i: (bi * nblk + jnp.maximum(i - 1, 0), 2)),
            pl.BlockSpec(blk, lambda bi, i: (bi * nblk + i, 2)),
            pl.BlockSpec((N_HEAD, ATT_BASE), lambda bi, i: (0, 0)),
        ],
        out_specs=pl.BlockSpec(blk, lambda bi, i: (bi * nblk + i, 0)),
        out_shape=jax.ShapeDtypeStruct((n, W_A), BF),
        scratch_shapes=[pltpu.VMEM((2 * ROW_TILE, W_A), BF),
                        pltpu.VMEM((2 * ROW_TILE, W_A), BF),
                        pltpu.VMEM((N_HEAD, ATT_Q_SUB, ATT_K_WIN), F32)],
        compiler_params=pltpu.CompilerParams(
            dimension_semantics=("arbitrary", "arbitrary"), vmem_limit_bytes=VMEM_LIMIT),
        name="band_attn",
    )(qkv, qkv, qkv, qkv, qkv, bias_base)


def _merge_kernel(ya_ref, yb_ref, gate_ref, x_ref, pa_ref, pb_ref, wo_ref, g_ref, o_ref):
    ma = jnp.dot(ya_ref[...], pa_ref[...], preferred_element_type=F32)
    mb = jnp.dot(yb_ref[...], pb_ref[...], preferred_element_type=F32)
    merged = (gate_ref[:, 0:D_MODEL].astype(F32) * ma
              + gate_ref[:, D_MODEL:2 * D_MODEL].astype(F32) * mb)
    z = jnp.dot(merged.astype(BF), wo_ref[...], preferred_element_type=F32)
    o_ref[...] = x_ref[...] + _rms(z, g_ref[...])


def _merge(ya, yb, gates, x2, pa, pb, wo, g):
    n = x2.shape[0]
    const = lambda i: (0, 0)
    row = lambda i: (i, 0)
    return pl.pallas_call(
        _merge_kernel,
        grid=(n // ROW_TILE,),
        in_specs=[
            pl.BlockSpec((ROW_TILE, W_A), row),
            pl.BlockSpec((ROW_TILE, W_B), row),
            pl.BlockSpec((ROW_TILE, GATE_COLS), row),
            pl.BlockSpec((ROW_TILE, D_MODEL), row),
            pl.BlockSpec((W_A, D_MODEL), const),
            pl.BlockSpec((W_B, D_MODEL), const),
            pl.BlockSpec((D_MODEL, D_MODEL), const),
            pl.BlockSpec((1, D_MODEL), const),
        ],
        out_specs=pl.BlockSpec((ROW_TILE, D_MODEL), row),
        out_shape=jax.ShapeDtypeStruct((n, D_MODEL), F32),
        compiler_params=pltpu.CompilerParams(
            dimension_semantics=("arbitrary",), vmem_limit_bytes=VMEM_LIMIT),
        name="merge_out",
    )(ya, yb, gates, x2, pa, pb, wo, g)


FF_TILE = 1024


def _ffn_kernel(x_ref, g1_ref, wu_ref, wd_ref, g2_ref, o_ref):
    x = x_ref[...]
    hf = _rms(x, g1_ref[...]).astype(BF)
    acc = jnp.zeros(x.shape, F32)
    for c in range(0, D_FF, FF_TILE):
        u = jnp.dot(hf, wu_ref[:, c:c + FF_TILE], preferred_element_type=F32)
        u = jnp.maximum(u, 0.0)
        u = (u * u).astype(BF)
        acc = acc + jnp.dot(u, wd_ref[c:c + FF_TILE, :], preferred_element_type=F32)
    o_ref[...] = x + _rms(acc, g2_ref[...])


def _ffn(x2, g1, wu, wd, g2):
    n = x2.shape[0]
    const = lambda i: (0, 0)
    row = lambda i: (i, 0)
    return pl.pallas_call(
        _ffn_kernel,
        grid=(n // ROW_TILE,),
        in_specs=[
            pl.BlockSpec((ROW_TILE, D_MODEL), row),
            pl.BlockSpec((1, D_MODEL), const),
            pl.BlockSpec((D_MODEL, D_FF), const),
            pl.BlockSpec((D_FF, D_MODEL), const),
            pl.BlockSpec((1, D_MODEL), const),
        ],
        out_specs=pl.BlockSpec((ROW_TILE, D_MODEL), row),
        out_shape=jax.ShapeDtypeStruct((n, D_MODEL), F32),
        compiler_params=pltpu.CompilerParams(
            dimension_semantics=("arbitrary",), vmem_limit_bytes=VMEM_LIMIT),
        name="ffn",
    )(x2, g1, wu, wd, g2)


def _layer(x2, b, s, lp):
    row = lambda a: a.reshape(1, -1).astype(F32)
    zeros_lora = jnp.zeros((DECAY_LORA, W_B), BF)
    blk = jnp.arange(SEG_W) // HEAD
    prm = {
        "mu": row(lp["shift_mu"]),
        "w0": row(lp["w0"]),
        "w2": jnp.concatenate([lp["w2"].astype(BF), zeros_lora], axis=0),
        "a0": row(lp["a0"]),
        "a2": jnp.concatenate([zeros_lora, lp["a2"].astype(BF)], axis=0),
        "g2": lp["g2"].astype(BF),
        "k_k": row(lp["k_k"]),
        "k_a": row(lp["k_a"]),
        "r_k": row(lp["r_k"]),
        "ln_w": row(lp["ln_x_w"]),
        "ln_b": row(lp["ln_x_b"]),
        "bd": (blk[:, None] == blk[None, :]).astype(BF),
    }
    qkv, gates, yb = _mix(x2.reshape(b, s, D_MODEL), row(lp["pre_mix_g"]),
                          lp["w_in"].astype(BF), row(lp["gate_bias"]), prm)
    ya = _attention(qkv, _attn_bias_base(lp["rel_bias"]), b, s)

    x2 = _merge(ya, yb, gates, x2, lp["proj_a"].astype(BF), lp["proj_b"].astype(BF),
                lp["w_out"].astype(BF), row(lp["post_mix_g"]))
    x2 = _ffn(x2, row(lp["pre_ffn_g"]), lp["w_up"].astype(BF), lp["w_down"].astype(BF),
              row(lp["post_ffn_g"]))
    return x2


@jax.jit
def _forward(x, params):
    b, s, d = x.shape
    assert d == D_MODEL and s % ROW_TILE == 0
    x2 = x.reshape(b * s, d)
    depth = params["w_in"].shape[0]
    for l in range(depth):
        lp = {name: val[l] for name, val in params.items()}
        x2 = _layer(x2, b, s, lp)
    return x2.reshape(b, s, d)


def kernel(x, pre_mix_g, w_in, gate_bias, rel_bias, shift_mu, w0, w2, a0, a2, g2, k_k, k_a, r_k, ln_x_w, ln_x_b, proj_a, proj_b, w_out, post_mix_g, pre_ffn_g, w_up, w_down, post_ffn_g):
    params = dict(pre_mix_g=pre_mix_g, w_in=w_in, gate_bias=gate_bias, rel_bias=rel_bias,
                  shift_mu=shift_mu, w0=w0, w2=w2, a0=a0, a2=a2, g2=g2, k_k=k_k, k_a=k_a,
                  r_k=r_k, ln_x_w=ln_x_w, ln_x_b=ln_x_b, proj_a=proj_a, proj_b=proj_b,
                  w_out=w_out, post_mix_g=post_mix_g, pre_ffn_g=pre_ffn_g, w_up=w_up,
                  w_down=w_down, post_ffn_g=post_ffn_g)
    return _forward(x, params)
```

```python
import math

import jax
import jax.numpy as jnp
from jax import lax
from jax.experimental import pallas as pl
from jax.experimental.pallas import tpu as pltpu

BF = jnp.bfloat16
F32 = jnp.float32

D_MODEL = 1024
CHUNK = 64
BAND_PREV = 8
REL_CLIP = 256
W_A = D_MODEL // 2
HEAD = 64
N_HEAD = W_A // HEAD
W_B = D_MODEL // 2
DECAY_LORA = 64
A_LORA = 64
GATE_LORA = 128
D_FF = 4 * D_MODEL
ATT_COLS = 3 * W_A
RWKV_COLS = 3 * W_B + DECAY_LORA + A_LORA + GATE_LORA
GATE_COLS = 2 * D_MODEL
IN_COLS = ATT_COLS + RWKV_COLS + GATE_COLS
RMS_EPS = 1e-6
GN_EPS = HEAD * 1e-5
NEG_INF = -1e30
LOG2E = math.log2(math.e)

ROW_TILE = 512
ATT_Q_SUB = 128
ATT_K_WIN = ATT_Q_SUB + BAND_PREV * CHUNK
ATT_BASE = ATT_K_WIN + ATT_Q_SUB
SEG_W = 256
PROJ_PIECE = 256
VMEM_LIMIT = 56 * 1024 * 1024


def _mm(a, b):
    return jnp.dot(a.astype(BF), b.astype(BF), preferred_element_type=F32)


def _mm_nt(a, b):
    return lax.dot_general(a.astype(BF), b.astype(BF), (((1,), (1,)), ((), ())),
                           preferred_element_type=F32)


def _rms(x, g):
    ms = jnp.mean(x * x, axis=-1, keepdims=True)
    return x * lax.rsqrt(ms + RMS_EPS) * g


def _split2(x):
    hi = x.astype(BF)
    lo = (x - hi.astype(F32)).astype(BF)
    return hi, lo


def _seg_sum(x, bd):
    w = bd.shape[0]
    xb = x.astype(BF)
    return jnp.concatenate(
        [jnp.dot(xb[:, j:j + w], bd, preferred_element_type=F32)
         for j in range(0, x.shape[1], w)], axis=1)


def _mix_kernel(x0_ref, x_ref, gpre_ref, w_ref, gbias_ref, mu_ref, w0_ref, w2_ref, a0_ref, a2_ref,
                g2_ref, kk_ref, ka_ref, rk_ref, lnw_ref, lnb_ref, bd_ref, tri_ref,
                qkv_ref, gate_ref, o_ref, h_scr, p_scr, carry_ref, state_ref, y_scr):
    t = pl.program_id(0)
    nb = x_ref.shape[0]
    c = CHUNK
    slab = 2 * HEAD
    cur = (t + 1) % 2
    nxt = t % 2

    @pl.when(t == 0)
    def _():
        carry_ref[...] = jnp.zeros_like(carry_ref)
        state_ref[...] = jnp.zeros_like(state_ref)
        p_scr[1] = jnp.zeros(p_scr.shape[1:], F32)
        x0 = jnp.concatenate([x0_ref[bi] for bi in range(nb)], axis=0)
        h_scr[...] = _rms(x0, gpre_ref[...]).astype(BF)

    h = h_scr[...]
    row_w = lax.broadcasted_iota(jnp.int32, (nb * c, PROJ_PIECE), 0)

    def proj_piece(c0):
        c1 = min(c0 + PROJ_PIECE, ATT_COLS if c0 < ATT_COLS else
                 ATT_COLS + RWKV_COLS if c0 < ATT_COLS + RWKV_COLS else IN_COLS)

        def run():
            acc = jnp.dot(h, w_ref[:, c0:c1], preferred_element_type=F32)
            if c0 < ATT_COLS:
                if c0 < W_A:
                    acc = acc * (HEAD ** -0.5 * LOG2E)
                out = acc.astype(BF)
                for bi in range(nb):
                    qkv_ref[bi, :, c0:c1] = out[bi * c:(bi + 1) * c, :]
            elif c0 < ATT_COLS + RWKV_COLS:
                cols = slice(c0 - ATT_COLS, c1 - ATT_COLS)
                prev = pltpu.roll(acc, 1, 0)
                for bi in range(nb):
                    prev = jnp.where(row_w[:, 0:c1 - c0] == bi * c,
                                     jnp.broadcast_to(carry_ref[bi, :, cols], acc.shape), prev)
                    carry_ref[bi, :, cols] = acc[(bi + 1) * c - 1:(bi + 1) * c, :]
                p_scr[nxt, :, cols] = acc + (prev - acc) * mu_ref[:, cols]
            else:
                g0 = c0 - ATT_COLS - RWKV_COLS
                out = jax.nn.sigmoid(acc + gbias_ref[:, g0:g0 + c1 - c0]).astype(BF)
                for bi in range(nb):
                    gate_ref[bi, :, g0:g0 + c1 - c0] = out[bi * c:(bi + 1) * c, :]
        return run, c1

    pieces = []
    col = 0
    while col < IN_COLS:
        run, col = proj_piece(col)
        pieces.append(run)
    pieces = iter(pieces)

    def fill(count=1):
        for _ in range(count):
            run = next(pieces, None)
            if run is not None:
                run()

    fill()

    ps = p_scr[cur]
    r = ps[:, 0:W_B]
    k = ps[:, W_B:2 * W_B]
    v = ps[:, 2 * W_B:3 * W_B]
    lora_in = ps[:, 3 * W_B:3 * W_B + DECAY_LORA + A_LORA]
    cg = ps[:, 3 * W_B + DECAY_LORA + A_LORA:]

    u = w0_ref[...] + _mm(jnp.tanh(lora_in), w2_ref[...])
    lw = -math.exp(-0.5) * jax.nn.sigmoid(u)
    a = jax.nn.sigmoid(a0_ref[...] + _mm(lora_in, a2_ref[...]))
    g = _mm(jax.nn.sigmoid(cg), g2_ref[...])
    fill()

    bd = bd_ref[...]
    kk = k * kk_ref[...]
    kk = kk * lax.rsqrt(jnp.maximum(_seg_sum(kk * kk, bd), 1e-24))
    fill()
    k = k * (1.0 + (a - 1.0) * ka_ref[...])
    aa = -kk
    bb = kk * a

    tri = tri_ref[...]
    cum = sum(jnp.dot(tri, part, preferred_element_type=F32) for part in _split2(lw))
    fill()
    last = jnp.concatenate(
        [jnp.broadcast_to(cum[(bi + 1) * c - 1:(bi + 1) * c, :], (c, W_B)) for bi in range(nb)],
        axis=0)
    inv = jnp.exp(-cum)
    tail = jnp.exp(last - cum)
    at = aa * jnp.exp(cum - lw)
    bt = bb * inv
    kt = k * inv
    rt = r * jnp.exp(cum)
    bh = bb * tail
    kh = k * tail
    w_end = jnp.exp(last)
    bonus = _seg_sum(r * k * rk_ref[...], bd) * v

    lane = lax.broadcasted_iota(jnp.int32, (c, slab), 1)
    lane2 = lax.broadcasted_iota(jnp.int32, (2 * c, slab), 1)
    row2 = lax.broadcasted_iota(jnp.int32, (2 * c, slab), 0)
    half = [lane < HEAD, lane >= HEAD]
    half2 = [lane2 < HEAD, lane2 >= HEAD]
    rl = row2 & (c - 1)
    cl = lane2 & (c - 1)
    mask4 = jnp.logical_or(rl > cl, jnp.logical_and(row2 >= c, rl == cl))
    zeros_top = jnp.zeros((c, slab), BF)

    chains = [(bi, pp, e) for bi in range(nb) for pp in range(N_HEAD // 2) for e in range(2)]
    n = range(len(chains))

    def blk(x, ch):
        bi, pp, _ = ch
        return x[bi * c:(bi + 1) * c, pp * slab:(pp + 1) * slab]

    ar = [jnp.concatenate([blk(at, ch), blk(rt, ch)], axis=0) for ch in chains]
    bk = [jnp.concatenate([blk(bt, ch), blk(kt, ch)], axis=0).astype(BF) for ch in chains]
    bkh = [jnp.concatenate([blk(bh, ch), blk(kh, ch)], axis=0).astype(BF) for ch in chains]
    vsw = [pltpu.roll(blk(v, ch), HEAD, 1) for ch in chains]
    vx = [jnp.where(half[1 - chains[i][2]], vsw[i], 0.0).astype(BF) for i in n]
    a4 = [jnp.where(mask4,
                    _mm_nt(jnp.where(half2[chains[i][2]], ar[i], 0.0), bk[i]), 0.0).astype(BF)
          for i in n]
    fill()
    x_o = [jnp.dot(a4[i][0:c, :], jnp.concatenate([zeros_top, vx[i]], axis=0),
                   preferred_element_type=F32) for i in n]
    ta = [jnp.where(half[chains[i][2]], blk(at, chains[i]), x_o[i]) for i in n]
    pw = [a4[i][0:c, 0:c] for i in n]
    for _ in range(int(math.log2(c)) - 1):
        fill()
        both = [jnp.dot(pw[i].astype(BF),
                        jnp.concatenate([ta[i].astype(BF), pw[i].astype(BF)], axis=1),
                        preferred_element_type=F32) for i in n]
        ta = [ta[i] + both[i][:, 0:slab] for i in n]
        pw = [both[i][:, slab:slab + c] for i in n]
    fill()
    ta = [ta[i] + _mm(pw[i], ta[i]) for i in n]
    fill()
    ry = [jnp.dot(a4[i][c:2 * c, :], jnp.concatenate([ta[i].astype(BF), vx[i]], axis=0),
                  preferred_element_type=F32) for i in n]
    gm = [jnp.concatenate(
        [ry[i] + jnp.where(half[chains[i][2]], blk(rt, chains[i]), 0.0), ta[i]], axis=0)
        for i in n]
    fill(2)
    s0 = [state_ref[ch[0], 2 * ch[1] + ch[2]] for ch in chains]
    yu = [_mm_nt(gm[i], s0[i]) + gm[i] for i in n]
    fill(2)
    for i in n:
        bi, pp, e = chains[i]
        uv = jnp.concatenate([yu[i][c:2 * c, :].astype(BF), vx[i]], axis=0)
        upd = lax.dot_general(uv, bkh[i], (((0,), (0,)), ((), ())), preferred_element_type=F32)
        keep = jnp.logical_and(half2[e], (row2 >= HEAD) if e == 0 else (row2 < HEAD))
        state_ref[bi, 2 * pp + e] = jnp.where(keep, s0[i] * blk(w_end, chains[i])[0:1, :] + upd, 0.0)
    for bi in range(nb):
        for pp in range(N_HEAD // 2):
            i0 = chains.index((bi, pp, 0))
            both = jnp.where(half[1], yu[i0][0:c, :], yu[i0 + 1][0:c, :])
            y_scr[bi * c:(bi + 1) * c, pp * slab:(pp + 1) * slab] = pltpu.roll(both, HEAD, 1)
    fill(2)

    y = y_scr[...]
    yc = y - _seg_sum(y, bd) * (1.0 / HEAD)
    fill()
    var = _seg_sum(yc * yc, bd) * (1.0 / HEAD)
    fill(IN_COLS // PROJ_PIECE)
    yn = yc * lax.rsqrt(var + GN_EPS) * lnw_ref[...] + lnb_ref[...]
    out = ((yn + bonus) * g).astype(BF)
    for bi in range(nb):
        o_ref[bi] = out[bi * c:(bi + 1) * c, :]

    xs = jnp.concatenate([x_ref[bi] for bi in range(nb)], axis=0)
    h_scr[...] = _rms(xs, gpre_ref[...]).astype(BF)


def _mix(x3, g_pre, w_bf, gate_bias, prm):
    b, s, _ = x3.shape
    nchunk = s // CHUNK
    const2 = lambda t: (0, 0)
    vec = pl.BlockSpec((1, W_B), const2)
    lora = DECAY_LORA + A_LORA
    rows = jnp.arange(b * CHUNK)
    tri = jnp.logical_and(rows[:, None] >= rows[None, :],
                          rows[:, None] // CHUNK == rows[None, :] // CHUNK).astype(BF)
    this = lambda t: (0, jnp.minimum(t, nchunk - 1), 0)
    ahead = lambda t: (0, jnp.minimum(t + 1, nchunk - 1), 0)
    qkv, gates, yb = pl.pallas_call(
        _mix_kernel,
        grid=(nchunk + 1,),
        in_specs=[
            pl.BlockSpec((b, CHUNK, D_MODEL), lambda t: (0, 0, 0)),
            pl.BlockSpec((b, CHUNK, D_MODEL), ahead),
            pl.BlockSpec((1, D_MODEL), const2),
            pl.BlockSpec((D_MODEL, IN_COLS), const2),
            pl.BlockSpec((1, GATE_COLS), const2),
            pl.BlockSpec((1, RWKV_COLS), const2),
            vec,
            pl.BlockSpec((lora, W_B), const2),
            vec,
            pl.BlockSpec((lora, W_B), const2),
            pl.BlockSpec((GATE_LORA, W_B), const2),
            vec, vec, vec, vec, vec,
            pl.BlockSpec((SEG_W, SEG_W), const2),
            pl.BlockSpec((b * CHUNK, b * CHUNK), const2),
        ],
        out_specs=[
            pl.BlockSpec((b, CHUNK, ATT_COLS), this),
            pl.BlockSpec((b, CHUNK, GATE_COLS), this),
            pl.BlockSpec((b, CHUNK, W_B), lambda t: (0, jnp.maximum(t - 1, 0), 0)),
        ],
        out_shape=[
            jax.ShapeDtypeStruct((b, s, ATT_COLS), BF),
            jax.ShapeDtypeStruct((b, s, GATE_COLS), BF),
            jax.ShapeDtypeStruct((b, s, W_B), BF),
        ],
        scratch_shapes=[pltpu.VMEM((b * CHUNK, D_MODEL), BF),
                        pltpu.VMEM((2, b * CHUNK, RWKV_COLS), F32),
                        pltpu.VMEM((b, 1, RWKV_COLS), F32),
                        pltpu.VMEM((b, N_HEAD, 2 * HEAD, 2 * HEAD), F32),
                        pltpu.VMEM((b * CHUNK, W_B), F32)],
        compiler_params=pltpu.CompilerParams(
            dimension_semantics=("arbitrary",), vmem_limit_bytes=VMEM_LIMIT),
        name="inproj_rwkv7",
    )(x3, x3, g_pre, w_bf, gate_bias, prm["mu"], prm["w0"], prm["w2"], prm["a0"], prm["a2"],
      prm["g2"], prm["k_k"], prm["k_a"], prm["r_k"], prm["ln_w"], prm["ln_b"], prm["bd"], tri)
    n = b * s
    return qkv.reshape(n, ATT_COLS), gates.reshape(n, GATE_COLS), yb.reshape(n, W_B)


def _attn_kernel(q_ref, kp_ref, kc_ref, vp_ref, vc_ref, base_ref, o_ref, kwin, vwin, bias_scr):
    i = pl.program_id(1)

    @pl.when(jnp.logical_and(pl.program_id(0) == 0, i == 0))
    def _():
        qi = lax.broadcasted_iota(jnp.int32, (ATT_Q_SUB, ATT_K_WIN), 0)
        kj = lax.broadcasted_iota(jnp.int32, (ATT_Q_SUB, ATT_K_WIN), 1)
        dchunk = (BAND_PREV + qi // CHUNK) - kj // CHUNK
        band = jnp.logical_and(dchunk >= 0, dchunk <= BAND_PREV)
        for h in range(N_HEAD):
            rows = jnp.broadcast_to(base_ref[h:h + 1, :], (ATT_Q_SUB, ATT_BASE))
            toep = pltpu.roll(rows, 0, 1, stride=1, stride_axis=0)
            bias_scr[h] = jnp.where(band, toep[:, 0:ATT_K_WIN] * LOG2E, NEG_INF)

    kwin[0:ROW_TILE, :] = kp_ref[...]
    kwin[ROW_TILE:2 * ROW_TILE, :] = kc_ref[...]
    vwin[0:ROW_TILE, :] = vp_ref[...]
    vwin[ROW_TILE:2 * ROW_TILE, :] = vc_ref[...]
    lane = lax.broadcasted_iota(jnp.int32, (ATT_Q_SUB, 2 * HEAD), 1)
    col = lax.broadcasted_iota(jnp.int32, (ATT_Q_SUB, ATT_K_WIN), 1)

    def block(first):
        for j in range(ROW_TILE // ATT_Q_SUB):
            r0 = j * ATT_Q_SUB
            hs = range(N_HEAD)
            cs = [(h // 2) * 2 * HEAD for h in hs]
            ss = []
            for h in hs:
                q2 = q_ref[r0:r0 + ATT_Q_SUB, cs[h]:cs[h] + 2 * HEAD]
                in_head = (lane >= HEAD) if h % 2 else (lane < HEAD)
                qm = jnp.where(in_head, q2, jnp.zeros_like(q2))
                ss.append(lax.dot_general(qm, kwin[r0:r0 + ATT_K_WIN, cs[h]:cs[h] + 2 * HEAD],
                                          (((1,), (1,)), ((), ())), preferred_element_type=F32))
            exs, ls = [], []
            for h in hs:
                s = ss[h] + bias_scr[h]
                if first:
                    s = jnp.where(col >= ROW_TILE - r0, s, NEG_INF)
                m = jnp.max(s, axis=-1, keepdims=True)
                ex = jnp.exp2(s - m)
                ls.append(jnp.sum(ex, axis=-1, keepdims=True))
                exs.append(ex.astype(BF))
            outs = [jnp.dot(exs[h], vwin[r0:r0 + ATT_K_WIN, cs[h]:cs[h] + 2 * HEAD],
                            preferred_element_type=F32) / ls[h] for h in hs]
            for p in range(N_HEAD // 2):
                o = jnp.where(lane < HEAD, outs[2 * p], outs[2 * p + 1])
                o_ref[r0:r0 + ATT_Q_SUB, cs[2 * p]:cs[2 * p] + 2 * HEAD] = o.astype(BF)

    pl.when(i == 0)(lambda: block(True))
    pl.when(i > 0)(lambda: block(False))


def _attn_bias_base(rel_bias):
    pos = jnp.arange(ATT_BASE)
    d = jnp.where(pos < ATT_K_WIN, pos, pos - ATT_BASE)
    idx = jnp.clip(BAND_PREV * CHUNK - d, -(CHUNK - 1), REL_CLIP) + (CHUNK - 1)
    return rel_bias.astype(F32)[:, idx]


def _attention(qkv, bias_base, b, s):
    n = b * s
    nblk = s // ROW_TILE
    blk = (ROW_TILE, W_A)
    return pl.pallas_call(
        _attn_kernel,
        grid=(b, nblk),
        in_specs=[
            pl.BlockSpec(blk, lambda bi, i: (bi * nblk + i, 0)),
            pl.BlockSpec(blk, lambda bi, i: (bi * nblk + jnp.maximum(i - 1, 0), 1)),
            pl.BlockSpec(blk, lambda bi, i: (bi * nblk + i, 1)),
            pl.BlockSpec(blk, lambda bi, i: (bi * nblk + jnp.maximum(i - 1, 0), 2)),
            pl.BlockSpec(blk, lambda bi, i: (bi * nblk + i, 2)),
            pl.BlockSpec((N_HEAD, ATT_BASE), lambda bi, i: (0, 0)),
        ],
        out_specs=pl.BlockSpec(blk, lambda bi, i: (bi * nblk + i, 0)),
        out_shape=jax.ShapeDtypeStruct((n, W_A), BF),
        scratch_shapes=[pltpu.VMEM((2 * ROW_TILE, W_A), BF),
                        pltpu.VMEM((2 * ROW_TILE, W_A), BF),
                        pltpu.VMEM((N_HEAD, ATT_Q_SUB, ATT_K_WIN), F32)],
        compiler_params=pltpu.CompilerParams(
            dimension_semantics=("arbitrary", "arbitrary"), vmem_limit_bytes=VMEM_LIMIT),
        name="band_attn",
    )(qkv, qkv, qkv, qkv, qkv, bias_base)


FF_TILE = 1024


def _tail_kernel(ya_ref, yb_ref, gate_ref, x_ref, pa_ref, pb_ref, wo_ref, gmix_ref,
                 g1_ref, wu_ref, wd_ref, g2_ref, o_ref):
    ma = jnp.dot(ya_ref[...], pa_ref[...], preferred_element_type=F32)
    mb = jnp.dot(yb_ref[...], pb_ref[...], preferred_element_type=F32)
    merged = (gate_ref[:, 0:D_MODEL].astype(F32) * ma
              + gate_ref[:, D_MODEL:2 * D_MODEL].astype(F32) * mb)
    z = jnp.dot(merged.astype(BF), wo_ref[...], preferred_element_type=F32)
    x = x_ref[...] + _rms(z, gmix_ref[...])

    hf = _rms(x, g1_ref[...]).astype(BF)
    acc = jnp.zeros(x.shape, F32)
    for c in range(0, D_FF, FF_TILE):
        u = jnp.dot(hf, wu_ref[:, c:c + FF_TILE], preferred_element_type=F32)
        u = jnp.maximum(u, 0.0)
        u = (u * u).astype(BF)
        acc = acc + jnp.dot(u, wd_ref[c:c + FF_TILE, :], preferred_element_type=F32)
    o_ref[...] = x + _rms(acc, g2_ref[...])


def _tail(ya, yb, gates, x2, pa, pb, wo, gmix, g1, wu, wd, g2):
    n = x2.shape[0]
    row = lambda i: (i, 0)

    def resident(shape):
        return pl.BlockSpec(shape, lambda i: (0, 0), pipeline_mode=pl.Buffered(1))

    return pl.pallas_call(
        _tail_kernel,
        grid=(n // ROW_TILE,),
        in_specs=[
            pl.BlockSpec((ROW_TILE, W_A), row),
            pl.BlockSpec((ROW_TILE, W_B), row),
            pl.BlockSpec((ROW_TILE, GATE_COLS), row),
            pl.BlockSpec((ROW_TILE, D_MODEL), row),
            resident((W_A, D_MODEL)),
            resident((W_B, D_MODEL)),
            resident((D_MODEL, D_MODEL)),
            resident((1, D_MODEL)),
            resident((1, D_MODEL)),
            resident((D_MODEL, D_FF)),
            resident((D_FF, D_MODEL)),
            resident((1, D_MODEL)),
        ],
        out_specs=pl.BlockSpec((ROW_TILE, D_MODEL), row),
        out_shape=jax.ShapeDtypeStruct((n, D_MODEL), F32),
        compiler_params=pltpu.CompilerParams(
            dimension_semantics=("arbitrary",), vmem_limit_bytes=VMEM_LIMIT),
        name="merge_ffn",
    )(ya, yb, gates, x2, pa, pb, wo, gmix, g1, wu, wd, g2)


def _layer(x2, b, s, lp):
    row = lambda a: a.reshape(1, -1).astype(F32)
    zeros_lora = jnp.zeros((DECAY_LORA, W_B), BF)
    blk = jnp.arange(SEG_W) // HEAD
    prm = {
        "mu": row(lp["shift_mu"]),
        "w0": row(lp["w0"]),
        "w2": jnp.concatenate([lp["w2"].astype(BF), zeros_lora], axis=0),
        "a0": row(lp["a0"]),
        "a2": jnp.concatenate([zeros_lora, lp["a2"].astype(BF)], axis=0),
        "g2": lp["g2"].astype(BF),
        "k_k": row(lp["k_k"]),
        "k_a": row(lp["k_a"]),
        "r_k": row(lp["r_k"]),
        "ln_w": row(lp["ln_x_w"]),
        "ln_b": row(lp["ln_x_b"]),
        "bd": (blk[:, None] == blk[None, :]).astype(BF),
    }
    qkv, gates, yb = _mix(x2.reshape(b, s, D_MODEL), row(lp["pre_mix_g"]),
                          lp["w_in"].astype(BF), row(lp["gate_bias"]), prm)
    ya = _attention(qkv, _attn_bias_base(lp["rel_bias"]), b, s)

    return _tail(ya, yb, gates, x2, lp["proj_a"].astype(BF), lp["proj_b"].astype(BF),
                 lp["w_out"].astype(BF), row(lp["post_mix_g"]), row(lp["pre_ffn_g"]),
                 lp["w_up"].astype(BF), lp["w_down"].astype(BF), row(lp["post_ffn_g"]))


@jax.jit
def _forward(x, params):
    b, s, d = x.shape
    assert d == D_MODEL and s % ROW_TILE == 0
    x2 = x.reshape(b * s, d)
    depth = params["w_in"].shape[0]
    for l in range(depth):
        lp = {name: val[l] for name, val in params.items()}
        x2 = _layer(x2, b, s, lp)
    return x2.reshape(b, s, d)


def kernel(x, pre_mix_g, w_in, gate_bias, rel_bias, shift_mu, w0, w2, a0, a2, g2, k_k, k_a, r_k, ln_x_w, ln_x_b, proj_a, proj_b, w_out, post_mix_g, pre_ffn_g, w_up, w_down, post_ffn_g):
    params = dict(pre_mix_g=pre_mix_g, w_in=w_in, gate_bias=gate_bias, rel_bias=rel_bias,
                  shift_mu=shift_mu, w0=w0, w2=w2, a0=a0, a2=a2, g2=g2, k_k=k_k, k_a=k_a,
                  r_k=r_k, ln_x_w=ln_x_w, ln_x_b=ln_x_b, proj_a=proj_a, proj_b=proj_b,
                  w_out=w_out, post_mix_g=post_mix_g, pre_ffn_g=pre_ffn_g, w_up=w_up,
                  w_down=w_down, post_ffn_g=post_ffn_g)
    return _forward(x, params)
```

```python
import math

import jax
import jax.numpy as jnp
from jax import lax
from jax.experimental import pallas as pl
from jax.experimental.pallas import tpu as pltpu

BF = jnp.bfloat16
F32 = jnp.float32

D_MODEL = 1024
CHUNK = 64
BAND_PREV = 8
REL_CLIP = 256
W_A = D_MODEL // 2
HEAD = 64
N_HEAD = W_A // HEAD
W_B = D_MODEL // 2
DECAY_LORA = 64
A_LORA = 64
GATE_LORA = 128
D_FF = 4 * D_MODEL
ATT_COLS = 3 * W_A
RWKV_COLS = 3 * W_B + DECAY_LORA + A_LORA + GATE_LORA
GATE_COLS = 2 * D_MODEL
IN_COLS = ATT_COLS + RWKV_COLS + GATE_COLS
RMS_EPS = 1e-6
GN_EPS = HEAD * 1e-5
NEG_INF = -1e30
LOG2E = math.log2(math.e)

ROW_TILE = 512
ATT_Q_SUB = 128
ATT_K_WIN = ATT_Q_SUB + BAND_PREV * CHUNK
ATT_BASE = ATT_K_WIN + ATT_Q_SUB
SEG_W = 256
PROJ_PIECE = 256
VMEM_LIMIT = 56 * 1024 * 1024


def _mm(a, b):
    return jnp.dot(a.astype(BF), b.astype(BF), preferred_element_type=F32)


def _mm_nt(a, b):
    return lax.dot_general(a.astype(BF), b.astype(BF), (((1,), (1,)), ((), ())),
                           preferred_element_type=F32)


def _rms(x, g):
    ms = jnp.mean(x * x, axis=-1, keepdims=True)
    return x * lax.rsqrt(ms + RMS_EPS) * g


def _split2(x):
    hi = x.astype(BF)
    lo = (x - hi.astype(F32)).astype(BF)
    return hi, lo


def _seg_sum(x, bd):
    w = bd.shape[0]
    xb = x.astype(BF)
    return jnp.concatenate(
        [jnp.dot(xb[:, j:j + w], bd, preferred_element_type=F32)
         for j in range(0, x.shape[1], w)], axis=1)


def _mix_kernel(x0_ref, x_ref, gpre_ref, w_ref, gbias_ref, mu_ref, w0_ref, w2_ref, a0_ref, a2_ref,
                g2_ref, kk_ref, ka_ref, rk_ref, lnw_ref, lnb_ref, bd_ref, tri_ref,
                qkv_ref, gate_ref, o_ref, h_scr, p_scr, carry_ref, state_ref, y_scr):
    t = pl.program_id(0)
    nb = x_ref.shape[0]
    c = CHUNK
    slab = 2 * HEAD
    cur = (t + 1) % 2
    nxt = t % 2

    @pl.when(t == 0)
    def _():
        carry_ref[...] = jnp.zeros_like(carry_ref)
        state_ref[...] = jnp.zeros_like(state_ref)
        p_scr[1] = jnp.zeros(p_scr.shape[1:], F32)
        x0 = jnp.concatenate([x0_ref[bi] for bi in range(nb)], axis=0)
        h_scr[...] = _rms(x0, gpre_ref[...]).astype(BF)

    h = h_scr[...]
    row_w = lax.broadcasted_iota(jnp.int32, (nb * c, PROJ_PIECE), 0)

    def proj_piece(c0):
        c1 = min(c0 + PROJ_PIECE, ATT_COLS if c0 < ATT_COLS else
                 ATT_COLS + RWKV_COLS if c0 < ATT_COLS + RWKV_COLS else IN_COLS)

        def run():
            acc = jnp.dot(h, w_ref[:, c0:c1], preferred_element_type=F32)
            if c0 < ATT_COLS:
                if c0 < W_A:
                    acc = acc * (HEAD ** -0.5 * LOG2E)
                out = acc.astype(BF)
                for bi in range(nb):
                    qkv_ref[bi, :, c0:c1] = out[bi * c:(bi + 1) * c, :]
            elif c0 < ATT_COLS + RWKV_COLS:
                cols = slice(c0 - ATT_COLS, c1 - ATT_COLS)
                prev = pltpu.roll(acc, 1, 0)
                for bi in range(nb):
                    prev = jnp.where(row_w[:, 0:c1 - c0] == bi * c,
                                     jnp.broadcast_to(carry_ref[bi, :, cols], acc.shape), prev)
                    carry_ref[bi, :, cols] = acc[(bi + 1) * c - 1:(bi + 1) * c, :]
                p_scr[nxt, :, cols] = acc + (prev - acc) * mu_ref[:, cols]
            else:
                g0 = c0 - ATT_COLS - RWKV_COLS
                out = jax.nn.sigmoid(acc + gbias_ref[:, g0:g0 + c1 - c0]).astype(BF)
                for bi in range(nb):
                    gate_ref[bi, :, g0:g0 + c1 - c0] = out[bi * c:(bi + 1) * c, :]
        return run, c1

    pieces = []
    col = 0
    while col < IN_COLS:
        run, col = proj_piece(col)
        pieces.append(run)
    pieces = iter(pieces)

    def fill(count=1):
        for _ in range(count):
            run = next(pieces, None)
            if run is not None:
                run()

    fill()

    ps = p_scr[cur]
    r = ps[:, 0:W_B]
    k = ps[:, W_B:2 * W_B]
    v = ps[:, 2 * W_B:3 * W_B]
    lora_in = ps[:, 3 * W_B:3 * W_B + DECAY_LORA + A_LORA]
    cg = ps[:, 3 * W_B + DECAY_LORA + A_LORA:]

    u = w0_ref[...] + _mm(jnp.tanh(lora_in), w2_ref[...])
    lw = -(math.exp(-0.5) * LOG2E) * jax.nn.sigmoid(u)
    a = jax.nn.sigmoid(a0_ref[...] + _mm(lora_in, a2_ref[...]))
    g = _mm(jax.nn.sigmoid(cg), g2_ref[...])
    fill()

    bd = bd_ref[...]
    kk = k * kk_ref[...]
    kk = kk * lax.rsqrt(jnp.maximum(_seg_sum(kk * kk, bd), 1e-24))
    fill()
    k = k * (1.0 + (a - 1.0) * ka_ref[...])
    aa = -kk
    bb = kk * a

    tri = tri_ref[...]
    cum = sum(jnp.dot(tri, part, preferred_element_type=F32) for part in _split2(lw))
    fill()
    last = jnp.concatenate(
        [jnp.broadcast_to(cum[(bi + 1) * c - 1:(bi + 1) * c, :], (c, W_B)) for bi in range(nb)],
        axis=0)
    inv = jnp.exp2(-cum)
    w_end = jnp.exp2(last)
    tail = w_end * inv
    at = aa * jnp.exp2(cum - lw)
    bt = bb * inv
    kt = k * inv
    rt = r * jnp.exp2(cum)
    bh = bb * tail
    kh = k * tail
    bonus = _seg_sum(r * k * rk_ref[...], bd) * v

    lane = lax.broadcasted_iota(jnp.int32, (c, slab), 1)
    lane2 = lax.broadcasted_iota(jnp.int32, (2 * c, slab), 1)
    row2 = lax.broadcasted_iota(jnp.int32, (2 * c, slab), 0)
    half = [lane < HEAD, lane >= HEAD]
    half2 = [lane2 < HEAD, lane2 >= HEAD]
    rl = row2 & (c - 1)
    cl = lane2 & (c - 1)
    mask4 = jnp.logical_or(rl > cl, jnp.logical_and(row2 >= c, rl == cl))
    zeros_top = jnp.zeros((c, slab), BF)

    chains = [(bi, pp, e) for bi in range(nb) for pp in range(N_HEAD // 2) for e in range(2)]
    n = range(len(chains))

    def blk(x, ch):
        bi, pp, _ = ch
        return x[bi * c:(bi + 1) * c, pp * slab:(pp + 1) * slab]

    ar = [jnp.concatenate([blk(at, ch), blk(rt, ch)], axis=0) for ch in chains]
    bk = [jnp.concatenate([blk(bt, ch), blk(kt, ch)], axis=0).astype(BF) for ch in chains]
    bkh = [jnp.concatenate([blk(bh, ch), blk(kh, ch)], axis=0).astype(BF) for ch in chains]
    vsw = [pltpu.roll(blk(v, ch), HEAD, 1) for ch in chains]
    vx = [jnp.where(half[1 - chains[i][2]], vsw[i], 0.0).astype(BF) for i in n]
    a4 = [jnp.where(mask4,
                    _mm_nt(jnp.where(half2[chains[i][2]], ar[i], 0.0), bk[i]), 0.0).astype(BF)
          for i in n]
    fill()
    x_o = [jnp.dot(a4[i][0:c, :], jnp.concatenate([zeros_top, vx[i]], axis=0),
                   preferred_element_type=F32) for i in n]
    ta = [jnp.where(half[chains[i][2]], blk(at, chains[i]), x_o[i]) for i in n]
    pw = [a4[i][0:c, 0:c] for i in n]
    for _ in range(int(math.log2(c)) - 1):
        fill()
        both = [jnp.dot(pw[i].astype(BF),
                        jnp.concatenate([ta[i].astype(BF), pw[i].astype(BF)], axis=1),
                        preferred_element_type=F32) for i in n]
        ta = [ta[i] + both[i][:, 0:slab] for i in n]
        pw = [both[i][:, slab:slab + c] for i in n]
    fill()
    ta = [ta[i] + _mm(pw[i], ta[i]) for i in n]
    fill()
    ry = [jnp.dot(a4[i][c:2 * c, :], jnp.concatenate([ta[i].astype(BF), vx[i]], axis=0),
                  preferred_element_type=F32) for i in n]
    gm = [jnp.concatenate(
        [ry[i] + jnp.where(half[chains[i][2]], blk(rt, chains[i]), 0.0), ta[i]], axis=0)
        for i in n]
    fill(2)
    s0 = [state_ref[ch[0], 2 * ch[1] + ch[2]] for ch in chains]
    yu = [_mm_nt(gm[i], s0[i]) + gm[i] for i in n]
    fill(2)
    for i in n:
        bi, pp, e = chains[i]
        uv = jnp.concatenate([yu[i][c:2 * c, :].astype(BF), vx[i]], axis=0)
        upd = lax.dot_general(uv, bkh[i], (((0,), (0,)), ((), ())), preferred_element_type=F32)
        keep = jnp.logical_and(half2[e], (row2 >= HEAD) if e == 0 else (row2 < HEAD))
        state_ref[bi, 2 * pp + e] = jnp.where(keep, s0[i] * blk(w_end, chains[i])[0:1, :] + upd, 0.0)
    for bi in range(nb):
        for pp in range(N_HEAD // 2):
            i0 = chains.index((bi, pp, 0))
            both = jnp.where(half[1], yu[i0][0:c, :], yu[i0 + 1][0:c, :])
            y_scr[bi * c:(bi + 1) * c, pp * slab:(pp + 1) * slab] = pltpu.roll(both, HEAD, 1)
    fill(2)

    y = y_scr[...]
    yc = y - _seg_sum(y, bd) * (1.0 / HEAD)
    fill()
    var = _seg_sum(yc * yc, bd) * (1.0 / HEAD)
    fill(IN_COLS // PROJ_PIECE)
    yn = yc * lax.rsqrt(var + GN_EPS) * lnw_ref[...] + lnb_ref[...]
    out = ((yn + bonus) * g).astype(BF)
    for bi in range(nb):
        o_ref[bi] = out[bi * c:(bi + 1) * c, :]

    xs = jnp.concatenate([x_ref[bi] for bi in range(nb)], axis=0)
    h_scr[...] = _rms(xs, gpre_ref[...]).astype(BF)


def _mix(x3, g_pre, w_bf, gate_bias, prm):
    b, s, _ = x3.shape
    nchunk = s // CHUNK
    const2 = lambda t: (0, 0)
    vec = pl.BlockSpec((1, W_B), const2)
    lora = DECAY_LORA + A_LORA
    rows = jnp.arange(b * CHUNK)
    tri = jnp.logical_and(rows[:, None] >= rows[None, :],
                          rows[:, None] // CHUNK == rows[None, :] // CHUNK).astype(BF)
    this = lambda t: (0, jnp.minimum(t, nchunk - 1), 0)
    ahead = lambda t: (0, jnp.minimum(t + 1, nchunk - 1), 0)
    qkv, gates, yb = pl.pallas_call(
        _mix_kernel,
        grid=(nchunk + 1,),
        in_specs=[
            pl.BlockSpec((b, CHUNK, D_MODEL), lambda t: (0, 0, 0)),
            pl.BlockSpec((b, CHUNK, D_MODEL), ahead),
            pl.BlockSpec((1, D_MODEL), const2),
            pl.BlockSpec((D_MODEL, IN_COLS), const2),
            pl.BlockSpec((1, GATE_COLS), const2),
            pl.BlockSpec((1, RWKV_COLS), const2),
            vec,
            pl.BlockSpec((lora, W_B), const2),
            vec,
            pl.BlockSpec((lora, W_B), const2),
            pl.BlockSpec((GATE_LORA, W_B), const2),
            vec, vec, vec, vec, vec,
            pl.BlockSpec((SEG_W, SEG_W), const2),
            pl.BlockSpec((b * CHUNK, b * CHUNK), const2),
        ],
        out_specs=[
            pl.BlockSpec((b, CHUNK, ATT_COLS), this),
            pl.BlockSpec((b, CHUNK, GATE_COLS), this),
            pl.BlockSpec((b, CHUNK, W_B), lambda t: (0, jnp.maximum(t - 1, 0), 0)),
        ],
        out_shape=[
            jax.ShapeDtypeStruct((b, s, ATT_COLS), BF),
            jax.ShapeDtypeStruct((b, s, GATE_COLS), BF),
            jax.ShapeDtypeStruct((b, s, W_B), BF),
        ],
        scratch_shapes=[pltpu.VMEM((b * CHUNK, D_MODEL), BF),
                        pltpu.VMEM((2, b * CHUNK, RWKV_COLS), F32),
                        pltpu.VMEM((b, 1, RWKV_COLS), F32),
                        pltpu.VMEM((b, N_HEAD, 2 * HEAD, 2 * HEAD), F32),
                        pltpu.VMEM((b * CHUNK, W_B), F32)],
        compiler_params=pltpu.CompilerParams(
            dimension_semantics=("arbitrary",), vmem_limit_bytes=VMEM_LIMIT),
        name="inproj_rwkv7",
    )(x3, x3, g_pre, w_bf, gate_bias, prm["mu"], prm["w0"], prm["w2"], prm["a0"], prm["a2"],
      prm["g2"], prm["k_k"], prm["k_a"], prm["r_k"], prm["ln_w"], prm["ln_b"], prm["bd"], tri)
    n = b * s
    return qkv.reshape(n, ATT_COLS), gates.reshape(n, GATE_COLS), yb.reshape(n, W_B)


def _attn_kernel(q_ref, kp_ref, kc_ref, vp_ref, vc_ref, base_ref, o_ref, kwin, vwin, bias_scr):
    i = pl.program_id(1)

    @pl.when(jnp.logical_and(pl.program_id(0) == 0, i == 0))
    def _():
        qi = lax.broadcasted_iota(jnp.int32, (ATT_Q_SUB, ATT_K_WIN), 0)
        kj = lax.broadcasted_iota(jnp.int32, (ATT_Q_SUB, ATT_K_WIN), 1)
        dchunk = (BAND_PREV + qi // CHUNK) - kj // CHUNK
        band = jnp.logical_and(dchunk >= 0, dchunk <= BAND_PREV)
        for h in range(N_HEAD):
            rows = jnp.broadcast_to(base_ref[h:h + 1, :], (ATT_Q_SUB, ATT_BASE))
            toep = pltpu.roll(rows, 0, 1, stride=1, stride_axis=0)
            bias_scr[h] = jnp.where(band, toep[:, 0:ATT_K_WIN] * LOG2E, NEG_INF)

    kwin[0:ROW_TILE, :] = kp_ref[...]
    kwin[ROW_TILE:2 * ROW_TILE, :] = kc_ref[...]
    vwin[0:ROW_TILE, :] = vp_ref[...]
    vwin[ROW_TILE:2 * ROW_TILE, :] = vc_ref[...]
    lane = lax.broadcasted_iota(jnp.int32, (ATT_Q_SUB, 2 * HEAD), 1)
    col = lax.broadcasted_iota(jnp.int32, (ATT_Q_SUB, ATT_K_WIN), 1)

    def block(first):
        for j in range(ROW_TILE // ATT_Q_SUB):
            r0 = j * ATT_Q_SUB
            hs = range(N_HEAD)
            cs = [(h // 2) * 2 * HEAD for h in hs]
            ss = []
            for h in hs:
                q2 = q_ref[r0:r0 + ATT_Q_SUB, cs[h]:cs[h] + 2 * HEAD]
                in_head = (lane >= HEAD) if h % 2 else (lane < HEAD)
                qm = jnp.where(in_head, q2, jnp.zeros_like(q2))
                ss.append(lax.dot_general(qm, kwin[r0:r0 + ATT_K_WIN, cs[h]:cs[h] + 2 * HEAD],
                                          (((1,), (1,)), ((), ())), preferred_element_type=F32))
            exs, ls = [], []
            for h in hs:
                s = ss[h] + bias_scr[h]
                if first:
                    s = jnp.where(col >= ROW_TILE - r0, s, NEG_INF)
                m = jnp.max(s, axis=-1, keepdims=True)
                ex = jnp.exp2(s - m)
                ls.append(jnp.sum(ex, axis=-1, keepdims=True))
                exs.append(ex.astype(BF))
            outs = [jnp.dot(exs[h], vwin[r0:r0 + ATT_K_WIN, cs[h]:cs[h] + 2 * HEAD],
                            preferred_element_type=F32) / ls[h] for h in hs]
            for p in range(N_HEAD // 2):
                o = jnp.where(lane < HEAD, outs[2 * p], outs[2 * p + 1])
                o_ref[r0:r0 + ATT_Q_SUB, cs[2 * p]:cs[2 * p] + 2 * HEAD] = o.astype(BF)

    pl.when(i == 0)(lambda: block(True))
    pl.when(i > 0)(lambda: block(False))


def _attn_bias_base(rel_bias):
    pos = jnp.arange(ATT_BASE)
    d = jnp.where(pos < ATT_K_WIN, pos, pos - ATT_BASE)
    idx = jnp.clip(BAND_PREV * CHUNK - d, -(CHUNK - 1), REL_CLIP) + (CHUNK - 1)
    return rel_bias.astype(F32)[:, idx]


def _attention(qkv, bias_base, b, s):
    n = b * s
    nblk = s // ROW_TILE
    blk = (ROW_TILE, W_A)
    return pl.pallas_call(
        _attn_kernel,
        grid=(b, nblk),
        in_specs=[
            pl.BlockSpec(blk, lambda bi, i: (bi * nblk + i, 0)),
            pl.BlockSpec(blk, lambda bi, i: (bi * nblk + jnp.maximum(i - 1, 0), 1)),
            pl.BlockSpec(blk, lambda bi, i: (bi * nblk + i, 1)),
            pl.BlockSpec(blk, lambda bi, i: (bi * nblk + jnp.maximum(i - 1, 0), 2)),
            pl.BlockSpec(blk, lambda bi, i: (bi * nblk + i, 2)),
            pl.BlockSpec((N_HEAD, ATT_BASE), lambda bi, i: (0, 0)),
        ],
        out_specs=pl.BlockSpec(blk, lambda bi, i: (bi * nblk + i, 0)),
        out_shape=jax.ShapeDtypeStruct((n, W_A), BF),
        scratch_shapes=[pltpu.VMEM((2 * ROW_TILE, W_A), BF),
                        pltpu.VMEM((2 * ROW_TILE, W_A), BF),
                        pltpu.VMEM((N_HEAD, ATT_Q_SUB, ATT_K_WIN), F32)],
        compiler_params=pltpu.CompilerParams(
            dimension_semantics=("arbitrary", "arbitrary"), vmem_limit_bytes=VMEM_LIMIT),
        name="band_attn",
    )(qkv, qkv, qkv, qkv, qkv, bias_base)


FF_TILE = 1024


def _tail_kernel(ya_ref, yb_ref, gate_ref, x_ref, pa_ref, pb_ref, wo_ref, gmix_ref,
                 g1_ref, wu_ref, wd_ref, g2_ref, o_ref):
    half = ROW_TILE // 2
    rows = [slice(r0, r0 + half) for r0 in range(0, ROW_TILE, half)]
    z = []
    for rs in rows:
        ma = jnp.dot(ya_ref[rs, :], pa_ref[...], preferred_element_type=F32)
        mb = jnp.dot(yb_ref[rs, :], pb_ref[...], preferred_element_type=F32)
        merged = (gate_ref[rs, 0:D_MODEL].astype(F32) * ma
                  + gate_ref[rs, D_MODEL:2 * D_MODEL].astype(F32) * mb)
        z.append(jnp.dot(merged.astype(BF), wo_ref[...], preferred_element_type=F32))
    x = [x_ref[rs, :] + _rms(z[i], gmix_ref[...]) for i, rs in enumerate(rows)]
    hf = [_rms(xi, g1_ref[...]).astype(BF) for xi in x]
    acc = [jnp.zeros(xi.shape, F32) for xi in x]
    for c in range(0, D_FF, FF_TILE):
        u = [jnp.dot(h, wu_ref[:, c:c + FF_TILE], preferred_element_type=F32) for h in hf]
        u = [jnp.maximum(ui, 0.0) for ui in u]
        u = [(ui * ui).astype(BF) for ui in u]
        acc = [a + jnp.dot(ui, wd_ref[c:c + FF_TILE, :], preferred_element_type=F32)
               for a, ui in zip(acc, u)]
    for i, rs in enumerate(rows):
        o_ref[rs, :] = x[i] + _rms(acc[i], g2_ref[...])


def _tail(ya, yb, gates, x2, pa, pb, wo, gmix, g1, wu, wd, g2):
    n = x2.shape[0]
    row = lambda i: (i, 0)

    def resident(shape):
        return pl.BlockSpec(shape, lambda i: (0, 0), pipeline_mode=pl.Buffered(1))

    return pl.pallas_call(
        _tail_kernel,
        grid=(n // ROW_TILE,),
        in_specs=[
            pl.BlockSpec((ROW_TILE, W_A), row),
            pl.BlockSpec((ROW_TILE, W_B), row),
            pl.BlockSpec((ROW_TILE, GATE_COLS), row),
            pl.BlockSpec((ROW_TILE, D_MODEL), row),
            resident((W_A, D_MODEL)),
            resident((W_B, D_MODEL)),
            resident((D_MODEL, D_MODEL)),
            resident((1, D_MODEL)),
            resident((1, D_MODEL)),
            resident((D_MODEL, D_FF)),
            resident((D_FF, D_MODEL)),
            resident((1, D_MODEL)),
        ],
        out_specs=pl.BlockSpec((ROW_TILE, D_MODEL), row),
        out_shape=jax.ShapeDtypeStruct((n, D_MODEL), F32),
        compiler_params=pltpu.CompilerParams(
            dimension_semantics=("arbitrary",), vmem_limit_bytes=VMEM_LIMIT),
        name="merge_ffn",
    )(ya, yb, gates, x2, pa, pb, wo, gmix, g1, wu, wd, g2)


def _layer(x2, b, s, lp):
    row = lambda a: a.reshape(1, -1).astype(F32)
    zeros_lora = jnp.zeros((DECAY_LORA, W_B), BF)
    blk = jnp.arange(SEG_W) // HEAD
    prm = {
        "mu": row(lp["shift_mu"]),
        "w0": row(lp["w0"]),
        "w2": jnp.concatenate([lp["w2"].astype(BF), zeros_lora], axis=0),
        "a0": row(lp["a0"]),
        "a2": jnp.concatenate([zeros_lora, lp["a2"].astype(BF)], axis=0),
        "g2": lp["g2"].astype(BF),
        "k_k": row(lp["k_k"]),
        "k_a": row(lp["k_a"]),
        "r_k": row(lp["r_k"]),
        "ln_w": row(lp["ln_x_w"]),
        "ln_b": row(lp["ln_x_b"]),
        "bd": (blk[:, None] == blk[None, :]).astype(BF),
    }
    qkv, gates, yb = _mix(x2.reshape(b, s, D_MODEL), row(lp["pre_mix_g"]),
                          lp["w_in"].astype(BF), row(lp["gate_bias"]), prm)
    ya = _attention(qkv, _attn_bias_base(lp["rel_bias"]), b, s)

    return _tail(ya, yb, gates, x2, lp["proj_a"].astype(BF), lp["proj_b"].astype(BF),
                 lp["w_out"].astype(BF), row(lp["post_mix_g"]), row(lp["pre_ffn_g"]),
                 lp["w_up"].astype(BF), lp["w_down"].astype(BF), row(lp["post_ffn_g"]))


@jax.jit
def _forward(x, params):
    b, s, d = x.shape
    assert d == D_MODEL and s % ROW_TILE == 0
    x2 = x.reshape(b * s, d)
    depth = params["w_in"].shape[0]
    for l in range(depth):
        lp = {name: val[l] for name, val in params.items()}
        x2 = _layer(x2, b, s, lp)
    return x2.reshape(b, s, d)


def kernel(x, pre_mix_g, w_in, gate_bias, rel_bias, shift_mu, w0, w2, a0, a2, g2, k_k, k_a, r_k, ln_x_w, ln_x_b, proj_a, proj_b, w_out, post_mix_g, pre_ffn_g, w_up, w_down, post_ffn_g):
    params = dict(pre_mix_g=pre_mix_g, w_in=w_in, gate_bias=gate_bias, rel_bias=rel_bias,
                  shift_mu=shift_mu, w0=w0, w2=w2, a0=a0, a2=a2, g2=g2, k_k=k_k, k_a=k_a,
                  r_k=r_k, ln_x_w=ln_x_w, ln_x_b=ln_x_b, proj_a=proj_a, proj_b=proj_b,
                  w_out=w_out, post_mix_g=post_mix_g, pre_ffn_g=pre_ffn_g, w_up=w_up,
                  w_down=w_down, post_ffn_g=post_ffn_g)
    return _forward(x, params)
```

```python
import math

import jax
import jax.numpy as jnp
from jax import lax
from jax.experimental import pallas as pl
from jax.experimental.pallas import tpu as pltpu

BF = jnp.bfloat16
F32 = jnp.float32

D_MODEL = 1024
CHUNK = 64
BAND_PREV = 8
REL_CLIP = 256
W_A = D_MODEL // 2
HEAD = 64
N_HEAD = W_A // HEAD
W_B = D_MODEL // 2
DECAY_LORA = 64
A_LORA = 64
GATE_LORA = 128
D_FF = 4 * D_MODEL
ATT_COLS = 3 * W_A
RWKV_COLS = 3 * W_B + DECAY_LORA + A_LORA + GATE_LORA
GATE_COLS = 2 * D_MODEL
IN_COLS = ATT_COLS + RWKV_COLS + GATE_COLS
RMS_EPS = 1e-6
GN_EPS = HEAD * 1e-5
NEG_INF = -1e30
LOG2E = math.log2(math.e)

ROW_TILE = 512
ATT_Q_SUB = 128
ATT_K_WIN = ATT_Q_SUB + BAND_PREV * CHUNK
ATT_BASE = ATT_K_WIN + ATT_Q_SUB
SEG_W = 256
PROJ_PIECE = 256
VMEM_LIMIT = 56 * 1024 * 1024


def _mm(a, b):
    return jnp.dot(a.astype(BF), b.astype(BF), preferred_element_type=F32)


def _mm_nt(a, b):
    return lax.dot_general(a.astype(BF), b.astype(BF), (((1,), (1,)), ((), ())),
                           preferred_element_type=F32)


def _rms(x, g):
    ms = jnp.mean(x * x, axis=-1, keepdims=True)
    return x * lax.rsqrt(ms + RMS_EPS) * g


def _split2(x):
    hi = x.astype(BF)
    lo = (x - hi.astype(F32)).astype(BF)
    return hi, lo


def _seg_sum(x, bd):
    w = bd.shape[0]
    xb = x.astype(BF)
    return jnp.concatenate(
        [jnp.dot(xb[:, j:j + w], bd, preferred_element_type=F32)
         for j in range(0, x.shape[1], w)], axis=1)


def _mix_kernel(x0_ref, x_ref, gpre_ref, w_ref, gbias_ref, mu_ref, w0_ref, w2_ref, a0_ref, a2_ref,
                g2_ref, kk_ref, ka_ref, rk_ref, lnw_ref, lnb_ref, bd_ref, tri_ref,
                qkv_ref, gate_ref, o_ref, h_scr, p_scr, carry_ref, state_ref, y_scr):
    t = pl.program_id(0)
    nb = x_ref.shape[0]
    c = CHUNK
    slab = 2 * HEAD
    cur = (t + 1) % 2
    nxt = t % 2

    @pl.when(t == 0)
    def _():
        carry_ref[...] = jnp.zeros_like(carry_ref)
        state_ref[...] = jnp.zeros_like(state_ref)
        p_scr[1] = jnp.zeros(p_scr.shape[1:], F32)
        x0 = jnp.concatenate([x0_ref[bi] for bi in range(nb)], axis=0)
        h_scr[...] = _rms(x0, gpre_ref[...]).astype(BF)

    h = h_scr[...]
    row_w = lax.broadcasted_iota(jnp.int32, (nb * c, PROJ_PIECE), 0)

    def proj_piece(c0):
        c1 = min(c0 + PROJ_PIECE, ATT_COLS if c0 < ATT_COLS else
                 ATT_COLS + RWKV_COLS if c0 < ATT_COLS + RWKV_COLS else IN_COLS)

        def run():
            acc = jnp.dot(h, w_ref[:, c0:c1], preferred_element_type=F32)
            if c0 < ATT_COLS:
                if c0 < W_A:
                    acc = acc * (HEAD ** -0.5 * LOG2E)
                out = acc.astype(BF)
                for bi in range(nb):
                    qkv_ref[bi, :, c0:c1] = out[bi * c:(bi + 1) * c, :]
            elif c0 < ATT_COLS + RWKV_COLS:
                cols = slice(c0 - ATT_COLS, c1 - ATT_COLS)
                prev = pltpu.roll(acc, 1, 0)
                for bi in range(nb):
                    prev = jnp.where(row_w[:, 0:c1 - c0] == bi * c,
                                     jnp.broadcast_to(carry_ref[bi, :, cols], acc.shape), prev)
                    carry_ref[bi, :, cols] = acc[(bi + 1) * c - 1:(bi + 1) * c, :]
                p_scr[nxt, :, cols] = acc + (prev - acc) * mu_ref[:, cols]
            else:
                g0 = c0 - ATT_COLS - RWKV_COLS
                out = jax.nn.sigmoid(acc + gbias_ref[:, g0:g0 + c1 - c0]).astype(BF)
                for bi in range(nb):
                    gate_ref[bi, :, g0:g0 + c1 - c0] = out[bi * c:(bi + 1) * c, :]
        return run, c1

    pieces = []
    col = 0
    while col < IN_COLS:
        run, col = proj_piece(col)
        pieces.append(run)
    pieces = iter(pieces)

    def fill(count=1):
        for _ in range(count):
            run = next(pieces, None)
            if run is not None:
                run()

    fill()

    ps = p_scr[cur]
    r = ps[:, 0:W_B]
    k = ps[:, W_B:2 * W_B]
    v = ps[:, 2 * W_B:3 * W_B]
    lora_in = ps[:, 3 * W_B:3 * W_B + DECAY_LORA + A_LORA]
    cg = ps[:, 3 * W_B + DECAY_LORA + A_LORA:]

    u = w0_ref[...] + _mm(jnp.tanh(lora_in), w2_ref[...])
    lw = -(math.exp(-0.5) * LOG2E) * jax.nn.sigmoid(u)
    a = jax.nn.sigmoid(a0_ref[...] + _mm(lora_in, a2_ref[...]))
    g = _mm(jax.nn.sigmoid(cg), g2_ref[...])
    fill()

    bd = bd_ref[...]
    kk = k * kk_ref[...]
    kk = kk * lax.rsqrt(jnp.maximum(_seg_sum(kk * kk, bd), 1e-24))
    fill()
    k = k * (1.0 + (a - 1.0) * ka_ref[...])
    aa = -kk
    bb = kk * a

    tri = tri_ref[...]
    cum = sum(jnp.dot(tri, part, preferred_element_type=F32) for part in _split2(lw))
    fill()
    last = jnp.concatenate(
        [jnp.broadcast_to(cum[(bi + 1) * c - 1:(bi + 1) * c, :], (c, W_B)) for bi in range(nb)],
        axis=0)
    inv = jnp.exp2(-cum)
    w_end = jnp.exp2(last)
    tail = w_end * inv
    at = aa * jnp.exp2(cum - lw)
    bt = bb * inv
    kt = k * inv
    rt = r * jnp.exp2(cum)
    bh = bb * tail
    kh = k * tail
    bonus = _seg_sum(r * k * rk_ref[...], bd) * v

    lane = lax.broadcasted_iota(jnp.int32, (c, slab), 1)
    lane2 = lax.broadcasted_iota(jnp.int32, (2 * c, slab), 1)
    row2 = lax.broadcasted_iota(jnp.int32, (2 * c, slab), 0)
    half = [lane < HEAD, lane >= HEAD]
    half2 = [lane2 < HEAD, lane2 >= HEAD]
    rl = row2 & (c - 1)
    cl = lane2 & (c - 1)
    mask4 = jnp.logical_or(rl > cl, jnp.logical_and(row2 >= c, rl == cl))

    chains = [(bi, pp, e) for bi in range(nb) for pp in range(N_HEAD // 2) for e in range(2)]
    n = range(len(chains))

    def blk(x, ch):
        bi, pp, _ = ch
        return x[bi * c:(bi + 1) * c, pp * slab:(pp + 1) * slab]

    ar = [jnp.concatenate([blk(at, ch), blk(rt, ch)], axis=0) for ch in chains]
    bk = [jnp.concatenate([blk(bt, ch), blk(kt, ch)], axis=0).astype(BF) for ch in chains]
    bkh = [jnp.concatenate([blk(bh, ch), blk(kh, ch)], axis=0).astype(BF) for ch in chains]
    vsw = [pltpu.roll(blk(v, ch), HEAD, 1) for ch in chains]
    vx = [jnp.where(half[1 - chains[i][2]], vsw[i], 0.0).astype(BF) for i in n]
    a4 = [jnp.where(mask4,
                    _mm_nt(jnp.where(half2[chains[i][2]], ar[i], 0.0), bk[i]), 0.0).astype(BF)
          for i in n]
    ta = [jnp.where(half[chains[i][2]], blk(at, chains[i]), 0.0) for i in n]
    pw = [a4[i][0:c, :] for i in n]
    zeros_half = jnp.zeros((c, slab), BF)
    n_step = int(math.log2(c))
    for step in range(n_step):
        fill()
        if step + 1 < n_step:
            rhs = [jnp.concatenate(
                [jnp.concatenate([ta[i].astype(BF), pw[i]], axis=1),
                 jnp.concatenate([vx[i], zeros_half], axis=1)], axis=0) for i in n]
        else:
            rhs = [jnp.concatenate([ta[i].astype(BF), vx[i]], axis=0) for i in n]
        both = [jnp.dot(pw[i], rhs[i], preferred_element_type=F32) for i in n]
        ta = [ta[i] + both[i][:, 0:slab] for i in n]
        if step + 1 < n_step:
            pw = [both[i][:, slab:2 * slab].astype(BF) for i in n]
    gm = [jnp.concatenate(
        [jnp.where(half[chains[i][2]], blk(rt, chains[i]), 0.0), ta[i]], axis=0) for i in n]
    fill(2)
    s0 = [state_ref[ch[0], 2 * ch[1] + ch[2]] for ch in chains]
    ys = [_mm_nt(gm[i], s0[i]) for i in n]
    uv = [jnp.concatenate([(ys[i][c:2 * c, :] + ta[i]).astype(BF), vx[i]], axis=0) for i in n]
    fill(2)
    yo = [ys[i][0:c, :] + jnp.dot(a4[i][c:2 * c, :], uv[i], preferred_element_type=F32)
          for i in n]
    for i in n:
        bi, pp, e = chains[i]
        upd = lax.dot_general(uv[i], bkh[i], (((0,), (0,)), ((), ())),
                              preferred_element_type=F32)
        keep = jnp.logical_and(half2[e], (row2 >= HEAD) if e == 0 else (row2 < HEAD))
        state_ref[bi, 2 * pp + e] = jnp.where(keep, s0[i] * blk(w_end, chains[i])[0:1, :] + upd, 0.0)
    for bi in range(nb):
        for pp in range(N_HEAD // 2):
            i0 = chains.index((bi, pp, 0))
            both = jnp.where(half[1], yo[i0], yo[i0 + 1])
            y_scr[bi * c:(bi + 1) * c, pp * slab:(pp + 1) * slab] = pltpu.roll(both, HEAD, 1)
    fill(2)

    y = y_scr[...]
    yc = y - _seg_sum(y, bd) * (1.0 / HEAD)
    fill()
    var = _seg_sum(yc * yc, bd) * (1.0 / HEAD)
    fill(IN_COLS // PROJ_PIECE)
    yn = yc * lax.rsqrt(var + GN_EPS) * lnw_ref[...] + lnb_ref[...]
    out = ((yn + bonus) * g).astype(BF)
    for bi in range(nb):
        o_ref[bi] = out[bi * c:(bi + 1) * c, :]

    xs = jnp.concatenate([x_ref[bi] for bi in range(nb)], axis=0)
    h_scr[...] = _rms(xs, gpre_ref[...]).astype(BF)


def _mix(x3, g_pre, w_bf, gate_bias, prm):
    b, s, _ = x3.shape
    nchunk = s // CHUNK
    const2 = lambda t: (0, 0)
    vec = pl.BlockSpec((1, W_B), const2)
    lora = DECAY_LORA + A_LORA
    rows = jnp.arange(b * CHUNK)
    tri = jnp.logical_and(rows[:, None] >= rows[None, :],
                          rows[:, None] // CHUNK == rows[None, :] // CHUNK).astype(BF)
    this = lambda t: (0, jnp.minimum(t, nchunk - 1), 0)
    ahead = lambda t: (0, jnp.minimum(t + 1, nchunk - 1), 0)
    qkv, gates, yb = pl.pallas_call(
        _mix_kernel,
        grid=(nchunk + 1,),
        in_specs=[
            pl.BlockSpec((b, CHUNK, D_MODEL), lambda t: (0, 0, 0)),
            pl.BlockSpec((b, CHUNK, D_MODEL), ahead),
            pl.BlockSpec((1, D_MODEL), const2),
            pl.BlockSpec((D_MODEL, IN_COLS), const2),
            pl.BlockSpec((1, GATE_COLS), const2),
            pl.BlockSpec((1, RWKV_COLS), const2),
            vec,
            pl.BlockSpec((lora, W_B), const2),
            vec,
            pl.BlockSpec((lora, W_B), const2),
            pl.BlockSpec((GATE_LORA, W_B), const2),
            vec, vec, vec, vec, vec,
            pl.BlockSpec((SEG_W, SEG_W), const2),
            pl.BlockSpec((b * CHUNK, b * CHUNK), const2),
        ],
        out_specs=[
            pl.BlockSpec((b, CHUNK, ATT_COLS), this),
            pl.BlockSpec((b, CHUNK, GATE_COLS), this),
            pl.BlockSpec((b, CHUNK, W_B), lambda t: (0, jnp.maximum(t - 1, 0), 0)),
        ],
        out_shape=[
            jax.ShapeDtypeStruct((b, s, ATT_COLS), BF),
            jax.ShapeDtypeStruct((b, s, GATE_COLS), BF),
            jax.ShapeDtypeStruct((b, s, W_B), BF),
        ],
        scratch_shapes=[pltpu.VMEM((b * CHUNK, D_MODEL), BF),
                        pltpu.VMEM((2, b * CHUNK, RWKV_COLS), F32),
                        pltpu.VMEM((b, 1, RWKV_COLS), F32),
                        pltpu.VMEM((b, N_HEAD, 2 * HEAD, 2 * HEAD), F32),
                        pltpu.VMEM((b * CHUNK, W_B), F32)],
        compiler_params=pltpu.CompilerParams(
            dimension_semantics=("arbitrary",), vmem_limit_bytes=VMEM_LIMIT),
        name="inproj_rwkv7",
    )(x3, x3, g_pre, w_bf, gate_bias, prm["mu"], prm["w0"], prm["w2"], prm["a0"], prm["a2"],
      prm["g2"], prm["k_k"], prm["k_a"], prm["r_k"], prm["ln_w"], prm["ln_b"], prm["bd"], tri)
    n = b * s
    return qkv.reshape(n, ATT_COLS), gates.reshape(n, GATE_COLS), yb.reshape(n, W_B)


def _attn_kernel(q_ref, kp_ref, kc_ref, vp_ref, vc_ref, base_ref, o_ref, kwin, vwin, bias_scr):
    i = pl.program_id(1)

    @pl.when(jnp.logical_and(pl.program_id(0) == 0, i == 0))
    def _():
        qi = lax.broadcasted_iota(jnp.int32, (ATT_Q_SUB, ATT_K_WIN), 0)
        kj = lax.broadcasted_iota(jnp.int32, (ATT_Q_SUB, ATT_K_WIN), 1)
        dchunk = (BAND_PREV + qi // CHUNK) - kj // CHUNK
        band = jnp.logical_and(dchunk >= 0, dchunk <= BAND_PREV)
        for h in range(N_HEAD):
            rows = jnp.broadcast_to(base_ref[h:h + 1, :], (ATT_Q_SUB, ATT_BASE))
            toep = pltpu.roll(rows, 0, 1, stride=1, stride_axis=0)
            bias_scr[h] = jnp.where(band, toep[:, 0:ATT_K_WIN] * LOG2E, NEG_INF)

    kwin[0:ROW_TILE, :] = kp_ref[...]
    kwin[ROW_TILE:2 * ROW_TILE, :] = kc_ref[...]
    vwin[0:ROW_TILE, :] = vp_ref[...]
    vwin[ROW_TILE:2 * ROW_TILE, :] = vc_ref[...]
    lane = lax.broadcasted_iota(jnp.int32, (ATT_Q_SUB, 2 * HEAD), 1)
    col = lax.broadcasted_iota(jnp.int32, (ATT_Q_SUB, ATT_K_WIN), 1)

    def block(first):
        for j in range(ROW_TILE // ATT_Q_SUB):
            r0 = j * ATT_Q_SUB
            hs = range(N_HEAD)
            cs = [(h // 2) * 2 * HEAD for h in hs]
            ss = []
            for h in hs:
                q2 = q_ref[r0:r0 + ATT_Q_SUB, cs[h]:cs[h] + 2 * HEAD]
                in_head = (lane >= HEAD) if h % 2 else (lane < HEAD)
                qm = jnp.where(in_head, q2, jnp.zeros_like(q2))
                ss.append(lax.dot_general(qm, kwin[r0:r0 + ATT_K_WIN, cs[h]:cs[h] + 2 * HEAD],
                                          (((1,), (1,)), ((), ())), preferred_element_type=F32))
            exs, ls = [], []
            for h in hs:
                s = ss[h] + bias_scr[h]
                if first:
                    s = jnp.where(col >= ROW_TILE - r0, s, NEG_INF)
                m = jnp.max(s, axis=-1, keepdims=True)
                ex = jnp.exp2(s - m)
                ls.append(jnp.sum(ex, axis=-1, keepdims=True))
                exs.append(ex.astype(BF))
            outs = [jnp.dot(exs[h], vwin[r0:r0 + ATT_K_WIN, cs[h]:cs[h] + 2 * HEAD],
                            preferred_element_type=F32) / ls[h] for h in hs]
            for p in range(N_HEAD // 2):
                o = jnp.where(lane < HEAD, outs[2 * p], outs[2 * p + 1])
                o_ref[r0:r0 + ATT_Q_SUB, cs[2 * p]:cs[2 * p] + 2 * HEAD] = o.astype(BF)

    pl.when(i == 0)(lambda: block(True))
    pl.when(i > 0)(lambda: block(False))


def _attn_bias_base(rel_bias):
    pos = jnp.arange(ATT_BASE)
    d = jnp.where(pos < ATT_K_WIN, pos, pos - ATT_BASE)
    idx = jnp.clip(BAND_PREV * CHUNK - d, -(CHUNK - 1), REL_CLIP) + (CHUNK - 1)
    return rel_bias.astype(F32)[:, idx]


def _attention(qkv, bias_base, b, s):
    n = b * s
    nblk = s // ROW_TILE
    blk = (ROW_TILE, W_A)
    return pl.pallas_call(
        _attn_kernel,
        grid=(b, nblk),
        in_specs=[
            pl.BlockSpec(blk, lambda bi, i: (bi * nblk + i, 0)),
            pl.BlockSpec(blk, lambda bi, i: (bi * nblk + jnp.maximum(i - 1, 0), 1)),
            pl.BlockSpec(blk, lambda bi, i: (bi * nblk + i, 1)),
            pl.BlockSpec(blk, lambda bi, i: (bi * nblk + jnp.maximum(i - 1, 0), 2)),
            pl.BlockSpec(blk, lambda bi, i: (bi * nblk + i, 2)),
            pl.BlockSpec((N_HEAD, ATT_BASE), lambda bi, i: (0, 0)),
        ],
        out_specs=pl.BlockSpec(blk, lambda bi, i: (bi * nblk + i, 0)),
        out_shape=jax.ShapeDtypeStruct((n, W_A), BF),
        scratch_shapes=[pltpu.VMEM((2 * ROW_TILE, W_A), BF),
                        pltpu.VMEM((2 * ROW_TILE, W_A), BF),
                        pltpu.VMEM((N_HEAD, ATT_Q_SUB, ATT_K_WIN), F32)],
        compiler_params=pltpu.CompilerParams(
            dimension_semantics=("arbitrary", "arbitrary"), vmem_limit_bytes=VMEM_LIMIT),
        name="band_attn",
    )(qkv, qkv, qkv, qkv, qkv, bias_base)


FF_TILE = 1024


def _tail_kernel(ya_ref, yb_ref, gate_ref, x_ref, pa_ref, pb_ref, wo_ref, gmix_ref,
                 g1_ref, wu_ref, wd_ref, g2_ref, o_ref):
    half = ROW_TILE // 2
    rows = [slice(r0, r0 + half) for r0 in range(0, ROW_TILE, half)]
    z = []
    for rs in rows:
        ma = jnp.dot(ya_ref[rs, :], pa_ref[...], preferred_element_type=F32)
        mb = jnp.dot(yb_ref[rs, :], pb_ref[...], preferred_element_type=F32)
        merged = (gate_ref[rs, 0:D_MODEL].astype(F32) * ma
                  + gate_ref[rs, D_MODEL:2 * D_MODEL].astype(F32) * mb)
        z.append(jnp.dot(merged.astype(BF), wo_ref[...], preferred_element_type=F32))
    x = [x_ref[rs, :] + _rms(z[i], gmix_ref[...]) for i, rs in enumerate(rows)]
    hf = [_rms(xi, g1_ref[...]).astype(BF) for xi in x]
    acc = [jnp.zeros(xi.shape, F32) for xi in x]
    for c in range(0, D_FF, FF_TILE):
        u = [jnp.dot(h, wu_ref[:, c:c + FF_TILE], preferred_element_type=F32) for h in hf]
        u = [jnp.maximum(ui, 0.0) for ui in u]
        u = [(ui * ui).astype(BF) for ui in u]
        acc = [a + jnp.dot(ui, wd_ref[c:c + FF_TILE, :], preferred_element_type=F32)
               for a, ui in zip(acc, u)]
    for i, rs in enumerate(rows):
        o_ref[rs, :] = x[i] + _rms(acc[i], g2_ref[...])


def _tail(ya, yb, gates, x2, pa, pb, wo, gmix, g1, wu, wd, g2):
    n = x2.shape[0]
    row = lambda i: (i, 0)

    def resident(shape):
        return pl.BlockSpec(shape, lambda i: (0, 0), pipeline_mode=pl.Buffered(1))

    return pl.pallas_call(
        _tail_kernel,
        grid=(n // ROW_TILE,),
        in_specs=[
            pl.BlockSpec((ROW_TILE, W_A), row),
            pl.BlockSpec((ROW_TILE, W_B), row),
            pl.BlockSpec((ROW_TILE, GATE_COLS), row),
            pl.BlockSpec((ROW_TILE, D_MODEL), row),
            resident((W_A, D_MODEL)),
            resident((W_B, D_MODEL)),
            resident((D_MODEL, D_MODEL)),
            resident((1, D_MODEL)),
            resident((1, D_MODEL)),
            resident((D_MODEL, D_FF)),
            resident((D_FF, D_MODEL)),
            resident((1, D_MODEL)),
        ],
        out_specs=pl.BlockSpec((ROW_TILE, D_MODEL), row),
        out_shape=jax.ShapeDtypeStruct((n, D_MODEL), F32),
        compiler_params=pltpu.CompilerParams(
            dimension_semantics=("arbitrary",), vmem_limit_bytes=VMEM_LIMIT),
        name="merge_ffn",
    )(ya, yb, gates, x2, pa, pb, wo, gmix, g1, wu, wd, g2)


def _layer(x2, b, s, lp):
    row = lambda a: a.reshape(1, -1).astype(F32)
    zeros_lora = jnp.zeros((DECAY_LORA, W_B), BF)
    blk = jnp.arange(SEG_W) // HEAD
    prm = {
        "mu": row(lp["shift_mu"]),
        "w0": row(lp["w0"]),
        "w2": jnp.concatenate([lp["w2"].astype(BF), zeros_lora], axis=0),
        "a0": row(lp["a0"]),
        "a2": jnp.concatenate([zeros_lora, lp["a2"].astype(BF)], axis=0),
        "g2": lp["g2"].astype(BF),
        "k_k": row(lp["k_k"]),
        "k_a": row(lp["k_a"]),
        "r_k": row(lp["r_k"]),
        "ln_w": row(lp["ln_x_w"]),
        "ln_b": row(lp["ln_x_b"]),
        "bd": (blk[:, None] == blk[None, :]).astype(BF),
    }
    qkv, gates, yb = _mix(x2.reshape(b, s, D_MODEL), row(lp["pre_mix_g"]),
                          lp["w_in"].astype(BF), row(lp["gate_bias"]), prm)
    ya = _attention(qkv, _attn_bias_base(lp["rel_bias"]), b, s)

    return _tail(ya, yb, gates, x2, lp["proj_a"].astype(BF), lp["proj_b"].astype(BF),
                 lp["w_out"].astype(BF), row(lp["post_mix_g"]), row(lp["pre_ffn_g"]),
                 lp["w_up"].astype(BF), lp["w_down"].astype(BF), row(lp["post_ffn_g"]))


@jax.jit
def _forward(x, params):
    b, s, d = x.shape
    assert d == D_MODEL and s % ROW_TILE == 0
    x2 = x.reshape(b * s, d)
    depth = params["w_in"].shape[0]
    for l in range(depth):
        lp = {name: val[l] for name, val in params.items()}
        x2 = _layer(x2, b, s, lp)
    return x2.reshape(b, s, d)


def kernel(x, pre_mix_g, w_in, gate_bias, rel_bias, shift_mu, w0, w2, a0, a2, g2, k_k, k_a, r_k, ln_x_w, ln_x_b, proj_a, proj_b, w_out, post_mix_g, pre_ffn_g, w_up, w_down, post_ffn_g):
    params = dict(pre_mix_g=pre_mix_g, w_in=w_in, gate_bias=gate_bias, rel_bias=rel_bias,
                  shift_mu=shift_mu, w0=w0, w2=w2, a0=a0, a2=a2, g2=g2, k_k=k_k, k_a=k_a,
                  r_k=r_k, ln_x_w=ln_x_w, ln_x_b=ln_x_b, proj_a=proj_a, proj_b=proj_b,
                  w_out=w_out, post_mix_g=post_mix_g, pre_ffn_g=pre_ffn_g, w_up=w_up,
                  w_down=w_down, post_ffn_g=post_ffn_g)
    return _forward(x, params)
```

```python
import math

import jax
import jax.numpy as jnp
from jax import lax
from jax.experimental import pallas as pl
from jax.experimental.pallas import tpu as pltpu

BF = jnp.bfloat16
F32 = jnp.float32

D_MODEL = 1024
CHUNK = 64
BAND_PREV = 8
REL_CLIP = 256
W_A = D_MODEL // 2
HEAD = 64
N_HEAD = W_A // HEAD
W_B = D_MODEL // 2
DECAY_LORA = 64
A_LORA = 64
GATE_LORA = 128
D_FF = 4 * D_MODEL
ATT_COLS = 3 * W_A
RWKV_COLS = 3 * W_B + DECAY_LORA + A_LORA + GATE_LORA
GATE_COLS = 2 * D_MODEL
IN_COLS = ATT_COLS + RWKV_COLS + GATE_COLS
RMS_EPS = 1e-6
GN_EPS = HEAD * 1e-5
NEG_INF = -1e30
LOG2E = math.log2(math.e)

ROW_TILE = 512
ATT_Q_SUB = 128
ATT_K_WIN = ATT_Q_SUB + BAND_PREV * CHUNK
ATT_BASE = ATT_K_WIN + ATT_Q_SUB
SEG_W = 256
PROJ_PIECE = 256
RWKV_TILE = 128
VMEM_LIMIT = 56 * 1024 * 1024


def _mm(a, b):
    return jnp.dot(a.astype(BF), b.astype(BF), preferred_element_type=F32)


def _mm_nt(a, b):
    return lax.dot_general(a.astype(BF), b.astype(BF), (((1,), (1,)), ((), ())),
                           preferred_element_type=F32)


def _rms(x, g):
    ms = jnp.mean(x * x, axis=-1, keepdims=True)
    return x * lax.rsqrt(ms + RMS_EPS) * g


def _split2(x):
    hi = x.astype(BF)
    lo = (x - hi.astype(F32)).astype(BF)
    return hi, lo


def _seg_sum(x, bd):
    w = bd.shape[0]
    xb = x.astype(BF)
    return jnp.concatenate(
        [jnp.dot(xb[:, j:j + w], bd, preferred_element_type=F32)
         for j in range(0, x.shape[1], w)], axis=1)


def _mix_kernel(x0_ref, x_ref, gpre_ref, w_ref, gbias_ref, mu_ref, w0_ref, w2_ref, a0_ref, a2_ref,
                g2_ref, kk_ref, ka_ref, rk_ref, lnw_ref, lnb_ref, bd_ref, tri_ref,
                qkv_ref, gate_ref, o_ref, h_scr, p_scr, carry_ref, state_ref, y_scr):
    t = pl.program_id(0)
    nb = x_ref.shape[0]
    c = CHUNK
    nsub = x_ref.shape[1] // c
    slab = 2 * HEAD
    cur = (t + 1) % 2
    nxt = t % 2

    @pl.when(t == 0)
    def _():
        carry_ref[...] = jnp.zeros_like(carry_ref)
        state_ref[...] = jnp.zeros_like(state_ref)
        p_scr[1] = jnp.zeros(p_scr.shape[1:], F32)
        for sub in range(nsub):
            x0 = jnp.concatenate([x0_ref[bi, sub * c:(sub + 1) * c, :] for bi in range(nb)], axis=0)
            h_scr[sub] = _rms(x0, gpre_ref[...]).astype(BF)

    def chunk_step(sub):
        rows_out = slice(sub * c, (sub + 1) * c)
        h = h_scr[sub]
        row_w = lax.broadcasted_iota(jnp.int32, (nb * c, PROJ_PIECE), 0)

        def proj_piece(c0):
            c1 = min(c0 + PROJ_PIECE, ATT_COLS if c0 < ATT_COLS else
                     ATT_COLS + RWKV_COLS if c0 < ATT_COLS + RWKV_COLS else IN_COLS)

            def run():
                acc = jnp.dot(h, w_ref[:, c0:c1], preferred_element_type=F32)
                if c0 < ATT_COLS:
                    if c0 < W_A:
                        acc = acc * (HEAD ** -0.5 * LOG2E)
                    out = acc.astype(BF)
                    for bi in range(nb):
                        qkv_ref[bi, rows_out, c0:c1] = out[bi * c:(bi + 1) * c, :]
                elif c0 < ATT_COLS + RWKV_COLS:
                    cols = slice(c0 - ATT_COLS, c1 - ATT_COLS)
                    prev = pltpu.roll(acc, 1, 0)
                    for bi in range(nb):
                        prev = jnp.where(row_w[:, 0:c1 - c0] == bi * c,
                                         jnp.broadcast_to(carry_ref[bi, :, cols], acc.shape), prev)
                        carry_ref[bi, :, cols] = acc[(bi + 1) * c - 1:(bi + 1) * c, :]
                    p_scr[nxt, sub, :, cols] = acc + (prev - acc) * mu_ref[:, cols]
                else:
                    g0 = c0 - ATT_COLS - RWKV_COLS
                    out = jax.nn.sigmoid(acc + gbias_ref[:, g0:g0 + c1 - c0]).astype(BF)
                    for bi in range(nb):
                        gate_ref[bi, rows_out, g0:g0 + c1 - c0] = out[bi * c:(bi + 1) * c, :]
            return run, c1

        pieces = []
        col = 0
        while col < IN_COLS:
            run, col = proj_piece(col)
            pieces.append(run)
        pieces = iter(pieces)

        def fill(count=1):
            for _ in range(count):
                run = next(pieces, None)
                if run is not None:
                    run()

        fill()

        ps = p_scr[cur, sub]
        r = ps[:, 0:W_B]
        k = ps[:, W_B:2 * W_B]
        v = ps[:, 2 * W_B:3 * W_B]
        lora_in = ps[:, 3 * W_B:3 * W_B + DECAY_LORA + A_LORA]
        cg = ps[:, 3 * W_B + DECAY_LORA + A_LORA:]

        u = w0_ref[...] + _mm(jnp.tanh(lora_in), w2_ref[...])
        lw = -(math.exp(-0.5) * LOG2E) * jax.nn.sigmoid(u)
        a = jax.nn.sigmoid(a0_ref[...] + _mm(lora_in, a2_ref[...]))
        g = _mm(jax.nn.sigmoid(cg), g2_ref[...])
        fill()

        bd = bd_ref[...]
        kk = k * kk_ref[...]
        kk = kk * lax.rsqrt(jnp.maximum(_seg_sum(kk * kk, bd), 1e-24))
        fill()
        k = k * (1.0 + (a - 1.0) * ka_ref[...])
        aa = -kk
        bb = kk * a

        tri = tri_ref[...]
        cum = sum(jnp.dot(tri, part, preferred_element_type=F32) for part in _split2(lw))
        fill()
        last = jnp.concatenate(
            [jnp.broadcast_to(cum[(bi + 1) * c - 1:(bi + 1) * c, :], (c, W_B)) for bi in range(nb)],
            axis=0)
        inv = jnp.exp2(-cum)
        w_end = jnp.exp2(last)
        tail = w_end * inv
        at = aa * jnp.exp2(cum - lw)
        bt = bb * inv
        kt = k * inv
        rt = r * jnp.exp2(cum)
        bh = bb * tail
        kh = k * tail
        bonus = _seg_sum(r * k * rk_ref[...], bd) * v

        lane = lax.broadcasted_iota(jnp.int32, (c, slab), 1)
        lane2 = lax.broadcasted_iota(jnp.int32, (2 * c, slab), 1)
        row2 = lax.broadcasted_iota(jnp.int32, (2 * c, slab), 0)
        half = [lane < HEAD, lane >= HEAD]
        half2 = [lane2 < HEAD, lane2 >= HEAD]
        rl = row2 & (c - 1)
        cl = lane2 & (c - 1)
        mask4 = jnp.logical_or(rl > cl, jnp.logical_and(row2 >= c, rl == cl))

        chains = [(bi, pp, e) for bi in range(nb) for pp in range(N_HEAD // 2) for e in range(2)]
        n = range(len(chains))

        def blk(x, ch):
            bi, pp, _ = ch
            return x[bi * c:(bi + 1) * c, pp * slab:(pp + 1) * slab]

        ar = [jnp.concatenate([blk(at, ch), blk(rt, ch)], axis=0) for ch in chains]
        bk = [jnp.concatenate([blk(bt, ch), blk(kt, ch)], axis=0).astype(BF) for ch in chains]
        bkh = [jnp.concatenate([blk(bh, ch), blk(kh, ch)], axis=0).astype(BF) for ch in chains]
        vsw = [pltpu.roll(blk(v, ch), HEAD, 1) for ch in chains]
        vx = [jnp.where(half[1 - chains[i][2]], vsw[i], 0.0).astype(BF) for i in n]
        a4 = [jnp.where(mask4,
                        _mm_nt(jnp.where(half2[chains[i][2]], ar[i], 0.0), bk[i]), 0.0).astype(BF)
              for i in n]
        ta = [jnp.where(half[chains[i][2]], blk(at, chains[i]), 0.0) for i in n]
        pw = [a4[i][0:c, :] for i in n]
        zeros_half = jnp.zeros((c, slab), BF)
        n_step = int(math.log2(c))
        for step in range(n_step):
            fill()
            if step + 1 < n_step:
                rhs = [jnp.concatenate(
                    [jnp.concatenate([ta[i].astype(BF), pw[i]], axis=1),
                     jnp.concatenate([vx[i], zeros_half], axis=1)], axis=0) for i in n]
            else:
                rhs = [jnp.concatenate([ta[i].astype(BF), vx[i]], axis=0) for i in n]
            both = [jnp.dot(pw[i], rhs[i], preferred_element_type=F32) for i in n]
            ta = [ta[i] + both[i][:, 0:slab] for i in n]
            if step + 1 < n_step:
                pw = [both[i][:, slab:2 * slab].astype(BF) for i in n]
        gm = [jnp.concatenate(
            [jnp.where(half[chains[i][2]], blk(rt, chains[i]), 0.0), ta[i]], axis=0) for i in n]
        fill(2)
        s0 = [state_ref[ch[0], 2 * ch[1] + ch[2]] for ch in chains]
        ys = [_mm_nt(gm[i], s0[i]) for i in n]
        uv = [jnp.concatenate([(ys[i][c:2 * c, :] + ta[i]).astype(BF), vx[i]], axis=0) for i in n]
        fill(2)
        yo = [ys[i][0:c, :] + jnp.dot(a4[i][c:2 * c, :], uv[i], preferred_element_type=F32)
              for i in n]
        for i in n:
            bi, pp, e = chains[i]
            upd = lax.dot_general(uv[i], bkh[i], (((0,), (0,)), ((), ())),
                                  preferred_element_type=F32)
            keep = jnp.logical_and(half2[e], (row2 >= HEAD) if e == 0 else (row2 < HEAD))
            state_ref[bi, 2 * pp + e] = jnp.where(keep, s0[i] * blk(w_end, chains[i])[0:1, :] + upd, 0.0)
        for bi in range(nb):
            for pp in range(N_HEAD // 2):
                i0 = chains.index((bi, pp, 0))
                both = jnp.where(half[1], yo[i0], yo[i0 + 1])
                y_scr[sub, bi * c:(bi + 1) * c, pp * slab:(pp + 1) * slab] = pltpu.roll(both, HEAD, 1)
        fill(2)

        y = y_scr[sub]
        yc = y - _seg_sum(y, bd) * (1.0 / HEAD)
        fill()
        var = _seg_sum(yc * yc, bd) * (1.0 / HEAD)
        fill(IN_COLS // PROJ_PIECE)
        yn = yc * lax.rsqrt(var + GN_EPS) * lnw_ref[...] + lnb_ref[...]
        out = ((yn + bonus) * g).astype(BF)
        for bi in range(nb):
            o_ref[bi, rows_out, :] = out[bi * c:(bi + 1) * c, :]

        xs = jnp.concatenate([x_ref[bi, rows_out, :] for bi in range(nb)], axis=0)
        h_scr[sub] = _rms(xs, gpre_ref[...]).astype(BF)

    for sub in range(nsub):
        chunk_step(sub)


def _mix(x3, g_pre, w_bf, gate_bias, prm):
    b, s, _ = x3.shape
    tt = RWKV_TILE
    nstep = s // tt
    const2 = lambda t: (0, 0)
    vec = pl.BlockSpec((1, W_B), const2)
    lora = DECAY_LORA + A_LORA
    rows = jnp.arange(b * CHUNK)
    tri = jnp.logical_and(rows[:, None] >= rows[None, :],
                          rows[:, None] // CHUNK == rows[None, :] // CHUNK).astype(BF)
    this = lambda t: (0, jnp.minimum(t, nstep - 1), 0)
    ahead = lambda t: (0, jnp.minimum(t + 1, nstep - 1), 0)
    qkv, gates, yb = pl.pallas_call(
        _mix_kernel,
        grid=(nstep + 1,),
        in_specs=[
            pl.BlockSpec((b, tt, D_MODEL), lambda t: (0, 0, 0)),
            pl.BlockSpec((b, tt, D_MODEL), ahead),
            pl.BlockSpec((1, D_MODEL), const2),
            pl.BlockSpec((D_MODEL, IN_COLS), const2),
            pl.BlockSpec((1, GATE_COLS), const2),
            pl.BlockSpec((1, RWKV_COLS), const2),
            vec,
            pl.BlockSpec((lora, W_B), const2),
            vec,
            pl.BlockSpec((lora, W_B), const2),
            pl.BlockSpec((GATE_LORA, W_B), const2),
            vec, vec, vec, vec, vec,
            pl.BlockSpec((SEG_W, SEG_W), const2),
            pl.BlockSpec((b * CHUNK, b * CHUNK), const2),
        ],
        out_specs=[
            pl.BlockSpec((b, tt, ATT_COLS), this),
            pl.BlockSpec((b, tt, GATE_COLS), this),
            pl.BlockSpec((b, tt, W_B), lambda t: (0, jnp.maximum(t - 1, 0), 0)),
        ],
        out_shape=[
            jax.ShapeDtypeStruct((b, s, ATT_COLS), BF),
            jax.ShapeDtypeStruct((b, s, GATE_COLS), BF),
            jax.ShapeDtypeStruct((b, s, W_B), BF),
        ],
        scratch_shapes=[pltpu.VMEM((tt // CHUNK, b * CHUNK, D_MODEL), BF),
                        pltpu.VMEM((2, tt // CHUNK, b * CHUNK, RWKV_COLS), F32),
                        pltpu.VMEM((b, 1, RWKV_COLS), F32),
                        pltpu.VMEM((b, N_HEAD, 2 * HEAD, 2 * HEAD), F32),
                        pltpu.VMEM((tt // CHUNK, b * CHUNK, W_B), F32)],
        compiler_params=pltpu.CompilerParams(
            dimension_semantics=("arbitrary",), vmem_limit_bytes=VMEM_LIMIT),
        name="inproj_rwkv7",
    )(x3, x3, g_pre, w_bf, gate_bias, prm["mu"], prm["w0"], prm["w2"], prm["a0"], prm["a2"],
      prm["g2"], prm["k_k"], prm["k_a"], prm["r_k"], prm["ln_w"], prm["ln_b"], prm["bd"], tri)
    n = b * s
    return qkv.reshape(n, ATT_COLS), gates.reshape(n, GATE_COLS), yb.reshape(n, W_B)


def _attn_kernel(q_ref, kp_ref, kc_ref, vp_ref, vc_ref, base_ref, o_ref, kwin, vwin, bias_scr):
    i = pl.program_id(1)

    @pl.when(jnp.logical_and(pl.program_id(0) == 0, i == 0))
    def _():
        qi = lax.broadcasted_iota(jnp.int32, (ATT_Q_SUB, ATT_K_WIN), 0)
        kj = lax.broadcasted_iota(jnp.int32, (ATT_Q_SUB, ATT_K_WIN), 1)
        dchunk = (BAND_PREV + qi // CHUNK) - kj // CHUNK
        band = jnp.logical_and(dchunk >= 0, dchunk <= BAND_PREV)
        for h in range(N_HEAD):
            rows = jnp.broadcast_to(base_ref[h:h + 1, :], (ATT_Q_SUB, ATT_BASE))
            toep = pltpu.roll(rows, 0, 1, stride=1, stride_axis=0)
            bias_scr[h] = jnp.where(band, toep[:, 0:ATT_K_WIN] * LOG2E, NEG_INF)

    kwin[0:ROW_TILE, :] = kp_ref[...]
    kwin[ROW_TILE:2 * ROW_TILE, :] = kc_ref[...]
    vwin[0:ROW_TILE, :] = vp_ref[...]
    vwin[ROW_TILE:2 * ROW_TILE, :] = vc_ref[...]
    lane = lax.broadcasted_iota(jnp.int32, (ATT_Q_SUB, 2 * HEAD), 1)
    col = lax.broadcasted_iota(jnp.int32, (ATT_Q_SUB, ATT_K_WIN), 1)

    def block(first):
        for j in range(ROW_TILE // ATT_Q_SUB):
            r0 = j * ATT_Q_SUB
            hs = range(N_HEAD)
            cs = [(h // 2) * 2 * HEAD for h in hs]
            ss = []
            for h in hs:
                q2 = q_ref[r0:r0 + ATT_Q_SUB, cs[h]:cs[h] + 2 * HEAD]
                in_head = (lane >= HEAD) if h % 2 else (lane < HEAD)
                qm = jnp.where(in_head, q2, jnp.zeros_like(q2))
                ss.append(lax.dot_general(qm, kwin[r0:r0 + ATT_K_WIN, cs[h]:cs[h] + 2 * HEAD],
                                          (((1,), (1,)), ((), ())), preferred_element_type=F32))
            exs, ls = [], []
            for h in hs:
                s = ss[h] + bias_scr[h]
                if first:
                    s = jnp.where(col >= ROW_TILE - r0, s, NEG_INF)
                m = jnp.max(s, axis=-1, keepdims=True)
                ex = jnp.exp2(s - m)
                ls.append(jnp.sum(ex, axis=-1, keepdims=True))
                exs.append(ex.astype(BF))
            outs = [jnp.dot(exs[h], vwin[r0:r0 + ATT_K_WIN, cs[h]:cs[h] + 2 * HEAD],
                            preferred_element_type=F32) / ls[h] for h in hs]
            for p in range(N_HEAD // 2):
                o = jnp.where(lane < HEAD, outs[2 * p], outs[2 * p + 1])
                o_ref[r0:r0 + ATT_Q_SUB, cs[2 * p]:cs[2 * p] + 2 * HEAD] = o.astype(BF)

    pl.when(i == 0)(lambda: block(True))
    pl.when(i > 0)(lambda: block(False))


def _attn_bias_base(rel_bias):
    pos = jnp.arange(ATT_BASE)
    d = jnp.where(pos < ATT_K_WIN, pos, pos - ATT_BASE)
    idx = jnp.clip(BAND_PREV * CHUNK - d, -(CHUNK - 1), REL_CLIP) + (CHUNK - 1)
    return rel_bias.astype(F32)[:, idx]


def _attention(qkv, bias_base, b, s):
    n = b * s
    nblk = s // ROW_TILE
    blk = (ROW_TILE, W_A)
    return pl.pallas_call(
        _attn_kernel,
        grid=(b, nblk),
        in_specs=[
            pl.BlockSpec(blk, lambda bi, i: (bi * nblk + i, 0)),
            pl.BlockSpec(blk, lambda bi, i: (bi * nblk + jnp.maximum(i - 1, 0), 1)),
            pl.BlockSpec(blk, lambda bi, i: (bi * nblk + i, 1)),
            pl.BlockSpec(blk, lambda bi, i: (bi * nblk + jnp.maximum(i - 1, 0), 2)),
            pl.BlockSpec(blk, lambda bi, i: (bi * nblk + i, 2)),
            pl.BlockSpec((N_HEAD, ATT_BASE), lambda bi, i: (0, 0)),
        ],
        out_specs=pl.BlockSpec(blk, lambda bi, i: (bi * nblk + i, 0)),
        out_shape=jax.ShapeDtypeStruct((n, W_A), BF),
        scratch_shapes=[pltpu.VMEM((2 * ROW_TILE, W_A), BF),
                        pltpu.VMEM((2 * ROW_TILE, W_A), BF),
                        pltpu.VMEM((N_HEAD, ATT_Q_SUB, ATT_K_WIN), F32)],
        compiler_params=pltpu.CompilerParams(
            dimension_semantics=("arbitrary", "arbitrary"), vmem_limit_bytes=VMEM_LIMIT),
        name="band_attn",
    )(qkv, qkv, qkv, qkv, qkv, bias_base)


FF_TILE = 1024


def _tail_kernel(ya_ref, yb_ref, gate_ref, x_ref, pa_ref, pb_ref, wo_ref, gmix_ref,
                 g1_ref, wu_ref, wd_ref, g2_ref, o_ref):
    half = ROW_TILE // 2
    rows = [slice(r0, r0 + half) for r0 in range(0, ROW_TILE, half)]
    z = []
    for rs in rows:
        ma = jnp.dot(ya_ref[rs, :], pa_ref[...], preferred_element_type=F32)
        mb = jnp.dot(yb_ref[rs, :], pb_ref[...], preferred_element_type=F32)
        merged = (gate_ref[rs, 0:D_MODEL].astype(F32) * ma
                  + gate_ref[rs, D_MODEL:2 * D_MODEL].astype(F32) * mb)
        z.append(jnp.dot(merged.astype(BF), wo_ref[...], preferred_element_type=F32))
    x = [x_ref[rs, :] + _rms(z[i], gmix_ref[...]) for i, rs in enumerate(rows)]
    hf = [_rms(xi, g1_ref[...]).astype(BF) for xi in x]
    acc = [jnp.zeros(xi.shape, F32) for xi in x]
    for c in range(0, D_FF, FF_TILE):
        u = [jnp.dot(h, wu_ref[:, c:c + FF_TILE], preferred_element_type=F32) for h in hf]
        u = [jnp.maximum(ui, 0.0) for ui in u]
        u = [(ui * ui).astype(BF) for ui in u]
        acc = [a + jnp.dot(ui, wd_ref[c:c + FF_TILE, :], preferred_element_type=F32)
               for a, ui in zip(acc, u)]
    for i, rs in enumerate(rows):
        o_ref[rs, :] = x[i] + _rms(acc[i], g2_ref[...])


def _tail(ya, yb, gates, x2, pa, pb, wo, gmix, g1, wu, wd, g2):
    n = x2.shape[0]
    row = lambda i: (i, 0)

    def resident(shape):
        return pl.BlockSpec(shape, lambda i: (0, 0), pipeline_mode=pl.Buffered(1))

    return pl.pallas_call(
        _tail_kernel,
        grid=(n // ROW_TILE,),
        in_specs=[
            pl.BlockSpec((ROW_TILE, W_A), row),
            pl.BlockSpec((ROW_TILE, W_B), row),
            pl.BlockSpec((ROW_TILE, GATE_COLS), row),
            pl.BlockSpec((ROW_TILE, D_MODEL), row),
            resident((W_A, D_MODEL)),
            resident((W_B, D_MODEL)),
            resident((D_MODEL, D_MODEL)),
            resident((1, D_MODEL)),
            resident((1, D_MODEL)),
            resident((D_MODEL, D_FF)),
            resident((D_FF, D_MODEL)),
            resident((1, D_MODEL)),
        ],
        out_specs=pl.BlockSpec((ROW_TILE, D_MODEL), row),
        out_shape=jax.ShapeDtypeStruct((n, D_MODEL), F32),
        compiler_params=pltpu.CompilerParams(
            dimension_semantics=("arbitrary",), vmem_limit_bytes=VMEM_LIMIT),
        name="merge_ffn",
    )(ya, yb, gates, x2, pa, pb, wo, gmix, g1, wu, wd, g2)


def _layer(x2, b, s, lp):
    row = lambda a: a.reshape(1, -1).astype(F32)
    zeros_lora = jnp.zeros((DECAY_LORA, W_B), BF)
    blk = jnp.arange(SEG_W) // HEAD
    prm = {
        "mu": row(lp["shift_mu"]),
        "w0": row(lp["w0"]),
        "w2": jnp.concatenate([lp["w2"].astype(BF), zeros_lora], axis=0),
        "a0": row(lp["a0"]),
        "a2": jnp.concatenate([zeros_lora, lp["a2"].astype(BF)], axis=0),
        "g2": lp["g2"].astype(BF),
        "k_k": row(lp["k_k"]),
        "k_a": row(lp["k_a"]),
        "r_k": row(lp["r_k"]),
        "ln_w": row(lp["ln_x_w"]),
        "ln_b": row(lp["ln_x_b"]),
        "bd": (blk[:, None] == blk[None, :]).astype(BF),
    }
    qkv, gates, yb = _mix(x2.reshape(b, s, D_MODEL), row(lp["pre_mix_g"]),
                          lp["w_in"].astype(BF), row(lp["gate_bias"]), prm)
    ya = _attention(qkv, _attn_bias_base(lp["rel_bias"]), b, s)

    return _tail(ya, yb, gates, x2, lp["proj_a"].astype(BF), lp["proj_b"].astype(BF),
                 lp["w_out"].astype(BF), row(lp["post_mix_g"]), row(lp["pre_ffn_g"]),
                 lp["w_up"].astype(BF), lp["w_down"].astype(BF), row(lp["post_ffn_g"]))


@jax.jit
def _forward(x, params):
    b, s, d = x.shape
    assert d == D_MODEL and s % ROW_TILE == 0 and s % RWKV_TILE == 0
    x2 = x.reshape(b * s, d)
    depth = params["w_in"].shape[0]
    for l in range(depth):
        lp = {name: val[l] for name, val in params.items()}
        x2 = _layer(x2, b, s, lp)
    return x2.reshape(b, s, d)


def kernel(x, pre_mix_g, w_in, gate_bias, rel_bias, shift_mu, w0, w2, a0, a2, g2, k_k, k_a, r_k, ln_x_w, ln_x_b, proj_a, proj_b, w_out, post_mix_g, pre_ffn_g, w_up, w_down, post_ffn_g):
    params = dict(pre_mix_g=pre_mix_g, w_in=w_in, gate_bias=gate_bias, rel_bias=rel_bias,
                  shift_mu=shift_mu, w0=w0, w2=w2, a0=a0, a2=a2, g2=g2, k_k=k_k, k_a=k_a,
                  r_k=r_k, ln_x_w=ln_x_w, ln_x_b=ln_x_b, proj_a=proj_a, proj_b=proj_b,
                  w_out=w_out, post_mix_g=post_mix_g, pre_ffn_g=pre_ffn_g, w_up=w_up,
                  w_down=w_down, post_ffn_g=post_ffn_g)
    return _forward(x, params)
```

```python
import math

import jax
import jax.numpy as jnp
from jax import lax
from jax.experimental import pallas as pl
from jax.experimental.pallas import tpu as pltpu

BF = jnp.bfloat16
F32 = jnp.float32

D_MODEL = 1024
CHUNK = 64
BAND_PREV = 8
REL_CLIP = 256
W_A = D_MODEL // 2
HEAD = 64
N_HEAD = W_A // HEAD
W_B = D_MODEL // 2
DECAY_LORA = 64
A_LORA = 64
GATE_LORA = 128
D_FF = 4 * D_MODEL
ATT_COLS = 3 * W_A
RWKV_COLS = 3 * W_B + DECAY_LORA + A_LORA + GATE_LORA
GATE_COLS = 2 * D_MODEL
IN_COLS = ATT_COLS + RWKV_COLS + GATE_COLS
RMS_EPS = 1e-6
GN_EPS = HEAD * 1e-5
NEG_INF = -1e30
LOG2E = math.log2(math.e)

ROW_TILE = 512
ATT_Q_SUB = 128
ATT_K_WIN = ATT_Q_SUB + BAND_PREV * CHUNK
ATT_BASE = ATT_K_WIN + ATT_Q_SUB
SEG_W = 256
PROJ_PIECE = 256
RWKV_TILE = 256
VMEM_LIMIT = 56 * 1024 * 1024


def _mm(a, b):
    return jnp.dot(a.astype(BF), b.astype(BF), preferred_element_type=F32)


def _mm_nt(a, b):
    return lax.dot_general(a.astype(BF), b.astype(BF), (((1,), (1,)), ((), ())),
                           preferred_element_type=F32)


def _rms(x, g):
    ms = jnp.mean(x * x, axis=-1, keepdims=True)
    return x * lax.rsqrt(ms + RMS_EPS) * g


def _split2(x):
    hi = x.astype(BF)
    lo = (x - hi.astype(F32)).astype(BF)
    return hi, lo


def _seg_sum(x, bd):
    w = bd.shape[0]
    xb = x.astype(BF)
    return jnp.concatenate(
        [jnp.dot(xb[:, j:j + w], bd, preferred_element_type=F32)
         for j in range(0, x.shape[1], w)], axis=1)


def _mix_kernel(x0_ref, x_ref, gpre_ref, w_ref, gbias_ref, mu_ref, w0_ref, w2_ref, a0_ref, a2_ref,
                g2_ref, kk_ref, ka_ref, rk_ref, lnw_ref, lnb_ref, bd_ref, tri_ref,
                qkv_ref, gate_ref, o_ref, h_scr, p_scr, carry_ref, state_ref, y_scr):
    t = pl.program_id(0)
    nb = x_ref.shape[0]
    c = CHUNK
    nsub = x_ref.shape[1] // c
    slab = 2 * HEAD
    cur = (t + 1) % 2
    nxt = t % 2

    @pl.when(t == 0)
    def _():
        carry_ref[...] = jnp.zeros_like(carry_ref)
        state_ref[...] = jnp.zeros_like(state_ref)
        p_scr[1] = jnp.zeros(p_scr.shape[1:], F32)
        for sub in range(nsub):
            x0 = jnp.concatenate([x0_ref[bi, sub * c:(sub + 1) * c, :] for bi in range(nb)], axis=0)
            h_scr[sub] = _rms(x0, gpre_ref[...]).astype(BF)

    def chunk_step(sub):
        rows_out = slice(sub * c, (sub + 1) * c)
        h = h_scr[sub]
        row_w = lax.broadcasted_iota(jnp.int32, (nb * c, PROJ_PIECE), 0)

        def proj_piece(c0):
            c1 = min(c0 + PROJ_PIECE, ATT_COLS if c0 < ATT_COLS else
                     ATT_COLS + RWKV_COLS if c0 < ATT_COLS + RWKV_COLS else IN_COLS)

            def run():
                acc = jnp.dot(h, w_ref[:, c0:c1], preferred_element_type=F32)
                if c0 < ATT_COLS:
                    if c0 < W_A:
                        acc = acc * (HEAD ** -0.5 * LOG2E)
                    out = acc.astype(BF)
                    for bi in range(nb):
                        qkv_ref[bi, rows_out, c0:c1] = out[bi * c:(bi + 1) * c, :]
                elif c0 < ATT_COLS + RWKV_COLS:
                    cols = slice(c0 - ATT_COLS, c1 - ATT_COLS)
                    prev = pltpu.roll(acc, 1, 0)
                    for bi in range(nb):
                        prev = jnp.where(row_w[:, 0:c1 - c0] == bi * c,
                                         jnp.broadcast_to(carry_ref[bi, :, cols], acc.shape), prev)
                        carry_ref[bi, :, cols] = acc[(bi + 1) * c - 1:(bi + 1) * c, :]
                    p_scr[nxt, sub, :, cols] = acc + (prev - acc) * mu_ref[:, cols]
                else:
                    g0 = c0 - ATT_COLS - RWKV_COLS
                    out = jax.nn.sigmoid(acc + gbias_ref[:, g0:g0 + c1 - c0]).astype(BF)
                    for bi in range(nb):
                        gate_ref[bi, rows_out, g0:g0 + c1 - c0] = out[bi * c:(bi + 1) * c, :]
            return run, c1

        pieces = []
        col = 0
        while col < IN_COLS:
            run, col = proj_piece(col)
            pieces.append(run)
        pieces = iter(pieces)

        def fill(count=1):
            for _ in range(count):
                run = next(pieces, None)
                if run is not None:
                    run()

        fill()

        ps = p_scr[cur, sub]
        r = ps[:, 0:W_B]
        k = ps[:, W_B:2 * W_B]
        v = ps[:, 2 * W_B:3 * W_B]
        lora_in = ps[:, 3 * W_B:3 * W_B + DECAY_LORA + A_LORA]
        cg = ps[:, 3 * W_B + DECAY_LORA + A_LORA:]

        u = w0_ref[...] + _mm(jnp.tanh(lora_in), w2_ref[...])
        lw = -(math.exp(-0.5) * LOG2E) * jax.nn.sigmoid(u)
        a = jax.nn.sigmoid(a0_ref[...] + _mm(lora_in, a2_ref[...]))
        g = _mm(jax.nn.sigmoid(cg), g2_ref[...])
        fill()

        bd = bd_ref[...]
        kk = k * kk_ref[...]
        kk = kk * lax.rsqrt(jnp.maximum(_seg_sum(kk * kk, bd), 1e-24))
        fill()
        k = k * (1.0 + (a - 1.0) * ka_ref[...])
        aa = -kk
        bb = kk * a

        tri = tri_ref[...]
        cum = sum(jnp.dot(tri, part, preferred_element_type=F32) for part in _split2(lw))
        fill()
        last = jnp.concatenate(
            [jnp.broadcast_to(cum[(bi + 1) * c - 1:(bi + 1) * c, :], (c, W_B)) for bi in range(nb)],
            axis=0)
        inv = jnp.exp2(-cum)
        w_end = jnp.exp2(last)
        tail = w_end * inv
        at = aa * jnp.exp2(cum - lw)
        bt = bb * inv
        kt = k * inv
        rt = r * jnp.exp2(cum)
        bh = bb * tail
        kh = k * tail
        bonus = _seg_sum(r * k * rk_ref[...], bd) * v

        lane = lax.broadcasted_iota(jnp.int32, (c, slab), 1)
        lane2 = lax.broadcasted_iota(jnp.int32, (2 * c, slab), 1)
        row2 = lax.broadcasted_iota(jnp.int32, (2 * c, slab), 0)
        half = [lane < HEAD, lane >= HEAD]
        half2 = [lane2 < HEAD, lane2 >= HEAD]
        rl = row2 & (c - 1)
        cl = lane2 & (c - 1)
        mask4 = jnp.logical_or(rl > cl, jnp.logical_and(row2 >= c, rl == cl))

        chains = [(bi, pp, e) for bi in range(nb) for pp in range(N_HEAD // 2) for e in range(2)]
        n = range(len(chains))

        def blk(x, ch):
            bi, pp, _ = ch
            return x[bi * c:(bi + 1) * c, pp * slab:(pp + 1) * slab]

        ar = [jnp.concatenate([blk(at, ch), blk(rt, ch)], axis=0) for ch in chains]
        bk = [jnp.concatenate([blk(bt, ch), blk(kt, ch)], axis=0).astype(BF) for ch in chains]
        bkh = [jnp.concatenate([blk(bh, ch), blk(kh, ch)], axis=0).astype(BF) for ch in chains]
        vsw = [pltpu.roll(blk(v, ch), HEAD, 1) for ch in chains]
        vx = [jnp.where(half[1 - chains[i][2]], vsw[i], 0.0).astype(BF) for i in n]
        a4 = [jnp.where(mask4,
                        _mm_nt(jnp.where(half2[chains[i][2]], ar[i], 0.0), bk[i]), 0.0).astype(BF)
              for i in n]
        ta = [jnp.where(half[chains[i][2]], blk(at, chains[i]), 0.0) for i in n]
        pw = [a4[i][0:c, :] for i in n]
        zeros_half = jnp.zeros((c, slab), BF)
        n_step = int(math.log2(c))
        for step in range(n_step):
            fill()
            if step + 1 < n_step:
                rhs = [jnp.concatenate(
                    [jnp.concatenate([ta[i].astype(BF), pw[i]], axis=1),
                     jnp.concatenate([vx[i], zeros_half], axis=1)], axis=0) for i in n]
            else:
                rhs = [jnp.concatenate([ta[i].astype(BF), vx[i]], axis=0) for i in n]
            both = [jnp.dot(pw[i], rhs[i], preferred_element_type=F32) for i in n]
            ta = [ta[i] + both[i][:, 0:slab] for i in n]
            if step + 1 < n_step:
                pw = [both[i][:, slab:2 * slab].astype(BF) for i in n]
        gm = [jnp.concatenate(
            [jnp.where(half[chains[i][2]], blk(rt, chains[i]), 0.0), ta[i]], axis=0) for i in n]
        fill(2)
        s0 = [state_ref[ch[0], 2 * ch[1] + ch[2]] for ch in chains]
        ys = [_mm_nt(gm[i], s0[i]) for i in n]
        uv = [jnp.concatenate([(ys[i][c:2 * c, :] + ta[i]).astype(BF), vx[i]], axis=0) for i in n]
        fill(2)
        yo = [ys[i][0:c, :] + jnp.dot(a4[i][c:2 * c, :], uv[i], preferred_element_type=F32)
              for i in n]
        for i in n:
            bi, pp, e = chains[i]
            upd = lax.dot_general(uv[i], bkh[i], (((0,), (0,)), ((), ())),
                                  preferred_element_type=F32)
            keep = jnp.logical_and(half2[e], (row2 >= HEAD) if e == 0 else (row2 < HEAD))
            state_ref[bi, 2 * pp + e] = jnp.where(keep, s0[i] * blk(w_end, chains[i])[0:1, :] + upd, 0.0)
        for bi in range(nb):
            for pp in range(N_HEAD // 2):
                i0 = chains.index((bi, pp, 0))
                both = jnp.where(half[1], yo[i0], yo[i0 + 1])
                y_scr[sub, bi * c:(bi + 1) * c, pp * slab:(pp + 1) * slab] = pltpu.roll(both, HEAD, 1)
        fill(2)

        y = y_scr[sub]
        yc = y - _seg_sum(y, bd) * (1.0 / HEAD)
        fill()
        var = _seg_sum(yc * yc, bd) * (1.0 / HEAD)
        fill(IN_COLS // PROJ_PIECE)
        yn = yc * lax.rsqrt(var + GN_EPS) * lnw_ref[...] + lnb_ref[...]
        out = ((yn + bonus) * g).astype(BF)
        for bi in range(nb):
            o_ref[bi, rows_out, :] = out[bi * c:(bi + 1) * c, :]

        xs = jnp.concatenate([x_ref[bi, rows_out, :] for bi in range(nb)], axis=0)
        h_scr[sub] = _rms(xs, gpre_ref[...]).astype(BF)

    for sub in range(nsub):
        chunk_step(sub)


def _mix(x3, g_pre, w_bf, gate_bias, prm):
    b, s, _ = x3.shape
    tt = RWKV_TILE
    nstep = s // tt
    const2 = lambda t: (0, 0)
    vec = pl.BlockSpec((1, W_B), const2)
    lora = DECAY_LORA + A_LORA
    rows = jnp.arange(b * CHUNK)
    tri = jnp.logical_and(rows[:, None] >= rows[None, :],
                          rows[:, None] // CHUNK == rows[None, :] // CHUNK).astype(BF)
    this = lambda t: (0, jnp.minimum(t, nstep - 1), 0)
    ahead = lambda t: (0, jnp.minimum(t + 1, nstep - 1), 0)
    qkv, gates, yb = pl.pallas_call(
        _mix_kernel,
        grid=(nstep + 1,),
        in_specs=[
            pl.BlockSpec((b, tt, D_MODEL), lambda t: (0, 0, 0)),
            pl.BlockSpec((b, tt, D_MODEL), ahead),
            pl.BlockSpec((1, D_MODEL), const2),
            pl.BlockSpec((D_MODEL, IN_COLS), const2),
            pl.BlockSpec((1, GATE_COLS), const2),
            pl.BlockSpec((1, RWKV_COLS), const2),
            vec,
            pl.BlockSpec((lora, W_B), const2),
            vec,
            pl.BlockSpec((lora, W_B), const2),
            pl.BlockSpec((GATE_LORA, W_B), const2),
            vec, vec, vec, vec, vec,
            pl.BlockSpec((SEG_W, SEG_W), const2),
            pl.BlockSpec((b * CHUNK, b * CHUNK), const2),
        ],
        out_specs=[
            pl.BlockSpec((b, tt, ATT_COLS), this),
            pl.BlockSpec((b, tt, GATE_COLS), this),
            pl.BlockSpec((b, tt, W_B), lambda t: (0, jnp.maximum(t - 1, 0), 0)),
        ],
        out_shape=[
            jax.ShapeDtypeStruct((b, s, ATT_COLS), BF),
            jax.ShapeDtypeStruct((b, s, GATE_COLS), BF),
            jax.ShapeDtypeStruct((b, s, W_B), BF),
        ],
        scratch_shapes=[pltpu.VMEM((tt // CHUNK, b * CHUNK, D_MODEL), BF),
                        pltpu.VMEM((2, tt // CHUNK, b * CHUNK, RWKV_COLS), F32),
                        pltpu.VMEM((b, 1, RWKV_COLS), F32),
                        pltpu.VMEM((b, N_HEAD, 2 * HEAD, 2 * HEAD), F32),
                        pltpu.VMEM((tt // CHUNK, b * CHUNK, W_B), F32)],
        compiler_params=pltpu.CompilerParams(
            dimension_semantics=("arbitrary",), vmem_limit_bytes=VMEM_LIMIT),
        name="inproj_rwkv7",
    )(x3, x3, g_pre, w_bf, gate_bias, prm["mu"], prm["w0"], prm["w2"], prm["a0"], prm["a2"],
      prm["g2"], prm["k_k"], prm["k_a"], prm["r_k"], prm["ln_w"], prm["ln_b"], prm["bd"], tri)
    n = b * s
    return qkv.reshape(n, ATT_COLS), gates.reshape(n, GATE_COLS), yb.reshape(n, W_B)


def _attn_kernel(q_ref, kp_ref, kc_ref, vp_ref, vc_ref, base_ref, o_ref, kwin, vwin, bias_scr):
    i = pl.program_id(1)

    @pl.when(jnp.logical_and(pl.program_id(0) == 0, i == 0))
    def _():
        qi = lax.broadcasted_iota(jnp.int32, (ATT_Q_SUB, ATT_K_WIN), 0)
        kj = lax.broadcasted_iota(jnp.int32, (ATT_Q_SUB, ATT_K_WIN), 1)
        dchunk = (BAND_PREV + qi // CHUNK) - kj // CHUNK
        band = jnp.logical_and(dchunk >= 0, dchunk <= BAND_PREV)
        for h in range(N_HEAD):
            rows = jnp.broadcast_to(base_ref[h:h + 1, :], (ATT_Q_SUB, ATT_BASE))
            toep = pltpu.roll(rows, 0, 1, stride=1, stride_axis=0)
            bias_scr[h] = jnp.where(band, toep[:, 0:ATT_K_WIN] * LOG2E, NEG_INF)

    kwin[0:ROW_TILE, :] = kp_ref[...]
    kwin[ROW_TILE:2 * ROW_TILE, :] = kc_ref[...]
    vwin[0:ROW_TILE, :] = vp_ref[...]
    vwin[ROW_TILE:2 * ROW_TILE, :] = vc_ref[...]
    lane = lax.broadcasted_iota(jnp.int32, (ATT_Q_SUB, 2 * HEAD), 1)
    col = lax.broadcasted_iota(jnp.int32, (ATT_Q_SUB, ATT_K_WIN), 1)

    def block(first):
        for j in range(ROW_TILE // ATT_Q_SUB):
            r0 = j * ATT_Q_SUB
            hs = range(N_HEAD)
            cs = [(h // 2) * 2 * HEAD for h in hs]
            ss = []
            for h in hs:
                q2 = q_ref[r0:r0 + ATT_Q_SUB, cs[h]:cs[h] + 2 * HEAD]
                in_head = (lane >= HEAD) if h % 2 else (lane < HEAD)
                qm = jnp.where(in_head, q2, jnp.zeros_like(q2))
                ss.append(lax.dot_general(qm, kwin[r0:r0 + ATT_K_WIN, cs[h]:cs[h] + 2 * HEAD],
                                          (((1,), (1,)), ((), ())), preferred_element_type=F32))
            exs, ls = [], []
            for h in hs:
                s = ss[h] + bias_scr[h]
                if first:
                    s = jnp.where(col >= ROW_TILE - r0, s, NEG_INF)
                m = jnp.max(s, axis=-1, keepdims=True)
                ex = jnp.exp2(s - m)
                ls.append(jnp.sum(ex, axis=-1, keepdims=True))
                exs.append(ex.astype(BF))
            outs = [jnp.dot(exs[h], vwin[r0:r0 + ATT_K_WIN, cs[h]:cs[h] + 2 * HEAD],
                            preferred_element_type=F32) / ls[h] for h in hs]
            for p in range(N_HEAD // 2):
                o = jnp.where(lane < HEAD, outs[2 * p], outs[2 * p + 1])
                o_ref[r0:r0 + ATT_Q_SUB, cs[2 * p]:cs[2 * p] + 2 * HEAD] = o.astype(BF)

    pl.when(i == 0)(lambda: block(True))
    pl.when(i > 0)(lambda: block(False))


def _attn_bias_base(rel_bias):
    pos = jnp.arange(ATT_BASE)
    d = jnp.where(pos < ATT_K_WIN, pos, pos - ATT_BASE)
    idx = jnp.clip(BAND_PREV * CHUNK - d, -(CHUNK - 1), REL_CLIP) + (CHUNK - 1)
    return rel_bias.astype(F32)[:, idx]


def _attention(qkv, bias_base, b, s):
    n = b * s
    nblk = s // ROW_TILE
    blk = (ROW_TILE, W_A)
    return pl.pallas_call(
        _attn_kernel,
        grid=(b, nblk),
        in_specs=[
            pl.BlockSpec(blk, lambda bi, i: (bi * nblk + i, 0)),
            pl.BlockSpec(blk, lambda bi, i: (bi * nblk + jnp.maximum(i - 1, 0), 1)),
            pl.BlockSpec(blk, lambda bi, i: (bi * nblk + i, 1)),
            pl.BlockSpec(blk, lambda bi, i: (bi * nblk + jnp.maximum(i - 1, 0), 2)),
            pl.BlockSpec(blk, lambda bi, i: (bi * nblk + i, 2)),
            pl.BlockSpec((N_HEAD, ATT_BASE), lambda bi, i: (0, 0)),
        ],
        out_specs=pl.BlockSpec(blk, lambda bi, i: (bi * nblk + i, 0)),
        out_shape=jax.ShapeDtypeStruct((n, W_A), BF),
        scratch_shapes=[pltpu.VMEM((2 * ROW_TILE, W_A), BF),
                        pltpu.VMEM((2 * ROW_TILE, W_A), BF),
                        pltpu.VMEM((N_HEAD, ATT_Q_SUB, ATT_K_WIN), F32)],
        compiler_params=pltpu.CompilerParams(
            dimension_semantics=("arbitrary", "arbitrary"), vmem_limit_bytes=VMEM_LIMIT),
        name="band_attn",
    )(qkv, qkv, qkv, qkv, qkv, bias_base)


FF_TILE = 1024


def _tail_kernel(ya_ref, yb_ref, gate_ref, x_ref, pa_ref, pb_ref, wo_ref, gmix_ref,
                 g1_ref, wu_ref, wd_ref, g2_ref, o_ref):
    half = ROW_TILE // 2
    rows = [slice(r0, r0 + half) for r0 in range(0, ROW_TILE, half)]
    z = []
    for rs in rows:
        ma = jnp.dot(ya_ref[rs, :], pa_ref[...], preferred_element_type=F32)
        mb = jnp.dot(yb_ref[rs, :], pb_ref[...], preferred_element_type=F32)
        merged = (gate_ref[rs, 0:D_MODEL].astype(F32) * ma
                  + gate_ref[rs, D_MODEL:2 * D_MODEL].astype(F32) * mb)
        z.append(jnp.dot(merged.astype(BF), wo_ref[...], preferred_element_type=F32))
    x = [x_ref[rs, :] + _rms(z[i], gmix_ref[...]) for i, rs in enumerate(rows)]
    hf = [_rms(xi, g1_ref[...]).astype(BF) for xi in x]
    acc = [jnp.zeros(xi.shape, F32) for xi in x]
    for c in range(0, D_FF, FF_TILE):
        u = [jnp.dot(h, wu_ref[:, c:c + FF_TILE], preferred_element_type=F32) for h in hf]
        u = [jnp.maximum(ui, 0.0) for ui in u]
        u = [(ui * ui).astype(BF) for ui in u]
        acc = [a + jnp.dot(ui, wd_ref[c:c + FF_TILE, :], preferred_element_type=F32)
               for a, ui in zip(acc, u)]
    for i, rs in enumerate(rows):
        o_ref[rs, :] = x[i] + _rms(acc[i], g2_ref[...])


def _tail(ya, yb, gates, x2, pa, pb, wo, gmix, g1, wu, wd, g2):
    n = x2.shape[0]
    row = lambda i: (i, 0)

    def resident(shape):
        return pl.BlockSpec(shape, lambda i: (0, 0), pipeline_mode=pl.Buffered(1))

    return pl.pallas_call(
        _tail_kernel,
        grid=(n // ROW_TILE,),
        in_specs=[
            pl.BlockSpec((ROW_TILE, W_A), row),
            pl.BlockSpec((ROW_TILE, W_B), row),
            pl.BlockSpec((ROW_TILE, GATE_COLS), row),
            pl.BlockSpec((ROW_TILE, D_MODEL), row),
            resident((W_A, D_MODEL)),
            resident((W_B, D_MODEL)),
            resident((D_MODEL, D_MODEL)),
            resident((1, D_MODEL)),
            resident((1, D_MODEL)),
            resident((D_MODEL, D_FF)),
            resident((D_FF, D_MODEL)),
            resident((1, D_MODEL)),
        ],
        out_specs=pl.BlockSpec((ROW_TILE, D_MODEL), row),
        out_shape=jax.ShapeDtypeStruct((n, D_MODEL), F32),
        compiler_params=pltpu.CompilerParams(
            dimension_semantics=("arbitrary",), vmem_limit_bytes=VMEM_LIMIT),
        name="merge_ffn",
    )(ya, yb, gates, x2, pa, pb, wo, gmix, g1, wu, wd, g2)


def _layer(x2, b, s, lp):
    row = lambda a: a.reshape(1, -1).astype(F32)
    zeros_lora = jnp.zeros((DECAY_LORA, W_B), BF)
    blk = jnp.arange(SEG_W) // HEAD
    prm = {
        "mu": row(lp["shift_mu"]),
        "w0": row(lp["w0"]),
        "w2": jnp.concatenate([lp["w2"].astype(BF), zeros_lora], axis=0),
        "a0": row(lp["a0"]),
        "a2": jnp.concatenate([zeros_lora, lp["a2"].astype(BF)], axis=0),
        "g2": lp["g2"].astype(BF),
        "k_k": row(lp["k_k"]),
        "k_a": row(lp["k_a"]),
        "r_k": row(lp["r_k"]),
        "ln_w": row(lp["ln_x_w"]),
        "ln_b": row(lp["ln_x_b"]),
        "bd": (blk[:, None] == blk[None, :]).astype(BF),
    }
    qkv, gates, yb = _mix(x2.reshape(b, s, D_MODEL), row(lp["pre_mix_g"]),
                          lp["w_in"].astype(BF), row(lp["gate_bias"]), prm)
    ya = _attention(qkv, _attn_bias_base(lp["rel_bias"]), b, s)

    return _tail(ya, yb, gates, x2, lp["proj_a"].astype(BF), lp["proj_b"].astype(BF),
                 lp["w_out"].astype(BF), row(lp["post_mix_g"]), row(lp["pre_ffn_g"]),
                 lp["w_up"].astype(BF), lp["w_down"].astype(BF), row(lp["post_ffn_g"]))


@jax.jit
def _forward(x, params):
    b, s, d = x.shape
    assert d == D_MODEL and s % ROW_TILE == 0 and s % RWKV_TILE == 0
    x2 = x.reshape(b * s, d)
    depth = params["w_in"].shape[0]
    for l in range(depth):
        lp = {name: val[l] for name, val in params.items()}
        x2 = _layer(x2, b, s, lp)
    return x2.reshape(b, s, d)


def kernel(x, pre_mix_g, w_in, gate_bias, rel_bias, shift_mu, w0, w2, a0, a2, g2, k_k, k_a, r_k, ln_x_w, ln_x_b, proj_a, proj_b, w_out, post_mix_g, pre_ffn_g, w_up, w_down, post_ffn_g):
    params = dict(pre_mix_g=pre_mix_g, w_in=w_in, gate_bias=gate_bias, rel_bias=rel_bias,
                  shift_mu=shift_mu, w0=w0, w2=w2, a0=a0, a2=a2, g2=g2, k_k=k_k, k_a=k_a,
                  r_k=r_k, ln_x_w=ln_x_w, ln_x_b=ln_x_b, proj_a=proj_a, proj_b=proj_b,
                  w_out=w_out, post_mix_g=post_mix_g, pre_ffn_g=pre_ffn_g, w_up=w_up,
                  w_down=w_down, post_ffn_g=post_ffn_g)
    return _forward(x, params)
```

```python
import math

import jax
import jax.numpy as jnp
from jax import lax
from jax.experimental import pallas as pl
from jax.experimental.pallas import tpu as pltpu

BF = jnp.bfloat16
F32 = jnp.float32

D_MODEL = 1024
CHUNK = 64
BAND_PREV = 8
REL_CLIP = 256
W_A = D_MODEL // 2
HEAD = 64
N_HEAD = W_A // HEAD
W_B = D_MODEL // 2
DECAY_LORA = 64
A_LORA = 64
GATE_LORA = 128
D_FF = 4 * D_MODEL
ATT_COLS = 3 * W_A
RWKV_COLS = 3 * W_B + DECAY_LORA + A_LORA + GATE_LORA
GATE_COLS = 2 * D_MODEL
IN_COLS = ATT_COLS + RWKV_COLS + GATE_COLS
RMS_EPS = 1e-6
GN_EPS = HEAD * 1e-5
NEG_INF = -1e30
LOG2E = math.log2(math.e)

ROW_TILE = 512
ATT_Q_SUB = 128
ATT_K_WIN = ATT_Q_SUB + BAND_PREV * CHUNK
ATT_BASE = ATT_K_WIN + ATT_Q_SUB
ATT_HEAD_GROUP = 4
SEG_W = 256
PROJ_PIECE = 256
RWKV_TILE = 256
VMEM_LIMIT = 56 * 1024 * 1024


def _mm(a, b):
    return jnp.dot(a.astype(BF), b.astype(BF), preferred_element_type=F32)


def _mm_nt(a, b):
    return lax.dot_general(a.astype(BF), b.astype(BF), (((1,), (1,)), ((), ())),
                           preferred_element_type=F32)


def _rms(x, g):
    ms = jnp.mean(x * x, axis=-1, keepdims=True)
    return x * lax.rsqrt(ms + RMS_EPS) * g


def _split2(x):
    hi = x.astype(BF)
    lo = (x - hi.astype(F32)).astype(BF)
    return hi, lo


def _seg_sum(x, bd):
    w = bd.shape[0]
    xb = x.astype(BF)
    return jnp.concatenate(
        [jnp.dot(xb[:, j:j + w], bd, preferred_element_type=F32)
         for j in range(0, x.shape[1], w)], axis=1)


def _mix_kernel(x0_ref, x_ref, gpre_ref, w_ref, gbias_ref, mu_ref, w0_ref, w2_ref, a0_ref, a2_ref,
                g2_ref, kk_ref, ka_ref, rk_ref, lnw_ref, lnb_ref, bd_ref, tri_ref,
                qkv_ref, gate_ref, o_ref, h_scr, p_scr, carry_ref, state_ref, y_scr):
    t = pl.program_id(0)
    nb = x_ref.shape[0]
    c = CHUNK
    nsub = x_ref.shape[1] // c
    slab = 2 * HEAD
    cur = (t + 1) % 2
    nxt = t % 2

    @pl.when(t == 0)
    def _():
        carry_ref[...] = jnp.zeros_like(carry_ref)
        state_ref[...] = jnp.zeros_like(state_ref)
        p_scr[1] = jnp.zeros(p_scr.shape[1:], F32)
        for sub in range(nsub):
            x0 = jnp.concatenate([x0_ref[bi, sub * c:(sub + 1) * c, :] for bi in range(nb)], axis=0)
            h_scr[sub] = _rms(x0, gpre_ref[...]).astype(BF)

    def chunk_step(sub):
        rows_out = slice(sub * c, (sub + 1) * c)
        h = h_scr[sub]
        row_w = lax.broadcasted_iota(jnp.int32, (nb * c, PROJ_PIECE), 0)

        def proj_piece(c0):
            c1 = min(c0 + PROJ_PIECE, ATT_COLS if c0 < ATT_COLS else
                     ATT_COLS + RWKV_COLS if c0 < ATT_COLS + RWKV_COLS else IN_COLS)

            def run():
                acc = jnp.dot(h, w_ref[:, c0:c1], preferred_element_type=F32)
                if c0 < ATT_COLS:
                    if c0 < W_A:
                        acc = acc * (HEAD ** -0.5 * LOG2E)
                    out = acc.astype(BF)
                    for bi in range(nb):
                        qkv_ref[bi, rows_out, c0:c1] = out[bi * c:(bi + 1) * c, :]
                elif c0 < ATT_COLS + RWKV_COLS:
                    cols = slice(c0 - ATT_COLS, c1 - ATT_COLS)
                    prev = pltpu.roll(acc, 1, 0)
                    for bi in range(nb):
                        prev = jnp.where(row_w[:, 0:c1 - c0] == bi * c,
                                         jnp.broadcast_to(carry_ref[bi, :, cols], acc.shape), prev)
                        carry_ref[bi, :, cols] = acc[(bi + 1) * c - 1:(bi + 1) * c, :]
                    p_scr[nxt, sub, :, cols] = acc + (prev - acc) * mu_ref[:, cols]
                else:
                    g0 = c0 - ATT_COLS - RWKV_COLS
                    out = jax.nn.sigmoid(acc + gbias_ref[:, g0:g0 + c1 - c0]).astype(BF)
                    for bi in range(nb):
                        gate_ref[bi, rows_out, g0:g0 + c1 - c0] = out[bi * c:(bi + 1) * c, :]
            return run, c1

        pieces = []
        col = 0
        while col < IN_COLS:
            run, col = proj_piece(col)
            pieces.append(run)
        pieces = iter(pieces)

        def fill(count=1):
            for _ in range(count):
                run = next(pieces, None)
                if run is not None:
                    run()

        fill()

        ps = p_scr[cur, sub]
        r = ps[:, 0:W_B]
        k = ps[:, W_B:2 * W_B]
        v = ps[:, 2 * W_B:3 * W_B]
        lora_in = ps[:, 3 * W_B:3 * W_B + DECAY_LORA + A_LORA]
        cg = ps[:, 3 * W_B + DECAY_LORA + A_LORA:]

        u = w0_ref[...] + _mm(jnp.tanh(lora_in), w2_ref[...])
        lw = -(math.exp(-0.5) * LOG2E) * jax.nn.sigmoid(u)
        a = jax.nn.sigmoid(a0_ref[...] + _mm(lora_in, a2_ref[...]))
        g = _mm(jax.nn.sigmoid(cg), g2_ref[...])
        fill()

        bd = bd_ref[...]
        kk = k * kk_ref[...]
        kk = kk * lax.rsqrt(jnp.maximum(_seg_sum(kk * kk, bd), 1e-24))
        fill()
        k = k * (1.0 + (a - 1.0) * ka_ref[...])
        aa = -kk
        bb = kk * a

        tri = tri_ref[...]
        cum = sum(jnp.dot(tri, part, preferred_element_type=F32) for part in _split2(lw))
        fill()
        last = jnp.concatenate(
            [jnp.broadcast_to(cum[(bi + 1) * c - 1:(bi + 1) * c, :], (c, W_B)) for bi in range(nb)],
            axis=0)
        inv = jnp.exp2(-cum)
        w_end = jnp.exp2(last)
        tail = w_end * inv
        at = aa * jnp.exp2(cum - lw)
        bt = bb * inv
        kt = k * inv
        rt = r * jnp.exp2(cum)
        bh = bb * tail
        kh = k * tail
        bonus = _seg_sum(r * k * rk_ref[...], bd) * v

        lane = lax.broadcasted_iota(jnp.int32, (c, slab), 1)
        lane2 = lax.broadcasted_iota(jnp.int32, (2 * c, slab), 1)
        row2 = lax.broadcasted_iota(jnp.int32, (2 * c, slab), 0)
        half = [lane < HEAD, lane >= HEAD]
        half2 = [lane2 < HEAD, lane2 >= HEAD]
        rl = row2 & (c - 1)
        cl = lane2 & (c - 1)
        mask4 = jnp.logical_or(rl > cl, jnp.logical_and(row2 >= c, rl == cl))

        chains = [(bi, pp, e) for bi in range(nb) for pp in range(N_HEAD // 2) for e in range(2)]
        n = range(len(chains))

        def blk(x, ch):
            bi, pp, _ = ch
            return x[bi * c:(bi + 1) * c, pp * slab:(pp + 1) * slab]

        ar = [jnp.concatenate([blk(at, ch), blk(rt, ch)], axis=0) for ch in chains]
        bk = [jnp.concatenate([blk(bt, ch), blk(kt, ch)], axis=0).astype(BF) for ch in chains]
        bkh = [jnp.concatenate([blk(bh, ch), blk(kh, ch)], axis=0).astype(BF) for ch in chains]
        vsw = [pltpu.roll(blk(v, ch), HEAD, 1) for ch in chains]
        vx = [jnp.where(half[1 - chains[i][2]], vsw[i], 0.0).astype(BF) for i in n]
        a4 = [jnp.where(mask4,
                        _mm_nt(jnp.where(half2[chains[i][2]], ar[i], 0.0), bk[i]), 0.0).astype(BF)
              for i in n]
        ta = [jnp.where(half[chains[i][2]], blk(at, chains[i]), 0.0) for i in n]
        pw = [a4[i][0:c, :] for i in n]
        zeros_half = jnp.zeros((c, slab), BF)
        n_step = int(math.log2(c))
        for step in range(n_step):
            fill()
            if step + 1 < n_step:
                rhs = [jnp.concatenate(
                    [jnp.concatenate([ta[i].astype(BF), pw[i]], axis=1),
                     jnp.concatenate([vx[i], zeros_half], axis=1)], axis=0) for i in n]
            else:
                rhs = [jnp.concatenate([ta[i].astype(BF), vx[i]], axis=0) for i in n]
            both = [jnp.dot(pw[i], rhs[i], preferred_element_type=F32) for i in n]
            ta = [ta[i] + both[i][:, 0:slab] for i in n]
            if step + 1 < n_step:
                pw = [both[i][:, slab:2 * slab].astype(BF) for i in n]
        gm = [jnp.concatenate(
            [jnp.where(half[chains[i][2]], blk(rt, chains[i]), 0.0), ta[i]], axis=0) for i in n]
        fill(2)
        s0 = [state_ref[ch[0], 2 * ch[1] + ch[2]] for ch in chains]
        ys = [_mm_nt(gm[i], s0[i]) for i in n]
        uv = [jnp.concatenate([(ys[i][c:2 * c, :] + ta[i]).astype(BF), vx[i]], axis=0) for i in n]
        fill(2)
        yo = [ys[i][0:c, :] + jnp.dot(a4[i][c:2 * c, :], uv[i], preferred_element_type=F32)
              for i in n]
        for i in n:
            bi, pp, e = chains[i]
            upd = lax.dot_general(uv[i], bkh[i], (((0,), (0,)), ((), ())),
                                  preferred_element_type=F32)
            keep = jnp.logical_and(half2[e], (row2 >= HEAD) if e == 0 else (row2 < HEAD))
            state_ref[bi, 2 * pp + e] = jnp.where(keep, s0[i] * blk(w_end, chains[i])[0:1, :] + upd, 0.0)
        for bi in range(nb):
            for pp in range(N_HEAD // 2):
                i0 = chains.index((bi, pp, 0))
                both = jnp.where(half[1], yo[i0], yo[i0 + 1])
                y_scr[sub, bi * c:(bi + 1) * c, pp * slab:(pp + 1) * slab] = pltpu.roll(both, HEAD, 1)
        fill(2)

        y = y_scr[sub]
        yc = y - _seg_sum(y, bd) * (1.0 / HEAD)
        fill()
        var = _seg_sum(yc * yc, bd) * (1.0 / HEAD)
        fill(IN_COLS // PROJ_PIECE)
        yn = yc * lax.rsqrt(var + GN_EPS) * lnw_ref[...] + lnb_ref[...]
        out = ((yn + bonus) * g).astype(BF)
        for bi in range(nb):
            o_ref[bi, rows_out, :] = out[bi * c:(bi + 1) * c, :]

        xs = jnp.concatenate([x_ref[bi, rows_out, :] for bi in range(nb)], axis=0)
        h_scr[sub] = _rms(xs, gpre_ref[...]).astype(BF)

    for sub in range(nsub):
        chunk_step(sub)


def _mix(x3, g_pre, w_bf, gate_bias, prm):
    b, s, _ = x3.shape
    tt = RWKV_TILE
    nstep = s // tt
    const2 = lambda t: (0, 0)
    vec = pl.BlockSpec((1, W_B), const2)
    lora = DECAY_LORA + A_LORA
    rows = jnp.arange(b * CHUNK)
    tri = jnp.logical_and(rows[:, None] >= rows[None, :],
                          rows[:, None] // CHUNK == rows[None, :] // CHUNK).astype(BF)
    this = lambda t: (0, jnp.minimum(t, nstep - 1), 0)
    ahead = lambda t: (0, jnp.minimum(t + 1, nstep - 1), 0)
    qkv, gates, yb = pl.pallas_call(
        _mix_kernel,
        grid=(nstep + 1,),
        in_specs=[
            pl.BlockSpec((b, tt, D_MODEL), lambda t: (0, 0, 0)),
            pl.BlockSpec((b, tt, D_MODEL), ahead),
            pl.BlockSpec((1, D_MODEL), const2),
            pl.BlockSpec((D_MODEL, IN_COLS), const2),
            pl.BlockSpec((1, GATE_COLS), const2),
            pl.BlockSpec((1, RWKV_COLS), const2),
            vec,
            pl.BlockSpec((lora, W_B), const2),
            vec,
            pl.BlockSpec((lora, W_B), const2),
            pl.BlockSpec((GATE_LORA, W_B), const2),
            vec, vec, vec, vec, vec,
            pl.BlockSpec((SEG_W, SEG_W), const2),
            pl.BlockSpec((b * CHUNK, b * CHUNK), const2),
        ],
        out_specs=[
            pl.BlockSpec((b, tt, ATT_COLS), this),
            pl.BlockSpec((b, tt, GATE_COLS), this),
            pl.BlockSpec((b, tt, W_B), lambda t: (0, jnp.maximum(t - 1, 0), 0)),
        ],
        out_shape=[
            jax.ShapeDtypeStruct((b, s, ATT_COLS), BF),
            jax.ShapeDtypeStruct((b, s, GATE_COLS), BF),
            jax.ShapeDtypeStruct((b, s, W_B), BF),
        ],
        scratch_shapes=[pltpu.VMEM((tt // CHUNK, b * CHUNK, D_MODEL), BF),
                        pltpu.VMEM((2, tt // CHUNK, b * CHUNK, RWKV_COLS), F32),
                        pltpu.VMEM((b, 1, RWKV_COLS), F32),
                        pltpu.VMEM((b, N_HEAD, 2 * HEAD, 2 * HEAD), F32),
                        pltpu.VMEM((tt // CHUNK, b * CHUNK, W_B), F32)],
        compiler_params=pltpu.CompilerParams(
            dimension_semantics=("arbitrary",), vmem_limit_bytes=VMEM_LIMIT),
        name="inproj_rwkv7",
    )(x3, x3, g_pre, w_bf, gate_bias, prm["mu"], prm["w0"], prm["w2"], prm["a0"], prm["a2"],
      prm["g2"], prm["k_k"], prm["k_a"], prm["r_k"], prm["ln_w"], prm["ln_b"], prm["bd"], tri)
    n = b * s
    return qkv.reshape(n, ATT_COLS), gates.reshape(n, GATE_COLS), yb.reshape(n, W_B)


def _attn_kernel(q_ref, kp_ref, kc_ref, vp_ref, vc_ref, base_ref, o_ref, kwin, vwin, bias_scr):
    i = pl.program_id(1)

    @pl.when(jnp.logical_and(pl.program_id(0) == 0, i == 0))
    def _():
        qi = lax.broadcasted_iota(jnp.int32, (ATT_Q_SUB, ATT_K_WIN), 0)
        kj = lax.broadcasted_iota(jnp.int32, (ATT_Q_SUB, ATT_K_WIN), 1)
        dchunk = (BAND_PREV + qi // CHUNK) - kj // CHUNK
        band = jnp.logical_and(dchunk >= 0, dchunk <= BAND_PREV)
        for h in range(N_HEAD):
            rows = jnp.broadcast_to(base_ref[h:h + 1, :], (ATT_Q_SUB, ATT_BASE))
            toep = pltpu.roll(rows, 0, 1, stride=1, stride_axis=0)
            bias_scr[h] = jnp.where(band, toep[:, 0:ATT_K_WIN] * LOG2E, NEG_INF)

    kwin[0:ROW_TILE, :] = kp_ref[...]
    kwin[ROW_TILE:2 * ROW_TILE, :] = kc_ref[...]
    vwin[0:ROW_TILE, :] = vp_ref[...]
    vwin[ROW_TILE:2 * ROW_TILE, :] = vc_ref[...]
    lane = lax.broadcasted_iota(jnp.int32, (ATT_Q_SUB, 2 * HEAD), 1)
    col = lax.broadcasted_iota(jnp.int32, (ATT_Q_SUB, ATT_K_WIN), 1)

    def block(first):
        for j in range(ROW_TILE // ATT_Q_SUB):
            r0 = j * ATT_Q_SUB
            for h0 in range(0, N_HEAD, ATT_HEAD_GROUP):
                hs = range(h0, h0 + ATT_HEAD_GROUP)
                cs = {h: (h // 2) * 2 * HEAD for h in hs}
                ss = {}
                for h in hs:
                    q2 = q_ref[r0:r0 + ATT_Q_SUB, cs[h]:cs[h] + 2 * HEAD]
                    in_head = (lane >= HEAD) if h % 2 else (lane < HEAD)
                    qm = jnp.where(in_head, q2, jnp.zeros_like(q2))
                    ss[h] = lax.dot_general(qm, kwin[r0:r0 + ATT_K_WIN, cs[h]:cs[h] + 2 * HEAD],
                                            (((1,), (1,)), ((), ())), preferred_element_type=F32)
                exs, ls = {}, {}
                for h in hs:
                    s = ss[h] + bias_scr[h]
                    if first:
                        s = jnp.where(col >= ROW_TILE - r0, s, NEG_INF)
                    m = jnp.max(s, axis=-1, keepdims=True)
                    ex = jnp.exp2(s - m)
                    ls[h] = jnp.sum(ex, axis=-1, keepdims=True)
                    exs[h] = ex.astype(BF)
                outs = {h: jnp.dot(exs[h], vwin[r0:r0 + ATT_K_WIN, cs[h]:cs[h] + 2 * HEAD],
                                   preferred_element_type=F32) / ls[h] for h in hs}
                for h in range(h0, h0 + ATT_HEAD_GROUP, 2):
                    o = jnp.where(lane < HEAD, outs[h], outs[h + 1])
                    o_ref[r0:r0 + ATT_Q_SUB, cs[h]:cs[h] + 2 * HEAD] = o.astype(BF)

    pl.when(i == 0)(lambda: block(True))
    pl.when(i > 0)(lambda: block(False))


def _attn_bias_base(rel_bias):
    pos = jnp.arange(ATT_BASE)
    d = jnp.where(pos < ATT_K_WIN, pos, pos - ATT_BASE)
    idx = jnp.clip(BAND_PREV * CHUNK - d, -(CHUNK - 1), REL_CLIP) + (CHUNK - 1)
    return rel_bias.astype(F32)[:, idx]


def _attention(qkv, bias_base, b, s):
    n = b * s
    nblk = s // ROW_TILE
    blk = (ROW_TILE, W_A)
    return pl.pallas_call(
        _attn_kernel,
        grid=(b, nblk),
        in_specs=[
            pl.BlockSpec(blk, lambda bi, i: (bi * nblk + i, 0)),
            pl.BlockSpec(blk, lambda bi, i: (bi * nblk + jnp.maximum(i - 1, 0), 1)),
            pl.BlockSpec(blk, lambda bi, i: (bi * nblk + i, 1)),
            pl.BlockSpec(blk, lambda bi, i: (bi * nblk + jnp.maximum(i - 1, 0), 2)),
            pl.BlockSpec(blk, lambda bi, i: (bi * nblk + i, 2)),
            pl.BlockSpec((N_HEAD, ATT_BASE), lambda bi, i: (0, 0)),
        ],
        out_specs=pl.BlockSpec(blk, lambda bi, i: (bi * nblk + i, 0)),
        out_shape=jax.ShapeDtypeStruct((n, W_A), BF),
        scratch_shapes=[pltpu.VMEM((2 * ROW_TILE, W_A), BF),
                        pltpu.VMEM((2 * ROW_TILE, W_A), BF),
                        pltpu.VMEM((N_HEAD, ATT_Q_SUB, ATT_K_WIN), F32)],
        compiler_params=pltpu.CompilerParams(
            dimension_semantics=("arbitrary", "arbitrary"), vmem_limit_bytes=VMEM_LIMIT),
        name="band_attn",
    )(qkv, qkv, qkv, qkv, qkv, bias_base)


FF_TILE = 1024


def _tail_kernel(ya_ref, yb_ref, gate_ref, x_ref, pa_ref, pb_ref, wo_ref, gmix_ref,
                 g1_ref, wu_ref, wd_ref, g2_ref, o_ref):
    half = ROW_TILE // 2
    rows = [slice(r0, r0 + half) for r0 in range(0, ROW_TILE, half)]
    z = []
    for rs in rows:
        ma = jnp.dot(ya_ref[rs, :], pa_ref[...], preferred_element_type=F32)
        mb = jnp.dot(yb_ref[rs, :], pb_ref[...], preferred_element_type=F32)
        merged = (gate_ref[rs, 0:D_MODEL].astype(F32) * ma
                  + gate_ref[rs, D_MODEL:2 * D_MODEL].astype(F32) * mb)
        z.append(jnp.dot(merged.astype(BF), wo_ref[...], preferred_element_type=F32))
    x = [x_ref[rs, :] + _rms(z[i], gmix_ref[...]) for i, rs in enumerate(rows)]
    hf = [_rms(xi, g1_ref[...]).astype(BF) for xi in x]
    acc = [jnp.zeros(xi.shape, F32) for xi in x]
    for c in range(0, D_FF, FF_TILE):
        u = [jnp.dot(h, wu_ref[:, c:c + FF_TILE], preferred_element_type=F32) for h in hf]
        u = [jnp.maximum(ui, 0.0) for ui in u]
        u = [(ui * ui).astype(BF) for ui in u]
        acc = [a + jnp.dot(ui, wd_ref[c:c + FF_TILE, :], preferred_element_type=F32)
               for a, ui in zip(acc, u)]
    for i, rs in enumerate(rows):
        o_ref[rs, :] = x[i] + _rms(acc[i], g2_ref[...])


def _tail(ya, yb, gates, x2, pa, pb, wo, gmix, g1, wu, wd, g2):
    n = x2.shape[0]
    row = lambda i: (i, 0)

    def resident(shape):
        return pl.BlockSpec(shape, lambda i: (0, 0), pipeline_mode=pl.Buffered(1))

    return pl.pallas_call(
        _tail_kernel,
        grid=(n // ROW_TILE,),
        in_specs=[
            pl.BlockSpec((ROW_TILE, W_A), row),
            pl.BlockSpec((ROW_TILE, W_B), row),
            pl.BlockSpec((ROW_TILE, GATE_COLS), row),
            pl.BlockSpec((ROW_TILE, D_MODEL), row),
            resident((W_A, D_MODEL)),
            resident((W_B, D_MODEL)),
            resident((D_MODEL, D_MODEL)),
            resident((1, D_MODEL)),
            resident((1, D_MODEL)),
            resident((D_MODEL, D_FF)),
            resident((D_FF, D_MODEL)),
            resident((1, D_MODEL)),
        ],
        out_specs=pl.BlockSpec((ROW_TILE, D_MODEL), row),
        out_shape=jax.ShapeDtypeStruct((n, D_MODEL), F32),
        compiler_params=pltpu.CompilerParams(
            dimension_semantics=("arbitrary",), vmem_limit_bytes=VMEM_LIMIT),
        name="merge_ffn",
    )(ya, yb, gates, x2, pa, pb, wo, gmix, g1, wu, wd, g2)


def _layer(x2, b, s, lp):
    row = lambda a: a.reshape(1, -1).astype(F32)
    zeros_lora = jnp.zeros((DECAY_LORA, W_B), BF)
    blk = jnp.arange(SEG_W) // HEAD
    prm = {
        "mu": row(lp["shift_mu"]),
        "w0": row(lp["w0"]),
        "w2": jnp.concatenate([lp["w2"].astype(BF), zeros_lora], axis=0),
        "a0": row(lp["a0"]),
        "a2": jnp.concatenate([zeros_lora, lp["a2"].astype(BF)], axis=0),
        "g2": lp["g2"].astype(BF),
        "k_k": row(lp["k_k"]),
        "k_a": row(lp["k_a"]),
        "r_k": row(lp["r_k"]),
        "ln_w": row(lp["ln_x_w"]),
        "ln_b": row(lp["ln_x_b"]),
        "bd": (blk[:, None] == blk[None, :]).astype(BF),
    }
    qkv, gates, yb = _mix(x2.reshape(b, s, D_MODEL), row(lp["pre_mix_g"]),
                          lp["w_in"].astype(BF), row(lp["gate_bias"]), prm)
    ya = _attention(qkv, _attn_bias_base(lp["rel_bias"]), b, s)

    return _tail(ya, yb, gates, x2, lp["proj_a"].astype(BF), lp["proj_b"].astype(BF),
                 lp["w_out"].astype(BF), row(lp["post_mix_g"]), row(lp["pre_ffn_g"]),
                 lp["w_up"].astype(BF), lp["w_down"].astype(BF), row(lp["post_ffn_g"]))


@jax.jit
def _forward(x, params):
    b, s, d = x.shape
    assert d == D_MODEL and s % ROW_TILE == 0 and s % RWKV_TILE == 0
    x2 = x.reshape(b * s, d)
    depth = params["w_in"].shape[0]
    for l in range(depth):
        lp = {name: val[l] for name, val in params.items()}
        x2 = _layer(x2, b, s, lp)
    return x2.reshape(b, s, d)


def kernel(x, pre_mix_g, w_in, gate_bias, rel_bias, shift_mu, w0, w2, a0, a2, g2, k_k, k_a, r_k, ln_x_w, ln_x_b, proj_a, proj_b, w_out, post_mix_g, pre_ffn_g, w_up, w_down, post_ffn_g):
    params = dict(pre_mix_g=pre_mix_g, w_in=w_in, gate_bias=gate_bias, rel_bias=rel_bias,
                  shift_mu=shift_mu, w0=w0, w2=w2, a0=a0, a2=a2, g2=g2, k_k=k_k, k_a=k_a,
                  r_k=r_k, ln_x_w=ln_x_w, ln_x_b=ln_x_b, proj_a=proj_a, proj_b=proj_b,
                  w_out=w_out, post_mix_g=post_mix_g, pre_ffn_g=pre_ffn_g, w_up=w_up,
                  w_down=w_down, post_ffn_g=post_ffn_g)
    return _forward(x, params)
```

```python
import math

import jax
import jax.numpy as jnp
from jax import lax
from jax.experimental import pallas as pl
from jax.experimental.pallas import tpu as pltpu

BF = jnp.bfloat16
F32 = jnp.float32

D_MODEL = 1024
CHUNK = 64
BAND_PREV = 8
REL_CLIP = 256
W_A = D_MODEL // 2
HEAD = 64
N_HEAD = W_A // HEAD
W_B = D_MODEL // 2
DECAY_LORA = 64
A_LORA = 64
GATE_LORA = 128
D_FF = 4 * D_MODEL
ATT_COLS = 3 * W_A
RWKV_COLS = 3 * W_B + DECAY_LORA + A_LORA + GATE_LORA
GATE_COLS = 2 * D_MODEL
IN_COLS = ATT_COLS + RWKV_COLS + GATE_COLS
RMS_EPS = 1e-6
GN_EPS = HEAD * 1e-5
NEG_INF = -1e30
LOG2E = math.log2(math.e)

ROW_TILE = 512
ATT_Q_SUB = 128
ATT_K_WIN = ATT_Q_SUB + BAND_PREV * CHUNK
ATT_BASE = ATT_K_WIN + ATT_Q_SUB
SEG_W = 256
PROJ_PIECE = 256
RWKV_TILE = 256
VMEM_LIMIT = 56 * 1024 * 1024


def _mm(a, b):
    return jnp.dot(a.astype(BF), b.astype(BF), preferred_element_type=F32)


def _mm_nt(a, b):
    return lax.dot_general(a.astype(BF), b.astype(BF), (((1,), (1,)), ((), ())),
                           preferred_element_type=F32)


def _sigmoid(x):
    return 0.5 * jnp.tanh(0.5 * x) + 0.5


def _rms(x, g):
    ms = jnp.mean(x * x, axis=-1, keepdims=True)
    return x * lax.rsqrt(ms + RMS_EPS) * g


def _split2(x):
    hi = x.astype(BF)
    lo = (x - hi.astype(F32)).astype(BF)
    return hi, lo


def _seg_sum(x, bd):
    w = bd.shape[0]
    xb = x.astype(BF)
    return jnp.concatenate(
        [jnp.dot(xb[:, j:j + w], bd, preferred_element_type=F32)
         for j in range(0, x.shape[1], w)], axis=1)


def _mix_kernel(x0_ref, x_ref, gpre_ref, w_ref, gbias_ref, mu_ref, w0_ref, w2_ref, a0_ref, a2_ref,
                g2_ref, kk_ref, ka_ref, rk_ref, lnw_ref, lnb_ref, bd_ref, tri_ref,
                qkv_ref, gate_ref, o_ref, h_scr, p_scr, carry_ref, state_ref, y_scr):
    t = pl.program_id(0)
    nb = x_ref.shape[0]
    c = CHUNK
    nsub = x_ref.shape[1] // c
    slab = 2 * HEAD
    cur = (t + 1) % 2
    nxt = t % 2

    @pl.when(t == 0)
    def _():
        carry_ref[...] = jnp.zeros_like(carry_ref)
        state_ref[...] = jnp.zeros_like(state_ref)
        p_scr[1] = jnp.zeros(p_scr.shape[1:], F32)
        for sub in range(nsub):
            x0 = jnp.concatenate([x0_ref[bi, sub * c:(sub + 1) * c, :] for bi in range(nb)], axis=0)
            h_scr[sub] = _rms(x0, gpre_ref[...]).astype(BF)

    def chunk_step(sub):
        rows_out = slice(sub * c, (sub + 1) * c)
        h = h_scr[sub]
        row_w = lax.broadcasted_iota(jnp.int32, (nb * c, PROJ_PIECE), 0)

        def proj_piece(c0):
            c1 = min(c0 + PROJ_PIECE, ATT_COLS if c0 < ATT_COLS else
                     ATT_COLS + RWKV_COLS if c0 < ATT_COLS + RWKV_COLS else IN_COLS)

            def run():
                acc = jnp.dot(h, w_ref[:, c0:c1], preferred_element_type=F32)
                if c0 < ATT_COLS:
                    if c0 < W_A:
                        acc = acc * (HEAD ** -0.5 * LOG2E)
                    out = acc.astype(BF)
                    for bi in range(nb):
                        qkv_ref[bi, rows_out, c0:c1] = out[bi * c:(bi + 1) * c, :]
                elif c0 < ATT_COLS + RWKV_COLS:
                    cols = slice(c0 - ATT_COLS, c1 - ATT_COLS)
                    prev = pltpu.roll(acc, 1, 0)
                    for bi in range(nb):
                        prev = jnp.where(row_w[:, 0:c1 - c0] == bi * c,
                                         jnp.broadcast_to(carry_ref[bi, :, cols], acc.shape), prev)
                        carry_ref[bi, :, cols] = acc[(bi + 1) * c - 1:(bi + 1) * c, :]
                    p_scr[nxt, sub, :, cols] = acc + (prev - acc) * mu_ref[:, cols]
                else:
                    g0 = c0 - ATT_COLS - RWKV_COLS
                    out = _sigmoid(acc + gbias_ref[:, g0:g0 + c1 - c0]).astype(BF)
                    for bi in range(nb):
                        gate_ref[bi, rows_out, g0:g0 + c1 - c0] = out[bi * c:(bi + 1) * c, :]
            return run, c1

        pieces = []
        col = 0
        while col < IN_COLS:
            run, col = proj_piece(col)
            pieces.append(run)
        pieces = iter(pieces)

        def fill(count=1):
            for _ in range(count):
                run = next(pieces, None)
                if run is not None:
                    run()

        fill()

        ps = p_scr[cur, sub]
        r = ps[:, 0:W_B]
        k = ps[:, W_B:2 * W_B]
        v = ps[:, 2 * W_B:3 * W_B]
        lora_in = ps[:, 3 * W_B:3 * W_B + DECAY_LORA + A_LORA]
        cg = ps[:, 3 * W_B + DECAY_LORA + A_LORA:]

        u = w0_ref[...] + _mm(jnp.tanh(lora_in), w2_ref[...])
        lw = -(math.exp(-0.5) * LOG2E) * _sigmoid(u)
        a = _sigmoid(a0_ref[...] + _mm(lora_in, a2_ref[...]))
        g = _mm(_sigmoid(cg), g2_ref[...])
        fill()

        bd = bd_ref[...]
        kk = k * kk_ref[...]
        kk = kk * lax.rsqrt(jnp.maximum(_seg_sum(kk * kk, bd), 1e-24))
        fill()
        k = k * (1.0 + (a - 1.0) * ka_ref[...])
        aa = -kk
        bb = kk * a

        tri = tri_ref[...]
        cum = sum(jnp.dot(tri, part, preferred_element_type=F32) for part in _split2(lw))
        fill()
        last = jnp.concatenate(
            [jnp.broadcast_to(cum[(bi + 1) * c - 1:(bi + 1) * c, :], (c, W_B)) for bi in range(nb)],
            axis=0)
        inv = jnp.exp2(-cum)
        w_end = jnp.exp2(last)
        tail = w_end * inv
        at = aa * jnp.exp2(cum - lw)
        bt = bb * inv
        kt = k * inv
        rt = r * jnp.exp2(cum)
        bh = bb * tail
        kh = k * tail
        bonus = _seg_sum(r * k * rk_ref[...], bd) * v

        lane = lax.broadcasted_iota(jnp.int32, (c, slab), 1)
        lane2 = lax.broadcasted_iota(jnp.int32, (2 * c, slab), 1)
        row2 = lax.broadcasted_iota(jnp.int32, (2 * c, slab), 0)
        half = [lane < HEAD, lane >= HEAD]
        half2 = [lane2 < HEAD, lane2 >= HEAD]
        rl = row2 & (c - 1)
        cl = lane2 & (c - 1)
        mask4 = jnp.logical_or(rl > cl, jnp.logical_and(row2 >= c, rl == cl))

        chains = [(bi, pp, e) for bi in range(nb) for pp in range(N_HEAD // 2) for e in range(2)]
        n = range(len(chains))

        def blk(x, ch):
            bi, pp, _ = ch
            return x[bi * c:(bi + 1) * c, pp * slab:(pp + 1) * slab]

        ar = [jnp.concatenate([blk(at, ch), blk(rt, ch)], axis=0) for ch in chains]
        bk = [jnp.concatenate([blk(bt, ch), blk(kt, ch)], axis=0).astype(BF) for ch in chains]
        bkh = [jnp.concatenate([blk(bh, ch), blk(kh, ch)], axis=0).astype(BF) for ch in chains]
        vsw = [pltpu.roll(blk(v, ch), HEAD, 1) for ch in chains]
        vx = [jnp.where(half[1 - chains[i][2]], vsw[i], 0.0).astype(BF) for i in n]
        a4 = [jnp.where(mask4,
                        _mm_nt(jnp.where(half2[chains[i][2]], ar[i], 0.0), bk[i]), 0.0).astype(BF)
              for i in n]
        ta = [jnp.where(half[chains[i][2]], blk(at, chains[i]), 0.0) for i in n]
        pw = [a4[i][0:c, :] for i in n]
        zeros_half = jnp.zeros((c, slab), BF)
        n_step = int(math.log2(c))
        for step in range(n_step):
            if step + 1 < n_step:
                rhs = [jnp.concatenate(
                    [jnp.concatenate([ta[i].astype(BF), pw[i]], axis=1),
                     jnp.concatenate([vx[i], zeros_half], axis=1)], axis=0) for i in n]
            else:
                rhs = [jnp.concatenate([ta[i].astype(BF), vx[i]], axis=0) for i in n]
            both = [jnp.dot(pw[i], rhs[i], preferred_element_type=F32) for i in n]
            ta = [ta[i] + both[i][:, 0:slab] for i in n]
            if step + 1 < n_step:
                pw = [both[i][:, slab:2 * slab].astype(BF) for i in n]
        gm = [jnp.concatenate(
            [jnp.where(half[chains[i][2]], blk(rt, chains[i]), 0.0), ta[i]], axis=0) for i in n]
        fill(5)
        s0 = [state_ref[ch[0], 2 * ch[1] + ch[2]] for ch in chains]
        ys = [_mm_nt(gm[i], s0[i]) for i in n]
        uv = [jnp.concatenate([(ys[i][c:2 * c, :] + ta[i]).astype(BF), vx[i]], axis=0) for i in n]
        fill(5)
        yo = [ys[i][0:c, :] + jnp.dot(a4[i][c:2 * c, :], uv[i], preferred_element_type=F32)
              for i in n]
        for i in n:
            bi, pp, e = chains[i]
            upd = lax.dot_general(uv[i], bkh[i], (((0,), (0,)), ((), ())),
                                  preferred_element_type=F32)
            keep = jnp.logical_and(half2[e], (row2 >= HEAD) if e == 0 else (row2 < HEAD))
            state_ref[bi, 2 * pp + e] = jnp.where(keep, s0[i] * blk(w_end, chains[i])[0:1, :] + upd, 0.0)
        for bi in range(nb):
            for pp in range(N_HEAD // 2):
                i0 = chains.index((bi, pp, 0))
                both = jnp.where(half[1], yo[i0], yo[i0 + 1])
                y_scr[sub, bi * c:(bi + 1) * c, pp * slab:(pp + 1) * slab] = pltpu.roll(both, HEAD, 1)
        fill(2)

        y = y_scr[sub]
        yc = y - _seg_sum(y, bd) * (1.0 / HEAD)
        fill()
        var = _seg_sum(yc * yc, bd) * (1.0 / HEAD)
        fill(IN_COLS // PROJ_PIECE)
        yn = yc * lax.rsqrt(var + GN_EPS) * lnw_ref[...] + lnb_ref[...]
        out = ((yn + bonus) * g).astype(BF)
        for bi in range(nb):
            o_ref[bi, rows_out, :] = out[bi * c:(bi + 1) * c, :]

        xs = jnp.concatenate([x_ref[bi, rows_out, :] for bi in range(nb)], axis=0)
        h_scr[sub] = _rms(xs, gpre_ref[...]).astype(BF)

    for sub in range(nsub):
        chunk_step(sub)


def _mix(x3, g_pre, w_bf, gate_bias, prm):
    b, s, _ = x3.shape
    tt = RWKV_TILE
    nstep = s // tt
    const2 = lambda t: (0, 0)
    vec = pl.BlockSpec((1, W_B), const2)
    lora = DECAY_LORA + A_LORA
    rows = jnp.arange(b * CHUNK)
    tri = jnp.logical_and(rows[:, None] >= rows[None, :],
                          rows[:, None] // CHUNK == rows[None, :] // CHUNK).astype(BF)
    this = lambda t: (0, jnp.minimum(t, nstep - 1), 0)
    ahead = lambda t: (0, jnp.minimum(t + 1, nstep - 1), 0)
    qkv, gates, yb = pl.pallas_call(
        _mix_kernel,
        grid=(nstep + 1,),
        in_specs=[
            pl.BlockSpec((b, tt, D_MODEL), lambda t: (0, 0, 0)),
            pl.BlockSpec((b, tt, D_MODEL), ahead),
            pl.BlockSpec((1, D_MODEL), const2),
            pl.BlockSpec((D_MODEL, IN_COLS), const2),
            pl.BlockSpec((1, GATE_COLS), const2),
            pl.BlockSpec((1, RWKV_COLS), const2),
            vec,
            pl.BlockSpec((lora, W_B), const2),
            vec,
            pl.BlockSpec((lora, W_B), const2),
            pl.BlockSpec((GATE_LORA, W_B), const2),
            vec, vec, vec, vec, vec,
            pl.BlockSpec((SEG_W, SEG_W), const2),
            pl.BlockSpec((b * CHUNK, b * CHUNK), const2),
        ],
        out_specs=[
            pl.BlockSpec((b, tt, ATT_COLS), this),
            pl.BlockSpec((b, tt, GATE_COLS), this),
            pl.BlockSpec((b, tt, W_B), lambda t: (0, jnp.maximum(t - 1, 0), 0)),
        ],
        out_shape=[
            jax.ShapeDtypeStruct((b, s, ATT_COLS), BF),
            jax.ShapeDtypeStruct((b, s, GATE_COLS), BF),
            jax.ShapeDtypeStruct((b, s, W_B), BF),
        ],
        scratch_shapes=[pltpu.VMEM((tt // CHUNK, b * CHUNK, D_MODEL), BF),
                        pltpu.VMEM((2, tt // CHUNK, b * CHUNK, RWKV_COLS), F32),
                        pltpu.VMEM((b, 1, RWKV_COLS), F32),
                        pltpu.VMEM((b, N_HEAD, 2 * HEAD, 2 * HEAD), F32),
                        pltpu.VMEM((tt // CHUNK, b * CHUNK, W_B), F32)],
        compiler_params=pltpu.CompilerParams(
            dimension_semantics=("arbitrary",), vmem_limit_bytes=VMEM_LIMIT),
        name="inproj_rwkv7",
    )(x3, x3, g_pre, w_bf, gate_bias, prm["mu"], prm["w0"], prm["w2"], prm["a0"], prm["a2"],
      prm["g2"], prm["k_k"], prm["k_a"], prm["r_k"], prm["ln_w"], prm["ln_b"], prm["bd"], tri)
    n = b * s
    return qkv.reshape(n, ATT_COLS), gates.reshape(n, GATE_COLS), yb.reshape(n, W_B)


def _attn_kernel(q_ref, kp_ref, kc_ref, vp_ref, vc_ref, base_ref, o_ref, kwin, vwin, bias_scr):
    i = pl.program_id(1)

    @pl.when(jnp.logical_and(pl.program_id(0) == 0, i == 0))
    def _():
        qi = lax.broadcasted_iota(jnp.int32, (ATT_Q_SUB, ATT_K_WIN), 0)
        kj = lax.broadcasted_iota(jnp.int32, (ATT_Q_SUB, ATT_K_WIN), 1)
        dchunk = (BAND_PREV + qi // CHUNK) - kj // CHUNK
        band = jnp.logical_and(dchunk >= 0, dchunk <= BAND_PREV)
        for h in range(N_HEAD):
            rows = jnp.broadcast_to(base_ref[h:h + 1, :], (ATT_Q_SUB, ATT_BASE))
            toep = pltpu.roll(rows, 0, 1, stride=1, stride_axis=0)
            bias_scr[h] = jnp.where(band, toep[:, 0:ATT_K_WIN] * LOG2E, NEG_INF)

    kwin[0:ROW_TILE, :] = kp_ref[...]
    kwin[ROW_TILE:2 * ROW_TILE, :] = kc_ref[...]
    vwin[0:ROW_TILE, :] = vp_ref[...]
    vwin[ROW_TILE:2 * ROW_TILE, :] = vc_ref[...]
    lane = lax.broadcasted_iota(jnp.int32, (ATT_Q_SUB, 2 * HEAD), 1)
    col = lax.broadcasted_iota(jnp.int32, (ATT_Q_SUB, ATT_K_WIN), 1)

    def block(first):
        for j in range(ROW_TILE // ATT_Q_SUB):
            r0 = j * ATT_Q_SUB
            hs = range(N_HEAD)
            cs = [(h // 2) * 2 * HEAD for h in hs]
            ss = []
            for h in hs:
                q2 = q_ref[r0:r0 + ATT_Q_SUB, cs[h]:cs[h] + 2 * HEAD]
                in_head = (lane >= HEAD) if h % 2 else (lane < HEAD)
                qm = jnp.where(in_head, q2, jnp.zeros_like(q2))
                ss.append(lax.dot_general(qm, kwin[r0:r0 + ATT_K_WIN, cs[h]:cs[h] + 2 * HEAD],
                                          (((1,), (1,)), ((), ())), preferred_element_type=F32))
            exs, ls = [], []
            for h in hs:
                s = ss[h] + bias_scr[h]
                if first:
                    s = jnp.where(col >= ROW_TILE - r0, s, NEG_INF)
                m = jnp.max(s, axis=-1, keepdims=True)
                ex = jnp.exp2(s - m)
                ls.append(jnp.sum(ex, axis=-1, keepdims=True))
                exs.append(ex.astype(BF))
            outs = [jnp.dot(exs[h], vwin[r0:r0 + ATT_K_WIN, cs[h]:cs[h] + 2 * HEAD],
                            preferred_element_type=F32) / ls[h] for h in hs]
            for p in range(N_HEAD // 2):
                o = jnp.where(lane < HEAD, outs[2 * p], outs[2 * p + 1])
                o_ref[r0:r0 + ATT_Q_SUB, cs[2 * p]:cs[2 * p] + 2 * HEAD] = o.astype(BF)

    pl.when(i == 0)(lambda: block(True))
    pl.when(i > 0)(lambda: block(False))


def _attn_bias_base(rel_bias):
    pos = jnp.arange(ATT_BASE)
    d = jnp.where(pos < ATT_K_WIN, pos, pos - ATT_BASE)
    idx = jnp.clip(BAND_PREV * CHUNK - d, -(CHUNK - 1), REL_CLIP) + (CHUNK - 1)
    return rel_bias.astype(F32)[:, idx]


def _attention(qkv, bias_base, b, s):
    n = b * s
    nblk = s // ROW_TILE
    blk = (ROW_TILE, W_A)
    return pl.pallas_call(
        _attn_kernel,
        grid=(b, nblk),
        in_specs=[
            pl.BlockSpec(blk, lambda bi, i: (bi * nblk + i, 0)),
            pl.BlockSpec(blk, lambda bi, i: (bi * nblk + jnp.maximum(i - 1, 0), 1)),
            pl.BlockSpec(blk, lambda bi, i: (bi * nblk + i, 1)),
            pl.BlockSpec(blk, lambda bi, i: (bi * nblk + jnp.maximum(i - 1, 0), 2)),
            pl.BlockSpec(blk, lambda bi, i: (bi * nblk + i, 2)),
            pl.BlockSpec((N_HEAD, ATT_BASE), lambda bi, i: (0, 0)),
        ],
        out_specs=pl.BlockSpec(blk, lambda bi, i: (bi * nblk + i, 0)),
        out_shape=jax.ShapeDtypeStruct((n, W_A), BF),
        scratch_shapes=[pltpu.VMEM((2 * ROW_TILE, W_A), BF),
                        pltpu.VMEM((2 * ROW_TILE, W_A), BF),
                        pltpu.VMEM((N_HEAD, ATT_Q_SUB, ATT_K_WIN), F32)],
        compiler_params=pltpu.CompilerParams(
            dimension_semantics=("arbitrary", "arbitrary"), vmem_limit_bytes=VMEM_LIMIT),
        name="band_attn",
    )(qkv, qkv, qkv, qkv, qkv, bias_base)


FF_TILE = 1024


def _tail_kernel(ya_ref, yb_ref, gate_ref, x_ref, pa_ref, pb_ref, wo_ref, gmix_ref,
                 g1_ref, wu_ref, wd_ref, g2_ref, o_ref):
    half = ROW_TILE // 2
    rows = [slice(r0, r0 + half) for r0 in range(0, ROW_TILE, half)]
    z = []
    for rs in rows:
        ma = jnp.dot(ya_ref[rs, :], pa_ref[...], preferred_element_type=F32)
        mb = jnp.dot(yb_ref[rs, :], pb_ref[...], preferred_element_type=F32)
        merged = (gate_ref[rs, 0:D_MODEL].astype(F32) * ma
                  + gate_ref[rs, D_MODEL:2 * D_MODEL].astype(F32) * mb)
        z.append(jnp.dot(merged.astype(BF), wo_ref[...], preferred_element_type=F32))
    x = [x_ref[rs, :] + _rms(z[i], gmix_ref[...]) for i, rs in enumerate(rows)]
    hf = [_rms(xi, g1_ref[...]).astype(BF) for xi in x]
    acc = [jnp.zeros(xi.shape, F32) for xi in x]
    for c in range(0, D_FF, FF_TILE):
        u = [jnp.dot(h, wu_ref[:, c:c + FF_TILE], preferred_element_type=F32) for h in hf]
        u = [jnp.maximum(ui, 0.0) for ui in u]
        u = [(ui * ui).astype(BF) for ui in u]
        acc = [a + jnp.dot(ui, wd_ref[c:c + FF_TILE, :], preferred_element_type=F32)
               for a, ui in zip(acc, u)]
    for i, rs in enumerate(rows):
        o_ref[rs, :] = x[i] + _rms(acc[i], g2_ref[...])


def _tail(ya, yb, gates, x2, pa, pb, wo, gmix, g1, wu, wd, g2):
    n = x2.shape[0]
    row = lambda i: (i, 0)

    def resident(shape):
        return pl.BlockSpec(shape, lambda i: (0, 0), pipeline_mode=pl.Buffered(1))

    return pl.pallas_call(
        _tail_kernel,
        grid=(n // ROW_TILE,),
        in_specs=[
            pl.BlockSpec((ROW_TILE, W_A), row),
            pl.BlockSpec((ROW_TILE, W_B), row),
            pl.BlockSpec((ROW_TILE, GATE_COLS), row),
            pl.BlockSpec((ROW_TILE, D_MODEL), row),
            resident((W_A, D_MODEL)),
            resident((W_B, D_MODEL)),
            resident((D_MODEL, D_MODEL)),
            resident((1, D_MODEL)),
            resident((1, D_MODEL)),
            resident((D_MODEL, D_FF)),
            resident((D_FF, D_MODEL)),
            resident((1, D_MODEL)),
        ],
        out_specs=pl.BlockSpec((ROW_TILE, D_MODEL), row),
        out_shape=jax.ShapeDtypeStruct((n, D_MODEL), F32),
        compiler_params=pltpu.CompilerParams(
            dimension_semantics=("arbitrary",), vmem_limit_bytes=VMEM_LIMIT),
        name="merge_ffn",
    )(ya, yb, gates, x2, pa, pb, wo, gmix, g1, wu, wd, g2)


def _layer(x2, b, s, lp):
    row = lambda a: a.reshape(1, -1).astype(F32)
    zeros_lora = jnp.zeros((DECAY_LORA, W_B), BF)
    blk = jnp.arange(SEG_W) // HEAD
    prm = {
        "mu": row(lp["shift_mu"]),
        "w0": row(lp["w0"]),
        "w2": jnp.concatenate([lp["w2"].astype(BF), zeros_lora], axis=0),
        "a0": row(lp["a0"]),
        "a2": jnp.concatenate([zeros_lora, lp["a2"].astype(BF)], axis=0),
        "g2": lp["g2"].astype(BF),
        "k_k": row(lp["k_k"]),
        "k_a": row(lp["k_a"]),
        "r_k": row(lp["r_k"]),
        "ln_w": row(lp["ln_x_w"]),
        "ln_b": row(lp["ln_x_b"]),
        "bd": (blk[:, None] == blk[None, :]).astype(BF),
    }
    qkv, gates, yb = _mix(x2.reshape(b, s, D_MODEL), row(lp["pre_mix_g"]),
                          lp["w_in"].astype(BF), row(lp["gate_bias"]), prm)
    ya = _attention(qkv, _attn_bias_base(lp["rel_bias"]), b, s)

    return _tail(ya, yb, gates, x2, lp["proj_a"].astype(BF), lp["proj_b"].astype(BF),
                 lp["w_out"].astype(BF), row(lp["post_mix_g"]), row(lp["pre_ffn_g"]),
                 lp["w_up"].astype(BF), lp["w_down"].astype(BF), row(lp["post_ffn_g"]))


@jax.jit
def _forward(x, params):
    b, s, d = x.shape
    assert d == D_MODEL and s % ROW_TILE == 0 and s % RWKV_TILE == 0
    x2 = x.reshape(b * s, d)
    depth = params["w_in"].shape[0]
    for l in range(depth):
        lp = {name: val[l] for name, val in params.items()}
        x2 = _layer(x2, b, s, lp)
    return x2.reshape(b, s, d)


def kernel(x, pre_mix_g, w_in, gate_bias, rel_bias, shift_mu, w0, w2, a0, a2, g2, k_k, k_a, r_k, ln_x_w, ln_x_b, proj_a, proj_b, w_out, post_mix_g, pre_ffn_g, w_up, w_down, post_ffn_g):
    params = dict(pre_mix_g=pre_mix_g, w_in=w_in, gate_bias=gate_bias, rel_bias=rel_bias,
                  shift_mu=shift_mu, w0=w0, w2=w2, a0=a0, a2=a2, g2=g2, k_k=k_k, k_a=k_a,
                  r_k=r_k, ln_x_w=ln_x_w, ln_x_b=ln_x_b, proj_a=proj_a, proj_b=proj_b,
                  w_out=w_out, post_mix_g=post_mix_g, pre_ffn_g=pre_ffn_g, w_up=w_up,
                  w_down=w_down, post_ffn_g=post_ffn_g)
    return _forward(x, params)
```
